```python
import math
import jax
import jax.numpy as jnp
from jax import lax
import numpy as np

D_MODEL = 1024
BATCH = 16
SEQ = 256
DEPTH = 4
DEC_BATCH = 4
DEC_SEQ = 4096
PAST_LEN = 512

GRID_W = 64
Q_BLOCK = 128
ROPE_BASE = 10000.0
EPS = 1e-6

HEAD_DIM = 64
GROUP_WIDTH = 256
D_MIX = 4 * GROUP_WIDTH

GQA_HEADS = 4
GQA_KV_HEADS = 2
MLA_HEADS = 4
MLA_Q_LORA = 192
MLA_KV_LORA = 128
MLA_NOPE = 64
MLA_ROPE = 32
MLA_QK = MLA_NOPE + MLA_ROPE
MLA_V = 64
DIFF_HEADS = 4
DIFF_QK = 32
DIFF_V = 64
RWKV_HEADS = 4
RWKV_N = 64
RWKV_W = RWKV_HEADS * RWKV_N
DECAY_LORA = 32
AAA_LORA = 32
GATE_LORA = 64
RWKV_GN_EPS = 64e-5
RWKV_COLS = 3 * RWKV_W + 2 * DECAY_LORA + 2 * AAA_LORA + GATE_LORA
N_EXPERTS = 32
TOP_K = 4
D_FF = 1024
SWIGLU_ALPHA = 1.702
SWIGLU_LIMIT = 7.0
MOE_BLOCK = 128

IN_SPLITS = (
    GQA_HEADS * HEAD_DIM, GQA_KV_HEADS * HEAD_DIM, GQA_KV_HEADS * HEAD_DIM,
    MLA_Q_LORA, MLA_KV_LORA, MLA_ROPE,
    DIFF_HEADS * 2 * DIFF_QK, DIFF_HEADS * 2 * DIFF_QK, DIFF_HEADS * DIFF_V,
    RWKV_COLS,
)
IN_COLS = sum(IN_SPLITS)

kernel_name = 'hybrid_diffusion_trunk_step'


def rms_norm(x, g, eps=EPS):
    xf = x.astype(jnp.float32)
    y = xf * lax.rsqrt(jnp.mean(xf * xf, axis=-1, keepdims=True) + eps)
    return (y * g.astype(jnp.float32)).astype(x.dtype)


def to_heads(x, n_heads):
    b, t, _ = x.shape
    return x.reshape(b, t, n_heads, -1).transpose(0, 2, 1, 3)


def merge_heads(x):
    b, h, t, d = x.shape
    return x.transpose(0, 2, 1, 3).reshape(b, t, h * d)


def split_cols(x, sizes):
    return jnp.split(x, np.cumsum(sizes)[:-1].tolist(), axis=-1)


def axial_rope_tables(rows, rot_dim, dtype):
    row = jnp.repeat(jnp.arange(rows, dtype=jnp.float32), GRID_W)
    col = (jnp.arange(rows * GRID_W) % GRID_W).astype(jnp.float32)
    n_freq = rot_dim // 4
    inv_freq = ROPE_BASE ** (-jnp.arange(n_freq, dtype=jnp.float32) / n_freq)
    ang = jnp.stack([row[:, None] * inv_freq, col[:, None] * inv_freq], axis=1)
    return jnp.cos(ang).astype(dtype), jnp.sin(ang).astype(dtype)


def apply_axial_rope(x, cos, sin):
    shp = x.shape
    xs = x.reshape(shp[:-1] + (2, 2, shp[-1] // 4))
    x1, x2 = xs[..., 0, :], xs[..., 1, :]
    out = jnp.stack([x1 * cos - x2 * sin, x2 * cos + x1 * sin], axis=-2)
    return out.reshape(shp)


def rope_tail(x, cos, sin, rot_dim):
    return jnp.concatenate([x[..., :-rot_dim], apply_axial_rope(x[..., -rot_dim:], cos, sin)], axis=-1)


def sweep_query_blocks(block_fn, qs):
    b, h, t, _ = qs[0].shape
    nb = t // Q_BLOCK
    blocks = tuple(jnp.moveaxis(q.reshape(b, h, nb, Q_BLOCK, q.shape[-1]), 2, 0) for q in qs)
    out = lax.map(lambda qb: block_fn(*qb), blocks)
    return jnp.moveaxis(out, 0, 2).reshape(b, h, t, out.shape[-1])


def gqa_attention(q, k, v):
    b, hq, _, d = q.shape
    hkv = k.shape[1]
    g = hq // hkv
    scale = d ** -0.5

    def block(qb):
        qb = qb.reshape(b, hkv, g, Q_BLOCK, d)
        s = jnp.einsum('bhgqd,bhkd->bhgqk', qb, k).astype(jnp.float32) * scale
        p = jax.nn.softmax(s, axis=-1).astype(v.dtype)
        return jnp.einsum('bhgqk,bhkd->bhgqd', p, v).reshape(b, hq, Q_BLOCK, v.shape[-1])

    return sweep_query_blocks(block, (q,))


def diff_lambda(lam_vecs, lam_init):
    lq1, lk1, lq2, lk2 = lam_vecs.astype(jnp.float32)
    return jnp.exp(jnp.sum(lq1 * lk1)) - jnp.exp(jnp.sum(lq2 * lk2)) + lam_init


def diff_attention(q, k, v, lam, subln, lam_init):
    q1, q2 = q[:, :, 0], q[:, :, 1]
    k1, k2 = k[:, :, 0], k[:, :, 1]
    scale = DIFF_QK ** -0.5

    def block(a1, a2):
        s1 = jnp.einsum('bhqd,bhkd->bhqk', a1, k1).astype(jnp.float32) * scale
        s2 = jnp.einsum('bhqd,bhkd->bhqk', a2, k2).astype(jnp.float32) * scale
        p = jax.nn.softmax(s1, axis=-1) - lam * jax.nn.softmax(s2, axis=-1)
        return jnp.einsum('bhqk,bhkd->bhqd', p.astype(v.dtype), v)

    o = sweep_query_blocks(block, (q1, q2))
    return merge_heads(rms_norm(o, subln) * (1.0 - lam_init))


def gqa_qkv(pq, pk, pv, L):
    q = rms_norm(to_heads(pq, GQA_HEADS), L['gqa_qn'])
    k = rms_norm(to_heads(pk, GQA_KV_HEADS), L['gqa_kn'])
    return q, k, to_heads(pv, GQA_KV_HEADS)


def mla_queries(p_cq, L):
    q = to_heads(rms_norm(p_cq, L['mla_qa_norm']) @ L['mla_q_up'], MLA_HEADS)
    return rms_norm(q, L['mla_qn'])


def mla_keys_values(ckv, krope, L):
    kv = to_heads(ckv @ L['mla_kv_up'], MLA_HEADS)
    k_nope, v = kv[..., :MLA_NOPE], kv[..., MLA_NOPE:]
    kr = jnp.broadcast_to(krope[:, None], k_nope.shape[:3] + (MLA_ROPE,))
    return rms_norm(jnp.concatenate([k_nope, kr], axis=-1), L['mla_kn']), v


def diff_qkv(pq, pk, pv, L):
    b, t, _ = pq.shape
    q = rms_norm(pq.reshape(b, t, DIFF_HEADS, 2, DIFF_QK), L['diff_qn']).transpose(0, 2, 3, 1, 4)
    k = rms_norm(pk.reshape(b, t, DIFF_HEADS, 2, DIFF_QK), L['diff_kn']).transpose(0, 2, 3, 1, 4)
    return q, k, to_heads(pv, DIFF_HEADS)


def centred_token_shift(p, mix):
    prev = jnp.pad(p[:, :-1], ((0, 0), (1, 0), (0, 0)))
    nxt = jnp.pad(p[:, 1:], ((0, 0), (0, 1), (0, 0)))
    return p + (prev - p) * mix[0] + (nxt - p) * mix[1]


def rwkv_scan(state0, r, w, k, v, kk, b, reverse):
    def step(s, inp):
        r_t, w_t, k_t, v_t, kk_t, b_t = inp
        sa = jnp.einsum('bhvk,bhk->bhv', s, kk_t)
        s = s * w_t[:, :, None, :] - sa[..., None] * b_t[:, :, None, :] + v_t[..., None] * k_t[:, :, None, :]
        return s, jnp.einsum('bhvk,bhk->bhv', s, r_t)

    seq = tuple(jnp.moveaxis(t, 1, 0) for t in (r, w, k, v, kk, b))
    s_final, y = lax.scan(step, state0, seq, reverse=reverse)
    return s_final, jnp.moveaxis(y, 0, 1)


def rwkv_time_mix(p, state0, L):
    b, t, _ = p.shape
    f32 = jnp.float32
    xs = centred_token_shift(p, L['rwkv_mix'])
    r, k, v, wl, al, gl = split_cols(xs, (RWKV_W, RWKV_W, RWKV_W, 2 * DECAY_LORA, 2 * AAA_LORA, GATE_LORA))
    wl = jnp.tanh(wl.reshape(b, t, 2, DECAY_LORA))
    al = al.reshape(b, t, 2, AAA_LORA)
    w_log = -jax.nn.softplus(-(L['rwkv_w0'] + jnp.einsum('btdl,dlc->btdc', wl, L['rwkv_w2'])).astype(f32)) - 0.5
    decay = jnp.exp(-jnp.exp(w_log))
    a = jax.nn.sigmoid((L['rwkv_a0'] + jnp.einsum('btdl,dlc->btdc', al, L['rwkv_a2'])).astype(f32))
    g = jax.nn.sigmoid(gl) @ L['rwkv_g2']
    hv = lambda z: z.reshape(z.shape[:-1] + (RWKV_HEADS, RWKV_N))
    r32, k32, v32 = r.astype(f32), k.astype(f32), v.astype(f32)
    kk = hv(k32 * L['rwkv_kk'].astype(f32))
    kk = kk * lax.rsqrt(jnp.sum(kk * kk, axis=-1, keepdims=True) + 1e-12)
    k_dir = k32[:, :, None] * (1.0 + (a - 1.0) * L['rwkv_ka'].astype(f32))
    r_h, v_h = hv(r32), hv(v32)
    ys, states, bonus = [], [], []
    for d in range(2):
        kd = hv(k_dir[:, :, d])
        s_d, y_d = rwkv_scan(state0[:, d], r_h, hv(decay[:, :, d]), kd, v_h, kk, kk * hv(a[:, :, d]), d == 1)
        ys.append(y_d)
        states.append(s_d)
        bonus.append(jnp.sum(r_h * kd * L['rwkv_rk'][d].astype(f32), axis=-1, keepdims=True) * v_h)
    y = ys[0] + ys[1]
    mu = jnp.mean(y, axis=-1, keepdims=True)
    var = jnp.mean(jnp.square(y - mu), axis=-1, keepdims=True)
    yn = ((y - mu) * lax.rsqrt(var + RWKV_GN_EPS)).reshape(b, t, RWKV_W)
    yn = yn * L['rwkv_ln_g'].astype(f32) + L['rwkv_ln_b'].astype(f32)
    out = (yn + (bonus[0] + bonus[1]).reshape(b, t, RWKV_W)) * g.astype(f32)
    return out.astype(p.dtype), jnp.stack(states, axis=1)


def moe_ffn(h, router_w, router_b, w1, b1, w2, b2):
    bsz, t, d = h.shape
    x = h.reshape(-1, d)
    n = x.shape[0]
    logits = (x @ router_w).astype(jnp.float32) + router_b.astype(jnp.float32)
    top_val, top_idx = lax.top_k(logits, TOP_K)
    gates = jax.nn.softmax(top_val, axis=-1).astype(x.dtype)
    flat_e = top_idx.reshape(-1)
    nk = n * TOP_K
    order = jnp.argsort(flat_e)
    sorted_e = flat_e[order]
    counts = jnp.bincount(flat_e, length=N_EXPERTS)
    padded = (counts + MOE_BLOCK - 1) // MOE_BLOCK * MOE_BLOCK
    start = jnp.cumsum(counts) - counts
    pad_end = jnp.cumsum(padded)
    pad_start = pad_end - padded
    dest = pad_start[sorted_e] + jnp.arange(nk) - start[sorted_e]
    n_blocks = (nk + N_EXPERTS * (MOE_BLOCK - 1) + MOE_BLOCK - 1) // MOE_BLOCK
    cap = n_blocks * MOE_BLOCK
    buf_tok = jnp.full((cap,), n, jnp.int32).at[dest].set((order // TOP_K).astype(jnp.int32))
    block_e = jnp.minimum(jnp.searchsorted(pad_end, jnp.arange(n_blocks) * MOE_BLOCK, side='right'), N_EXPERTS - 1)
    x_pad = jnp.concatenate([x, jnp.zeros((1, d), x.dtype)], axis=0)
    xb = x_pad[buf_tok].reshape(n_blocks, MOE_BLOCK, d)

    def expert_block(args):
        xblk, e = args
        hcat = xblk @ w1[e] + b1[e]
        glu = jnp.minimum(hcat[:, ::2], SWIGLU_LIMIT)
        lin = jnp.clip(hcat[:, 1::2], -SWIGLU_LIMIT, SWIGLU_LIMIT)
        act = glu * jax.nn.sigmoid(SWIGLU_ALPHA * glu) * (lin + 1.0)
        return act @ w2[e] + b2[e]

    yb = lax.map(expert_block, (xb, block_e)).reshape(cap, d)
    slot_dest = jnp.zeros((nk,), dest.dtype).at[order].set(dest)
    y = jnp.einsum('nkd,nk->nd', yb[slot_dest].reshape(n, TOP_K, d), gates)
    return y.reshape(bsz, t, d)


def adaln(cond, w, b):
    m = jax.nn.silu(cond) @ w + b
    return jnp.split(m[:, None, :], 6, axis=-1)


def mixer_inputs(x, mods, L):
    h = rms_norm(x, L['norm1_g']) * (1.0 + mods[1]) + mods[0]
    return split_cols(h @ L['w_in'], IN_SPLITS)


def finish_layer(x, mixed, mods, L):
    x = x + mods[2] * (jnp.concatenate(mixed, axis=-1) @ L['w_out'])
    h = rms_norm(x, L['norm2_g']) * (1.0 + mods[4]) + mods[3]
    return x + mods[5] * moe_ffn(h, L['router_w'], L['router_b'], L['moe_w1'], L['moe_b1'], L['moe_w2'], L['moe_b2'])


def context_layer(x, mods, L, lam_init):
    a_q, a_k, a_v, b_cq, b_ckv, b_kr, c_q, c_k, c_v, d_p = mixer_inputs(x, mods, L)
    qa, ka, va = gqa_qkv(a_q, a_k, a_v, L)
    o_a = merge_heads(gqa_attention(qa, ka, va))
    ckv = rms_norm(b_ckv, L['mla_kva_norm'])
    kb, vb = mla_keys_values(ckv, b_kr, L)
    o_b = merge_heads(gqa_attention(mla_queries(b_cq, L), kb, vb))
    qc, kc, vc = diff_qkv(c_q, c_k, c_v, L)
    o_c = diff_attention(qc, kc, vc, diff_lambda(L['diff_lam'], lam_init), L['diff_subln'], lam_init)
    zero_state = jnp.zeros((x.shape[0], 2, RWKV_HEADS, RWKV_N, RWKV_N), jnp.float32)
    o_d, s_d = rwkv_time_mix(d_p, zero_state, L)
    x = finish_layer(x, (o_a, o_b, o_c, o_d), mods, L)
    return x, (ka, va, ckv, b_kr, kc, vc, s_d.astype(x.dtype))


def latent_layer(x, mods, L, lam_init, ctx, ropes):
    ctx_ka, ctx_va, ctx_ckv, ctx_kr, ctx_kc, ctx_vc, ctx_s = ctx
    cos_a, sin_a, cos_m, sin_m, cos_c, sin_c = ropes
    a_q, a_k, a_v, b_cq, b_ckv, b_kr, c_q, c_k, c_v, d_p = mixer_inputs(x, mods, L)
    qa, ka, va = gqa_qkv(a_q, a_k, a_v, L)
    qa = apply_axial_rope(qa, cos_a, sin_a)
    ka = apply_axial_rope(ka, cos_a, sin_a)
    o_a = merge_heads(gqa_attention(qa, jnp.concatenate([ctx_ka, ka], axis=2), jnp.concatenate([ctx_va, va], axis=2)))
    qb = rope_tail(mla_queries(b_cq, L), cos_m, sin_m, MLA_ROPE)
    kb, vb = mla_keys_values(rms_norm(b_ckv, L['mla_kva_norm']), b_kr, L)
    kb = rope_tail(kb, cos_m, sin_m, MLA_ROPE)
    kb_ctx, vb_ctx = mla_keys_values(ctx_ckv, ctx_kr, L)
    o_b = merge_heads(gqa_attention(qb, jnp.concatenate([kb_ctx, kb], axis=2), jnp.concatenate([vb_ctx, vb], axis=2)))
    qc, kc, vc = diff_qkv(c_q, c_k, c_v, L)
    qc = apply_axial_rope(qc, cos_c, sin_c)
    kc = apply_axial_rope(kc, cos_c, sin_c)
    o_c = diff_attention(qc, jnp.concatenate([ctx_kc, kc], axis=3), jnp.concatenate([ctx_vc, vc], axis=2),
                         diff_lambda(L['diff_lam'], lam_init), L['diff_subln'], lam_init)
    o_d, _ = rwkv_time_mix(d_p, ctx_s.astype(jnp.float32), L)
    return finish_layer(x, (o_a, o_b, o_c, o_d), mods, L)


def setup_inputs(seed: int = 0) -> dict:
    key = jax.random.key(seed)
    keys = iter(jax.random.split(key, 64))

    def normal(shape, scale):
        return jax.random.normal(next(keys), shape, jnp.float32) * scale

    def gain(shape, base=1.0):
        return base + normal(shape, 0.02)

    L = DEPTH
    return {
        'x_prompt': normal((BATCH, SEQ, D_MODEL), 1.0),
        'x_sample': normal((DEC_BATCH, DEC_SEQ, D_MODEL), 1.0),
        'cache_gqa_k': normal((DEC_BATCH, L, GQA_KV_HEADS, PAST_LEN, HEAD_DIM), 1.0),
        'cache_gqa_v': normal((DEC_BATCH, L, GQA_KV_HEADS, PAST_LEN, HEAD_DIM), 1.0),
        'cache_mla_ckv': normal((DEC_BATCH, L, PAST_LEN, MLA_KV_LORA), 1.0),
        'cache_mla_krope': normal((DEC_BATCH, L, PAST_LEN, MLA_ROPE), 1.0),
        'cache_diff_k': normal((DEC_BATCH, L, DIFF_HEADS, 2, PAST_LEN, DIFF_QK), 1.0),
        'cache_diff_v': normal((DEC_BATCH, L, DIFF_HEADS, PAST_LEN, DIFF_V), 1.0),
        'state_rwkv': normal((DEC_BATCH, L, 2, RWKV_HEADS, RWKV_N, RWKV_N), 0.5),
        'c': normal((DEC_BATCH, D_MODEL), 1.0),
        'c_ctx': normal((D_MODEL,), 1.0),
        'norm1_g': gain((L, D_MODEL)),
        'norm2_g': gain((L, D_MODEL)),
        'ada_w': normal((L, D_MODEL, 6 * D_MODEL), 0.5 * D_MODEL ** -0.5),
        'ada_b': normal((L, 6 * D_MODEL), 0.01),
        'w_in': normal((L, D_MODEL, IN_COLS), D_MODEL ** -0.5),
        'w_out': normal((L, D_MIX, D_MODEL), D_MIX ** -0.5),
        'gqa_qn': gain((L, HEAD_DIM)),
        'gqa_kn': gain((L, HEAD_DIM)),
        'mla_qa_norm': gain((L, MLA_Q_LORA)),
        'mla_q_up': normal((L, MLA_Q_LORA, MLA_HEADS * MLA_QK), MLA_Q_LORA ** -0.5),
        'mla_kva_norm': gain((L, MLA_KV_LORA)),
        'mla_kv_up': normal((L, MLA_KV_LORA, MLA_HEADS * (MLA_NOPE + MLA_V)), MLA_KV_LORA ** -0.5),
        'mla_qn': gain((L, MLA_QK)),
        'mla_kn': gain((L, MLA_QK)),
        'diff_qn': gain((L, DIFF_QK)),
        'diff_kn': gain((L, DIFF_QK)),
        'diff_lam': normal((L, 4, DIFF_QK), 0.1),
        'diff_subln': gain((L, DIFF_V)),
        'rwkv_mix': jax.random.uniform(next(keys), (L, 2, RWKV_COLS), jnp.float32, 0.0, 0.5),
        'rwkv_w0': normal((L, 2, RWKV_W), 0.5),
        'rwkv_w2': normal((L, 2, DECAY_LORA, RWKV_W), 0.5 * DECAY_LORA ** -0.5),
        'rwkv_a0': normal((L, 2, RWKV_W), 0.5),
        'rwkv_a2': normal((L, 2, AAA_LORA, RWKV_W), 0.5 * AAA_LORA ** -0.5),
        'rwkv_rk': normal((L, 2, RWKV_HEADS, RWKV_N), 0.1),
        'rwkv_g2': normal((L, GATE_LORA, RWKV_W), GATE_LORA ** -0.5),
        'rwkv_kk': gain((L, RWKV_W), 0.85),
        'rwkv_ka': gain((L, RWKV_W)),
        'rwkv_ln_g': gain((L, RWKV_W)),
        'rwkv_ln_b': normal((L, RWKV_W), 0.01),
        'router_w': normal((L, D_MODEL, N_EXPERTS), D_MODEL ** -0.5),
        'router_b': normal((L, N_EXPERTS), 0.01),
        'moe_w1': normal((L, N_EXPERTS, D_MODEL, 2 * D_FF), D_MODEL ** -0.5),
        'moe_b1': normal((L, N_EXPERTS, 2 * D_FF), 0.01),
        'moe_w2': normal((L, N_EXPERTS, D_FF, D_MODEL), D_FF ** -0.5),
        'moe_b2': normal((L, N_EXPERTS, D_MODEL), 0.01),
    }


def reference(x_prompt, x_sample, cache_gqa_k, cache_gqa_v, cache_mla_ckv, cache_mla_krope, cache_diff_k,
              cache_diff_v, state_rwkv, c, c_ctx, norm1_g, norm2_g, ada_w, ada_b, w_in, w_out, gqa_qn, gqa_kn,
              mla_qa_norm, mla_q_up, mla_kva_norm, mla_kv_up, mla_qn, mla_kn, diff_qn, diff_kn, diff_lam,
              diff_subln, rwkv_mix, rwkv_w0, rwkv_w2, rwkv_a0, rwkv_a2, rwkv_rk, rwkv_g2, rwkv_kk, rwkv_ka,
              rwkv_ln_g, rwkv_ln_b, router_w, router_b, moe_w1, moe_b1, moe_w2, moe_b2):
    params = dict(norm1_g=norm1_g, norm2_g=norm2_g, ada_w=ada_w, ada_b=ada_b, w_in=w_in, w_out=w_out,
                  gqa_qn=gqa_qn, gqa_kn=gqa_kn, mla_qa_norm=mla_qa_norm, mla_q_up=mla_q_up,
                  mla_kva_norm=mla_kva_norm, mla_kv_up=mla_kv_up, mla_qn=mla_qn, mla_kn=mla_kn,
                  diff_qn=diff_qn, diff_kn=diff_kn, diff_lam=diff_lam, diff_subln=diff_subln,
                  rwkv_mix=rwkv_mix, rwkv_w0=rwkv_w0, rwkv_w2=rwkv_w2, rwkv_a0=rwkv_a0, rwkv_a2=rwkv_a2,
                  rwkv_rk=rwkv_rk, rwkv_g2=rwkv_g2, rwkv_kk=rwkv_kk, rwkv_ka=rwkv_ka, rwkv_ln_g=rwkv_ln_g,
                  rwkv_ln_b=rwkv_ln_b, router_w=router_w, router_b=router_b, moe_w1=moe_w1, moe_b1=moe_b1,
                  moe_w2=moe_w2, moe_b2=moe_b2)
    rows = x_sample.shape[1] // GRID_W
    dt = x_sample.dtype
    ropes = (*axial_rope_tables(rows, HEAD_DIM, dt), *axial_rope_tables(rows, MLA_ROPE, dt),
             *axial_rope_tables(rows, DIFF_QK, dt))
    y_prompt, y_sample = x_prompt, x_sample
    ctx_lists = ([], [], [], [], [], [], [])
    for l in range(DEPTH):
        L = {name: w[l] for name, w in params.items()}
        lam_init = 0.8 - 0.6 * math.exp(-0.3 * l)
        y_prompt, ctx_l = context_layer(y_prompt, adaln(c_ctx[None], L['ada_w'], L['ada_b']), L, lam_init)
        for acc, tensor in zip(ctx_lists, ctx_l):
            acc.append(tensor)
        cache_l = (cache_gqa_k[:, l], cache_gqa_v[:, l], cache_mla_ckv[:, l], cache_mla_krope[:, l],
                   cache_diff_k[:, l], cache_diff_v[:, l], state_rwkv[:, l])
        y_sample = latent_layer(y_sample, adaln(c, L['ada_w'], L['ada_b']), L, lam_init, cache_l, ropes)
    new_gqa_k = jnp.stack(ctx_lists[0], axis=1)
    new_gqa_v = jnp.stack(ctx_lists[1], axis=1)
    new_mla_ckv = jnp.stack(ctx_lists[2], axis=1)
    new_mla_krope = jnp.stack(ctx_lists[3], axis=1)
    new_diff_k = jnp.stack(ctx_lists[4], axis=1)
    new_diff_v = jnp.stack(ctx_lists[5], axis=1)
    new_state_rwkv = jnp.stack(ctx_lists[6], axis=1)
    return (y_prompt, y_sample, new_gqa_k, new_gqa_v, new_mla_ckv, new_mla_krope, new_diff_k, new_diff_v, new_state_rwkv)
```

```python
import functools
import math

import jax
import jax.numpy as jnp
import numpy as np
from jax import lax
from jax.experimental import pallas as pl
from jax.experimental.pallas import tpu as pltpu

F32 = jnp.float32
BF16 = jnp.bfloat16

GRID_W = 64
ROPE_BASE = 10000.0
EPS = 1e-6
HEAD_DIM = 64
GQA_HEADS, GQA_KV_HEADS = 4, 2
MLA_HEADS, MLA_Q_LORA, MLA_KV_LORA, MLA_NOPE, MLA_ROPE, MLA_V = 4, 192, 128, 64, 32, 64
MLA_QK = MLA_NOPE + MLA_ROPE
MLA_PAD = 128
DIFF_HEADS, DIFF_QK, DIFF_V = 4, 32, 64
RWKV_HEADS, RWKV_N = 4, 64
RWKV_W = RWKV_HEADS * RWKV_N
DECAY_LORA, AAA_LORA, GATE_LORA = 32, 32, 64
RWKV_GN_EPS = 64e-5
RWKV_COLS = 3 * RWKV_W + 2 * DECAY_LORA + 2 * AAA_LORA + GATE_LORA
N_EXPERTS, TOP_K = 32, 4
SWIGLU_ALPHA, SWIGLU_LIMIT = 1.702, 7.0

LANES = 128
SUBLANES = 8
VMEM_LIMIT = 48 * 1024 * 1024
ROW_TILE = 256
Q_TILE = 256
SCAN_CHUNK = 64
SCAN_SUB = 16
SCAN_SEQS = 4
MOE_ROWS = 256
COMB_ROWS = 128
ADA_COLS = 1536


def _cparams(sem, vmem=VMEM_LIMIT):
    return pltpu.CompilerParams(dimension_semantics=sem, vmem_limit_bytes=vmem)


def _mm(a, b):
    return jnp.dot(a.astype(BF16), b.astype(BF16), preferred_element_type=F32)


def _mm_nt(a, b):
    return lax.dot_general(a.astype(BF16), b.astype(BF16), (((1,), (1,)), ((), ())),
                           preferred_element_type=F32)


def _split2(x):
    hi = x.astype(BF16)
    lo = (x - hi.astype(F32)).astype(BF16)
    return hi, lo


def _split3(x):
    hi = x.astype(BF16)
    r = x - hi.astype(F32)
    mid = r.astype(BF16)
    lo = (r - mid.astype(F32)).astype(BF16)
    return hi, mid, lo


def _mm_x2(x, ones_blk):
    hi, lo = _split2(x)
    return (jnp.dot(hi, ones_blk, preferred_element_type=F32)
            + jnp.dot(lo, ones_blk, preferred_element_type=F32))


def _mm3(a, b):
    ah, al = _split2(a)
    bh, bl = _split2(b)
    return (jnp.dot(ah, bh, preferred_element_type=F32)
            + jnp.dot(ah, bl, preferred_element_type=F32)
            + jnp.dot(al, bh, preferred_element_type=F32))


def _rope(x, cos, sin_signed, rot):
    n = x.shape[-1]
    q = rot // 4
    lane = lax.broadcasted_iota(jnp.int32, x.shape, 1)
    first = (lane & (rot // 2 - 1)) < q
    partner = jnp.where(first, pltpu.roll(x, n - q, 1), pltpu.roll(x, q, 1))
    return x * cos + partner * sin_signed


def _tile_lanes(x, reps):
    return x if reps == 1 else jnp.concatenate([x] * reps, axis=1)


def _ada_kernel(c_ref, w_ref, b_ref, o_ref):
    c = c_ref[...]
    o_ref[0] = _mm(c * jax.nn.sigmoid(c), w_ref[0]) + b_ref[0]


def _adaln(cond, ada_w, ada_b):
    depth, d, cols = ada_w.shape
    rows = cond.shape[0]
    return pl.pallas_call(
        _ada_kernel,
        out_shape=jax.ShapeDtypeStruct((depth, rows, cols), F32),
        grid=(depth, cols // ADA_COLS),
        in_specs=[
            pl.BlockSpec((rows, d), lambda l, j: (0, 0)),
            pl.BlockSpec((1, d, ADA_COLS), lambda l, j: (l, 0, j)),
            pl.BlockSpec((1, 1, ADA_COLS), lambda l, j: (l, 0, j)),
        ],
        out_specs=pl.BlockSpec((1, rows, ADA_COLS), lambda l, j: (l, 0, j)),
        compiler_params=_cparams(("arbitrary", "arbitrary")),
        name="adaln",
    )(cond, ada_w, ada_b.reshape(depth, 1, cols))


_C_AQ, _C_AK, _C_AV = 0, 256, 384
_C_CQ, _C_CKV, _C_KRP = 512, 768, 896
_C_DQ, _C_DK, _C_DV = 1408, 1664, 1920
_C_PD = 2176
_C_END = 3200


def _mla_kv(ckvn, krp, w_knope, w_v, kn_gain, blk128, cos, sin):
    kpre = _mm(ckvn, w_knope) + krp
    ss = _mm_x2(kpre * kpre, blk128) * (1.0 / MLA_QK)
    kb = kpre * lax.rsqrt(ss + EPS) * kn_gain
    kb = _rope(kb, cos, sin, MLA_ROPE)
    return kb, _mm(ckvn, w_v)


def _inproj_kernel(x_ref, sh_ref, sc_ref, g1_ref, w_ref, qup_ref, wkn_ref, wv_ref, gains_ref,
                   b64_ref, b32_ref, b128_ref, ra_ref, rm_ref, rc_ref,
                   qa_o, ka_o, va_o, qb_o, kb_o, vb_o, ckv_o, kr_o, qc_o, kc_o, vc_o, pd_o,
                   *, rows_per_group):
    i = pl.program_id(0)
    g = (i * ROW_TILE) // rows_per_group
    x = x_ref[...]
    ms = jnp.mean(x * x, axis=-1, keepdims=True)
    xn = x * lax.rsqrt(ms + EPS) * g1_ref[...]
    h = xn * (1.0 + sc_ref[pl.ds(g, 1), :]) + sh_ref[pl.ds(g, 1), :]
    p = _mm(h, w_ref[...])
    gains = gains_ref[...]
    b64, b32, b128 = b64_ref[...], b32_ref[...], b128_ref[...]

    ra = ra_ref[0]
    cos_a, sin_a = ra[:, :LANES], ra[:, LANES:]
    aq = p[:, _C_AQ:_C_AK]
    ssq = _mm_x2(aq * aq, b64) * (1.0 / HEAD_DIM)
    qa = aq * lax.rsqrt(ssq + EPS) * gains[0:1, :256]
    qa = _rope(qa, _tile_lanes(cos_a, 2), _tile_lanes(sin_a, 2), HEAD_DIM)
    qa_o[...] = qa * (HEAD_DIM ** -0.5)
    ak = p[:, _C_AK:_C_AV]
    ssk = _mm_x2(ak * ak, b64[:LANES, :LANES]) * (1.0 / HEAD_DIM)
    ka = ak * lax.rsqrt(ssk + EPS) * gains[1:2, :128]
    ka_o[...] = _rope(ka, cos_a, sin_a, HEAD_DIM)
    va_o[...] = p[:, _C_AV:_C_CQ]

    rm = rm_ref[0]
    cos_m, sin_m = _tile_lanes(rm[:, :LANES], MLA_HEADS), _tile_lanes(rm[:, LANES:], MLA_HEADS)
    cq = p[:, _C_CQ:_C_CKV]
    cqn = cq * lax.rsqrt(jnp.sum(cq * cq, axis=-1, keepdims=True) * (1.0 / MLA_Q_LORA) + EPS)
    qb = _mm(cqn * gains[2:3, :256], qup_ref[...])
    ssb = _mm_x2(qb * qb, b128) * (1.0 / MLA_QK)
    qb = qb * lax.rsqrt(ssb + EPS) * gains[4:5, :]
    qb_o[...] = _rope(qb, cos_m, sin_m, MLA_ROPE) * (MLA_QK ** -0.5)
    ckv = p[:, _C_CKV:_C_KRP]
    ckvn = ckv * lax.rsqrt(jnp.mean(ckv * ckv, axis=-1, keepdims=True) + EPS) * gains[3:4, :128]
    ckv_o[...] = ckvn
    krp = p[:, _C_KRP:_C_DQ]
    kr_o[...] = krp[:, :LANES]
    kb, vb = _mla_kv(ckvn, krp, wkn_ref[...], wv_ref[...], gains[5:6, :], b128, cos_m, sin_m)
    kb_o[...] = kb
    vb_o[...] = vb

    rc = rc_ref[0]
    cos_c, sin_c = _tile_lanes(rc[:, :LANES], 2), _tile_lanes(rc[:, LANES:], 2)
    dq = p[:, _C_DQ:_C_DK]
    ssd = _mm_x2(dq * dq, b32) * (1.0 / DIFF_QK)
    qc = dq * lax.rsqrt(ssd + EPS) * gains[6:7, :256]
    qc_o[...] = _rope(qc, cos_c, sin_c, DIFF_QK) * (DIFF_QK ** -0.5)
    dk = p[:, _C_DK:_C_DV]
    ssd = _mm_x2(dk * dk, b32) * (1.0 / DIFF_QK)
    kc = dk * lax.rsqrt(ssd + EPS) * gains[7:8, :256]
    kc_o[...] = _rope(kc, cos_c, sin_c, DIFF_QK)
    vc_o[...] = p[:, _C_DV:_C_PD]

    pd_o[...] = p[:, _C_PD:_C_END]


def _inproj(x, mods, g1, wpack, tables, *, rows_per_group, n_lat):
    n, d = x.shape
    steps = n // ROW_TILE
    lat_steps = n_lat // ROW_TILE
    pos_steps = rows_per_group // ROW_TILE

    def rope_map(i):
        is_ctx = i >= lat_steps
        return (jnp.where(is_ctx, 1, 0), jnp.where(is_ctx, 0, i % pos_steps), 0)

    full = lambda shape: pl.BlockSpec(shape, lambda i: (0,) * len(shape))
    row = lambda w: pl.BlockSpec((ROW_TILE, w), lambda i: (i, 0))
    widths = (256, 128, 128, 512, 512, 256, 128, 128, 256, 256, 256, 1024)
    return pl.pallas_call(
        functools.partial(_inproj_kernel, rows_per_group=rows_per_group),
        out_shape=tuple(jax.ShapeDtypeStruct((n, w), F32) for w in widths),
        grid=(steps,),
        in_specs=[
            row(d),
            pl.BlockSpec((SUBLANES, d), lambda i: (0, 0)),
            pl.BlockSpec((SUBLANES, d), lambda i: (0, 1)),
            full((1, d)),
            full(wpack["w_all"].shape), full(wpack["q_up"].shape), full(wpack["w_knope"].shape),
            full(wpack["w_v"].shape), full(wpack["gains"].shape),
            full(wpack["b64"].shape), full(wpack["b32"].shape), full(wpack["b128"].shape),
            pl.BlockSpec((1, ROW_TILE, 2 * LANES), rope_map),
            pl.BlockSpec((1, ROW_TILE, 2 * LANES), rope_map),
            pl.BlockSpec((1, ROW_TILE, 2 * LANES), rope_map),
        ],
        out_specs=tuple(row(w) for w in widths),
        compiler_params=_cparams(("arbitrary",)),
        name="inproj",
    )(x, mods, mods, g1, wpack["w_all"], wpack["q_up"], wpack["w_knope"], wpack["w_v"],
      wpack["gains"], wpack["b64"], wpack["b32"], wpack["b128"],
      tables["a"], tables["m"], tables["c"])


def _mla_cache_kernel(ckv_ref, krp_ref, wkn_ref, wv_ref, gains_ref, b128_ref, kb_o, vb_o):
    kn = gains_ref[...][5:6, :]
    one = jnp.ones((1, MLA_HEADS * MLA_PAD), F32)
    kb, vb = _mla_kv(ckv_ref[...], krp_ref[...], wkn_ref[...], wv_ref[...], kn, b128_ref[...],
                     one, jnp.zeros_like(one))
    kb_o[...] = kb
    vb_o[...] = vb


def _mla_cache(ckv, krp, wpack):
    rows = ckv.shape[0]
    tile = min(rows, ROW_TILE)
    full = lambda shape: pl.BlockSpec(shape, lambda i: (0,) * len(shape))
    row = lambda w: pl.BlockSpec((tile, w), lambda i: (i, 0))
    kw, vw = MLA_HEADS * MLA_PAD, MLA_HEADS * MLA_V
    return pl.pallas_call(
        _mla_cache_kernel,
        out_shape=(jax.ShapeDtypeStruct((rows, kw), F32), jax.ShapeDtypeStruct((rows, vw), F32)),
        grid=(rows // tile,),
        in_specs=[row(MLA_KV_LORA), row(kw), full(wpack["w_knope"].shape), full(wpack["w_v"].shape),
                  full(wpack["gains"].shape), full(wpack["b128"].shape)],
        out_specs=(row(kw), row(vw)),
        compiler_params=_cparams(("arbitrary",)),
        name="mla_cache",
    )(ckv, krp, wpack["w_knope"], wpack["w_v"], wpack["gains"], wpack["b128"])


def _softmax_pv(q, segs):
    scores = [_mm_nt(q, k) for k, _ in segs]
    m = scores[0].max(axis=-1, keepdims=True)
    for s in scores[1:]:
        m = jnp.maximum(m, s.max(axis=-1, keepdims=True))
    den = None
    acc = None
    for s, (_, v) in zip(scores, segs):
        e = jnp.exp(s - m)
        part = e.sum(axis=-1, keepdims=True)
        den = part if den is None else den + part
        pv = _mm(e, v)
        acc = pv if acc is None else acc + pv
    return acc / den


def _attn_kernel(*refs, has_cache, diff, lam_init):
    refs = list(refs)
    if diff:
        lam_ref, sub_ref = refs.pop(0), refs.pop(0)
    q_ref, k_ref, v_ref = refs[:3]
    o_ref = refs[-1]
    segs = [(k_ref[0, 0], v_ref[0, 0])]
    if has_cache:
        segs.append((refs[3][0, 0], refs[4][0, 0]))
    q = q_ref[0, 0]
    if not diff:
        o_ref[0, 0] = _softmax_pv(q, segs)
        return
    lane = lax.broadcasted_iota(jnp.int32, q.shape, 1)
    zero = jnp.zeros_like(q)
    o1 = _softmax_pv(jnp.where(lane < DIFF_QK, q, zero), segs)
    o2 = _softmax_pv(jnp.where(lane >= DIFF_QK, q, zero), segs)
    lv = lam_ref[...]
    lam = (jnp.exp(jnp.sum(lv[0:1] * lv[1:2], axis=-1, keepdims=True))
           - jnp.exp(jnp.sum(lv[2:3] * lv[3:4], axis=-1, keepdims=True)) + lam_init)
    o = o1 - lam * o2
    o = o * lax.rsqrt(jnp.mean(o * o, axis=-1, keepdims=True) + EPS) * sub_ref[...]
    o_ref[0, 0] = o * (1.0 - lam_init)


def _attention(q, k, v, kc, vc, *, n_req, seq, ctx_group, diff_params=None, lam_init=0.0):
    g_all, hq, s_all, dq = q.shape
    hk, dk, dv = k.shape[1], k.shape[3], v.shape[3]
    grp = hq // hk
    diff = diff_params is not None
    extra_in, extra_specs2, extra_specs3 = [], [], []
    if diff:
        extra_in = [diff_params[0], diff_params[1]]
        extra_specs3 = [pl.BlockSpec(diff_params[0].shape, lambda b, h, i: (0, 0)),
                        pl.BlockSpec(diff_params[1].shape, lambda b, h, i: (0, 0))]
        extra_specs2 = [pl.BlockSpec(diff_params[0].shape, lambda b, h: (0, 0)),
                        pl.BlockSpec(diff_params[1].shape, lambda b, h: (0, 0))]
    past = kc.shape[2]
    lat = pl.pallas_call(
        functools.partial(_attn_kernel, has_cache=True, diff=diff, lam_init=lam_init),
        out_shape=jax.ShapeDtypeStruct((n_req, hq, s_all, dv), F32),
        grid=(n_req, hq, s_all // Q_TILE),
        in_specs=extra_specs3 + [
            pl.BlockSpec((1, 1, Q_TILE, dq), lambda b, h, i: (b, h, i, 0)),
            pl.BlockSpec((1, 1, s_all, dk), lambda b, h, i: (b, h // grp, 0, 0)),
            pl.BlockSpec((1, 1, s_all, dv), lambda b, h, i: (b, h // grp, 0, 0)),
            pl.BlockSpec((1, 1, past, dk), lambda b, h, i: (b, h // grp, 0, 0)),
            pl.BlockSpec((1, 1, past, dv), lambda b, h, i: (b, h // grp, 0, 0)),
        ],
        out_specs=pl.BlockSpec((1, 1, Q_TILE, dv), lambda b, h, i: (b, h, i, 0)),
        compiler_params=_cparams(("arbitrary", "arbitrary", "arbitrary")),
        name="attn_latent",
    )(*extra_in, q, k, v, kc, vc)
    n_seq = s_all // seq
    ctx = pl.pallas_call(
        functools.partial(_attn_kernel, has_cache=False, diff=diff, lam_init=lam_init),
        out_shape=jax.ShapeDtypeStruct((1, hq, s_all, dv), F32),
        grid=(n_seq, hq),
        in_specs=extra_specs2 + [
            pl.BlockSpec((1, 1, seq, dq), lambda s, h: (ctx_group, h, s, 0)),
            pl.BlockSpec((1, 1, seq, dk), lambda s, h: (ctx_group, h // grp, s, 0)),
            pl.BlockSpec((1, 1, seq, dv), lambda s, h: (ctx_group, h // grp, s, 0)),
        ],
        out_specs=pl.BlockSpec((1, 1, seq, dv), lambda s, h: (0, h, s, 0)),
        compiler_params=_cparams(("arbitrary", "arbitrary")),
        name="attn_context",
    )(*extra_in, q, k, v)
    return jnp.concatenate([lat, ctx], axis=0)


def _rprep_kernel(p_ref, pp_ref, pn_ref, mix_ref, wl_ref, vec_ref, b64_ref,
                  r_o, v_o, kk_o, e0_o, e1_o, k0_o, k1_o, b0_o, b1_o, g_o, bon_o,
                  *, lat_steps, lat_seq_steps, ctx_seq_steps):
    i = pl.program_id(0)
    is_lat = i < lat_steps
    pos = jnp.where(is_lat, i % lat_seq_steps, (i - lat_steps) % ctx_seq_steps)
    last = jnp.where(is_lat, lat_seq_steps - 1, ctx_seq_steps - 1)
    at_start = pos == 0
    at_end = pos == last
    p = p_ref[...]
    rows = lax.broadcasted_iota(jnp.int32, p.shape, 0)
    prev_edge = jnp.where(at_start, 0.0, pp_ref[SUBLANES - 1:SUBLANES, :])
    next_edge = jnp.where(at_end, 0.0, pn_ref[0:1, :])
    prev = jnp.where(rows == 0, prev_edge, pltpu.roll(p, 1, 0))
    nxt = jnp.where(rows == ROW_TILE - 1, next_edge, pltpu.roll(p, ROW_TILE - 1, 0))
    mix = mix_ref[...]
    xs = p + (prev - p) * mix[0:1] + (nxt - p) * mix[1:2]
    r, k, v = xs[:, 0:256], xs[:, 256:512], xs[:, 512:768]
    lo = xs[:, 768:1024]
    lane = lax.broadcasted_iota(jnp.int32, lo.shape, 1)
    act = jnp.where(lane < 2 * DECAY_LORA, jnp.tanh(lo),
                    jnp.where(lane < 2 * (DECAY_LORA + AAA_LORA), lo, jax.nn.sigmoid(lo)))
    lora = _mm(act, wl_ref[...])
    vec = vec_ref[...]
    z = -(vec[0:1, :] + lora[:, 0:512])
    softplus = jnp.maximum(z, 0.0) + jnp.log1p(jnp.exp(-jnp.abs(z)))
    e = jnp.exp(-softplus - 0.5)
    a = jax.nn.sigmoid(vec[1:2, :] + lora[:, 512:1024])
    g_o[...] = lora[:, 1024:1280]
    b64 = b64_ref[...]
    kk = k * vec[2:3, 0:256]
    kk = kk * lax.rsqrt(_mm_x2(kk * kk, b64) + 1e-12)
    ka = vec[2:3, 256:512]
    r_o[...] = r
    v_o[...] = v
    kk_o[...] = kk
    bonus = None
    for d, (e_o, k_o, b_o) in enumerate(((e0_o, k0_o, b0_o), (e1_o, k1_o, b1_o))):
        a_d = a[:, 256 * d:256 * (d + 1)]
        k_d = k * (1.0 + (a_d - 1.0) * ka)
        e_o[...] = e[:, 256 * d:256 * (d + 1)]
        k_o[...] = k_d
        b_o[...] = kk * a_d
        term = _mm_x2(r * k_d * vec[3:4, 256 * d:256 * (d + 1)], b64) * v
        bonus = term if bonus is None else bonus + term
    bon_o[...] = bonus


def _rwkv_prep(pd, rpack, *, n_lat, lat_seq, ctx_seq):
    n, w = pd.shape
    steps = n // ROW_TILE
    halo_blocks = n // SUBLANES
    per = ROW_TILE // SUBLANES
    full = lambda shape: pl.BlockSpec(shape, lambda i: (0,) * len(shape))
    row = lambda width: pl.BlockSpec((ROW_TILE, width), lambda i: (i, 0))
    return pl.pallas_call(
        functools.partial(_rprep_kernel, lat_steps=n_lat // ROW_TILE,
                          lat_seq_steps=lat_seq // ROW_TILE, ctx_seq_steps=ctx_seq // ROW_TILE),
        out_shape=tuple(jax.ShapeDtypeStruct((n, RWKV_W), F32) for _ in range(11)),
        grid=(steps,),
        in_specs=[
            row(w),
            pl.BlockSpec((SUBLANES, w), lambda i: (jnp.maximum(i * per - 1, 0), 0)),
            pl.BlockSpec((SUBLANES, w), lambda i: (jnp.minimum((i + 1) * per, halo_blocks - 1), 0)),
            full(rpack["mix"].shape), full(rpack["w_lora"].shape), full(rpack["vec"].shape),
            full(rpack["b64"].shape),
        ],
        out_specs=tuple(row(RWKV_W) for _ in range(11)),
        compiler_params=_cparams(("arbitrary",)),
        name="rwkv_prep",
    )(pd, pd, pd, rpack["mix"], rpack["w_lora"], rpack["vec"], rpack["b64"])


def _scan_chunk(r, v, kk, e, k, b, h0, reverse):
    c = r.shape[0]
    ti = lax.broadcasted_iota(jnp.int32, (c, c), 0)
    si = lax.broadcasted_iota(jnp.int32, (c, c), 1)
    incl = (si >= ti) if reverse else (si <= ti)
    strict = (si > ti) if reverse else (si < ti)
    e_hi, e_mid, e_lo = _split3(e)
    incl_b = incl.astype(BF16)
    cs = (jnp.dot(incl_b, e_hi, preferred_element_type=F32)
          + jnp.dot(incl_b, e_mid, preferred_element_type=F32)
          + jnp.dot(incl_b, e_lo, preferred_element_type=F32))
    tot = cs[0:1, :] if reverse else cs[c - 1:c, :]
    g_prev = jnp.exp(e - cs)
    g_incl = jnp.exp(-cs)
    g_inv = jnp.exp(cs)
    kkg, rg, bq, kq = kk * g_prev, r * g_incl, b * g_inv, k * g_inv
    left = jnp.concatenate([kkg, rg], axis=0)
    sb = _mm_nt(left, bq)
    sk = _mm_nt(left, kq)
    zero = jnp.zeros((c, c), F32)
    a_m = jnp.where(strict, sb[:c], zero)
    b_m = jnp.where(strict, sk[:c], zero)
    a_p = jnp.where(incl, sb[c:], zero)
    b_p = jnp.where(incl, sk[c:], zero)

    eye = (ti == si).astype(F32)
    same = (ti // SCAN_SUB) == (si // SCAN_SUB)
    a_d = jnp.where(same, a_m, zero)
    a_off = a_m - a_d
    t_d = eye - a_d
    pw = _mm3(a_d, a_d)
    steps = int(math.log2(SCAN_SUB))
    for j in range(1, steps):
        t_d = t_d + _mm3(t_d, pw)
        if j + 1 < steps:
            pw = _mm3(pw, pw)
    nil = _mm3(t_d, a_off)
    inm = eye - nil
    nblk = c // SCAN_SUB
    if nblk > 2:
        nil2 = _mm3(nil, nil)
        inm = inm + _mm3(inm, nil2)
    t_full = _mm3(inm, t_d) if nblk > 1 else t_d

    bmv = _mm(b_m, v)
    w1 = _mm(t_full, kkg)
    z = _mm(t_full, bmv)
    cs_t = cs.T
    tot_col = jnp.sum(e.T, axis=1, keepdims=True)
    g_end_t = jnp.exp(cs_t - tot_col)
    kt_t = k.T * g_end_t
    bt_t = b.T * g_end_t
    u = _mm(w1, h0) + z
    h_new = jnp.exp(-tot_col) * h0 + _mm(kt_t, v) - _mm(bt_t, u)
    y = _mm(rg, h0) - _mm(a_p, u) + _mm(b_p, v)
    return y, h_new


def _scan_kernel(rf, vf, kkf, ef, kf, bf, rb, vb, kkb, eb, kb, bb, h0_ref,
                 y0_ref, y1_ref, hout_ref, h_scr, *, n_seq, n_chunks):
    c = pl.program_id(1)

    @pl.when(c == 0)
    def _():
        h_scr[...] = h0_ref[...]

    def seq_body(s, carry):
        for d, (r_, v_, kk_, e_, k_, b_, y_) in enumerate(
                ((rf, vf, kkf, ef, kf, bf, y0_ref), (rb, vb, kkb, eb, kb, bb, y1_ref))):
            rr, vv, kkk, ee, kkd, bbd = r_[s], v_[s], kk_[s], e_[s], k_[s], b_[s]
            ys = []
            for h in range(RWKV_HEADS):
                sl = slice(RWKV_N * h, RWKV_N * (h + 1))
                y, h_new = _scan_chunk(rr[:, sl], vv[:, sl], kkk[:, sl], ee[:, sl], kkd[:, sl],
                                       bbd[:, sl], h_scr[s, d, h], reverse=(d == 1))
                h_scr[s, d, h] = h_new
                ys.append(y)
            y_[s] = jnp.concatenate(ys, axis=1)
        return carry

    lax.fori_loop(0, n_seq, seq_body, 0)

    @pl.when(c == n_chunks - 1)
    def _():
        hout_ref[...] = h_scr[...]


def _rwkv_scan(streams, h0, *, first_seq, seq_len):
    n_chunks = seq_len // SCAN_CHUNK
    total = h0.shape[0]
    sb = min(SCAN_SEQS, total)
    assert total % sb == 0 and first_seq % sb == 0
    arrs = {name: a.reshape(-1, seq_len, RWKV_W) for name, a in streams.items()}
    blk0 = first_seq // sb
    blk = (sb, SCAN_CHUNK, RWKV_W)
    fwd = pl.BlockSpec(blk, lambda s, c: (blk0 + s, c, 0))
    bwd = pl.BlockSpec(blk, lambda s, c: (blk0 + s, n_chunks - 1 - c, 0))
    st_shape = (sb,) + h0.shape[1:]
    st = pl.BlockSpec(st_shape, lambda s, c: (s, 0, 0, 0, 0))
    y_shape = jax.ShapeDtypeStruct((total, seq_len, RWKV_W), F32)
    y0, y1, h_fin = pl.pallas_call(
        functools.partial(_scan_kernel, n_seq=sb, n_chunks=n_chunks),
        out_shape=(y_shape, y_shape, jax.ShapeDtypeStruct(h0.shape, F32)),
        grid=(total // sb, n_chunks),
        in_specs=[fwd] * 6 + [bwd] * 6 + [st],
        out_specs=(pl.BlockSpec(blk, lambda s, c: (s, c, 0)),
                   pl.BlockSpec(blk, lambda s, c: (s, n_chunks - 1 - c, 0)),
                   st),
        scratch_shapes=[pltpu.VMEM(st_shape, F32)],
        compiler_params=_cparams(("arbitrary", "arbitrary")),
        name="rwkv_scan",
    )(arrs["r"], arrs["v"], arrs["kk"], arrs["e0"], arrs["k0"], arrs["b0"],
      arrs["r"], arrs["v"], arrs["kk"], arrs["e1"], arrs["k1"], arrs["b1"], h0)
    return y0, y1, h_fin


def _outproj_kernel(x_ref, oa_ref, ob_ref, oc_ref, y0_ref, y1_ref, bon_ref, g_ref, ln_ref, b64_ref,
                    wo_ref, gate_ref, sh_ref, sc_ref, g2_ref, rwh_ref, rwl_ref, rb_ref,
                    x1_o, h2_o, lg_o, *, rows_per_group):
    i = pl.program_id(0)
    g = (i * ROW_TILE) // rows_per_group
    b64 = b64_ref[...]
    y = y0_ref[...] + y1_ref[...]
    mu = _mm_x2(y, b64) * (1.0 / RWKV_N)
    dy = y - mu
    var = _mm_x2(dy * dy, b64) * (1.0 / RWKV_N)
    ln = ln_ref[...]
    yn = dy * lax.rsqrt(var + RWKV_GN_EPS) * ln[0:1] + ln[1:2]
    od = (yn + bon_ref[...]) * g_ref[...]
    wo = wo_ref[...]
    mixed = (_mm(oa_ref[...], wo[0:256]) + _mm(ob_ref[...], wo[256:512])
             + _mm(oc_ref[...], wo[512:768]) + _mm(od, wo[768:1024]))
    x1 = x_ref[...] + gate_ref[pl.ds(g, 1), :] * mixed
    x1_o[...] = x1
    ms = jnp.mean(x1 * x1, axis=-1, keepdims=True)
    h2 = (x1 * lax.rsqrt(ms + EPS) * g2_ref[...]) * (1.0 + sc_ref[pl.ds(g, 1), :]) + sh_ref[pl.ds(g, 1), :]
    h2_o[...] = h2
    hh, hl = _split2(h2)
    rwh, rwl = rwh_ref[...], rwl_ref[...]
    lg_o[...] = (jnp.dot(hh, rwh, preferred_element_type=F32) + jnp.dot(hh, rwl, preferred_element_type=F32)
                 + jnp.dot(hl, rwh, preferred_element_type=F32)) + rb_ref[...]


def _outproj(x, oa, ob, oc, y0, y1, bonus, gate, ln, b64, wo, mods, g2, rw_hi, rw_lo, rb,
             *, rows_per_group):
    n, d = x.shape
    full = lambda shape: pl.BlockSpec(shape, lambda i: (0,) * len(shape))
    row = lambda w: pl.BlockSpec((ROW_TILE, w), lambda i: (i, 0))
    mod = lambda j: pl.BlockSpec((SUBLANES, d), lambda i: (0, j))
    return pl.pallas_call(
        functools.partial(_outproj_kernel, rows_per_group=rows_per_group),
        out_shape=(jax.ShapeDtypeStruct((n, d), F32), jax.ShapeDtypeStruct((n, d), F32),
                   jax.ShapeDtypeStruct((n, LANES), F32)),
        grid=(n // ROW_TILE,),
        in_specs=[row(d), row(256), row(256), row(256), row(256), row(256), row(256), row(256),
                  full(ln.shape), full(b64.shape), full(wo.shape), mod(2), mod(3), mod(4),
                  full(g2.shape), full(rw_hi.shape), full(rw_lo.shape), full(rb.shape)],
        out_specs=(row(d), row(d), row(LANES)),
        compiler_params=_cparams(("arbitrary",)),
        name="outproj",
    )(x, oa, ob, oc, y0, y1, bonus, gate, ln, b64, wo, mods, mods, mods, g2, rw_hi, rw_lo, rb)


def _row_copy(src_hbm, row, dst, sem):
    return pltpu.make_async_copy(src_hbm.at[pl.ds(row, 1), :], dst, sem)


def _moe_kernel(be_ref, nused_ref, cur_ref, nxt_ref, h_hbm, w1g_ref, w1l_ref, w2_ref,
                b1g_ref, b1l_ref, b2_ref, out_ref, xbuf, sem):
    del be_ref
    i = pl.program_id(0)
    nused = nused_ref[0]
    slot = i % 2

    def issue(idx_ref, sl):
        def body(r, carry):
            _row_copy(h_hbm, idx_ref[0, 0, r], xbuf.at[sl, pl.ds(r, 1), :], sem.at[sl]).start()
            return carry
        lax.fori_loop(0, MOE_ROWS, body, 0)

    @pl.when(i == 0)
    def _():
        issue(cur_ref, 0)

    @pl.when(i + 1 < nused)
    def _():
        issue(nxt_ref, 1 - slot)

    @pl.when(i < nused)
    def _():
        def wait_body(r, carry):
            _row_copy(h_hbm, 0, xbuf.at[slot, pl.ds(r, 1), :], sem.at[slot]).wait()
            return carry
        lax.fori_loop(0, MOE_ROWS, wait_body, 0)
        x = xbuf[slot]
        glu = jnp.minimum(_mm(x, w1g_ref[0]) + b1g_ref[0], SWIGLU_LIMIT)
        lin = jnp.clip(_mm(x, w1l_ref[0]) + b1l_ref[0], -SWIGLU_LIMIT, SWIGLU_LIMIT)
        act = glu * jax.nn.sigmoid(SWIGLU_ALPHA * glu) * (lin + 1.0)
        out_ref[...] = _mm(act, w2_ref[0]) + b2_ref[0]

    @pl.when(i >= nused)
    def _():
        out_ref[...] = jnp.zeros_like(out_ref)


def _moe_experts(h2, block_e, nused, buf_tok, w1g, w1l, w2, b1g, b1l, b2):
    n, d = h2.shape
    n_blocks = block_e.shape[0]
    dff = w1g.shape[2]
    idx = buf_tok.reshape(n_blocks, 1, MOE_ROWS)
    wspec = lambda s1, s2: pl.BlockSpec((1, s1, s2), lambda i, be, nu: (be[i], 0, 0))
    grid_spec = pltpu.PrefetchScalarGridSpec(
        num_scalar_prefetch=2,
        grid=(n_blocks,),
        in_specs=[
            pl.BlockSpec((1, 1, MOE_ROWS), lambda i, be, nu: (i, 0, 0), memory_space=pltpu.SMEM),
            pl.BlockSpec((1, 1, MOE_ROWS), lambda i, be, nu: (jnp.minimum(i + 1, n_blocks - 1), 0, 0),
                         memory_space=pltpu.SMEM),
            pl.BlockSpec(memory_space=pl.ANY),
            wspec(d, dff), wspec(d, dff), wspec(dff, d), wspec(1, dff), wspec(1, dff), wspec(1, d),
        ],
        out_specs=pl.BlockSpec((MOE_ROWS, d), lambda i, be, nu: (i, 0)),
        scratch_shapes=[pltpu.VMEM((2, MOE_ROWS, d), F32), pltpu.SemaphoreType.DMA((2,))],
    )
    return pl.pallas_call(
        _moe_kernel,
        out_shape=jax.ShapeDtypeStruct((n_blocks * MOE_ROWS, d), F32),
        grid_spec=grid_spec,
        compiler_params=_cparams(("arbitrary",)),
        name="moe_experts",
    )(block_e, nused, idx, idx, h2, w1g, w1l, w2, b1g, b1l, b2)


def _comb_kernel(cur_ref, nxt_ref, yb_hbm, x_ref, gates_ref, gate_ref, out_ref, buf, sem,
                 *, rows_per_group, n_steps):
    i = pl.program_id(0)
    slot = i % 2
    g = (i * COMB_ROWS) // rows_per_group

    def issue(idx_ref, sl):
        def body(t, carry):
            for kx in range(TOP_K):
                _row_copy(yb_hbm, idx_ref[0, 0, t * TOP_K + kx], buf.at[sl, kx, pl.ds(t, 1), :],
                          sem.at[sl]).start()
            return carry
        lax.fori_loop(0, COMB_ROWS, body, 0)

    @pl.when(i == 0)
    def _():
        issue(cur_ref, 0)

    @pl.when(i + 1 < n_steps)
    def _():
        issue(nxt_ref, 1 - slot)

    def wait_body(t, carry):
        for kx in range(TOP_K):
            _row_copy(yb_hbm, 0, buf.at[slot, kx, pl.ds(t, 1), :], sem.at[slot]).wait()
        return carry
    lax.fori_loop(0, COMB_ROWS, wait_body, 0)
    gates = gates_ref[...]
    acc = gates[:, 0:1] * buf[slot, 0]
    for kx in range(1, TOP_K):
        acc = acc + gates[:, kx:kx + 1] * buf[slot, kx]
    out_ref[...] = x_ref[...] + gate_ref[pl.ds(g, 1), :] * acc


def _moe_combine(yb, x1, gates, slot_dest, mods, *, rows_per_group):
    n, d = x1.shape
    steps = n // COMB_ROWS
    idx = slot_dest.reshape(steps, 1, COMB_ROWS * TOP_K)
    return pl.pallas_call(
        functools.partial(_comb_kernel, rows_per_group=rows_per_group, n_steps=steps),
        out_shape=jax.ShapeDtypeStruct((n, d), F32),
        grid=(steps,),
        in_specs=[
            pl.BlockSpec((1, 1, COMB_ROWS * TOP_K), lambda i: (i, 0, 0), memory_space=pltpu.SMEM),
            pl.BlockSpec((1, 1, COMB_ROWS * TOP_K), lambda i: (jnp.minimum(i + 1, steps - 1), 0, 0),
                         memory_space=pltpu.SMEM),
            pl.BlockSpec(memory_space=pl.ANY),
            pl.BlockSpec((COMB_ROWS, d), lambda i: (i, 0)),
            pl.BlockSpec((COMB_ROWS, TOP_K), lambda i: (i, 0)),
            pl.BlockSpec((SUBLANES, d), lambda i: (0, 5)),
        ],
        out_specs=pl.BlockSpec((COMB_ROWS, d), lambda i: (i, 0)),
        scratch_shapes=[pltpu.VMEM((2, TOP_K, COMB_ROWS, d), F32), pltpu.SemaphoreType.DMA((2,))],
        compiler_params=_cparams(("arbitrary",)),
        name="moe_combine",
    )(idx, idx, yb, x1, gates, mods)


def _route(logits):
    n = logits.shape[0]
    top_val, top_idx = lax.top_k(logits, TOP_K)
    gates = jax.nn.softmax(top_val, axis=-1)
    flat_e = top_idx.reshape(-1).astype(jnp.int32)
    nk = n * TOP_K
    order = jnp.argsort(flat_e)
    sorted_e = flat_e[order]
    counts = jnp.bincount(flat_e, length=N_EXPERTS).astype(jnp.int32)
    padded = (counts + MOE_ROWS - 1) // MOE_ROWS * MOE_ROWS
    start = jnp.cumsum(counts) - counts
    pad_end = jnp.cumsum(padded)
    pad_start = pad_end - padded
    dest = (pad_start[sorted_e] + jnp.arange(nk, dtype=jnp.int32) - start[sorted_e]).astype(jnp.int32)
    n_blocks = (nk + N_EXPERTS * (MOE_ROWS - 1) + MOE_ROWS - 1) // MOE_ROWS
    cap = n_blocks * MOE_ROWS
    buf_tok = jnp.zeros((cap,), jnp.int32).at[dest].set((order // TOP_K).astype(jnp.int32))
    block_e = jnp.minimum(
        jnp.searchsorted(pad_end, jnp.arange(n_blocks, dtype=jnp.int32) * MOE_ROWS, side="right"),
        N_EXPERTS - 1).astype(jnp.int32)
    nused = (pad_end[-1] // MOE_ROWS).astype(jnp.int32).reshape(1)
    slot_dest = jnp.zeros((nk,), jnp.int32).at[order].set(dest)
    return gates, buf_tok, block_e, nused, slot_dest


def _block_ones(n, blk):
    idx = np.arange(n) // blk
    return jnp.asarray(idx[:, None] == idx[None, :], dtype=BF16)


def _pad_cols(w, width):
    return jnp.pad(w, ((0, 0), (0, width - w.shape[1])))


def _place_heads(w, per_head, offset, heads=MLA_HEADS, slot=MLA_PAD):
    rows = w.shape[0]
    w3 = w.reshape(rows, heads, per_head)
    out = jnp.zeros((rows, heads, slot), w.dtype).at[:, :, offset:offset + per_head].set(w3)
    return out.reshape(rows, heads * slot)


def _pack_in_weights(P, l):
    w_in = P["w_in"][l]
    sizes = (256, 128, 128, MLA_Q_LORA, MLA_KV_LORA, MLA_ROPE, 256, 256, 256, RWKV_COLS)
    offs = np.concatenate([[0], np.cumsum(sizes)])
    seg = [w_in[:, offs[j]:offs[j + 1]] for j in range(len(sizes))]
    krp = _place_heads(jnp.tile(seg[5], (1, MLA_HEADS)), MLA_ROPE, MLA_NOPE)
    w_all = jnp.concatenate([seg[0], seg[1], seg[2], _pad_cols(seg[3], 256), seg[4], krp,
                             seg[6], seg[7], seg[8], _pad_cols(seg[9], 1024)], axis=1).astype(BF16)
    q_up = _place_heads(P["mla_q_up"][l], MLA_QK, 0)
    q_up = jnp.pad(q_up, ((0, 256 - MLA_Q_LORA), (0, 0))).astype(BF16)
    kv_up = P["mla_kv_up"][l].reshape(MLA_KV_LORA, MLA_HEADS, MLA_NOPE + MLA_V)
    w_knope = _place_heads(kv_up[:, :, :MLA_NOPE].reshape(MLA_KV_LORA, -1), MLA_NOPE, 0).astype(BF16)
    w_v = kv_up[:, :, MLA_NOPE:].reshape(MLA_KV_LORA, MLA_HEADS * MLA_V).astype(BF16)
    width = MLA_HEADS * MLA_PAD

    def rowpad(v):
        return jnp.pad(v, (0, width - v.shape[0]))

    gains = jnp.stack([
        rowpad(jnp.tile(P["gqa_qn"][l], GQA_HEADS)),
        rowpad(jnp.tile(P["gqa_kn"][l], GQA_KV_HEADS)),
        rowpad(P["mla_qa_norm"][l]),
        rowpad(P["mla_kva_norm"][l]),
        _place_heads(jnp.tile(P["mla_qn"][l], MLA_HEADS)[None], MLA_QK, 0)[0],
        _place_heads(jnp.tile(P["mla_kn"][l], MLA_HEADS)[None], MLA_QK, 0)[0],
        rowpad(jnp.tile(P["diff_qn"][l], 2 * DIFF_HEADS)),
        rowpad(jnp.tile(P["diff_kn"][l], 2 * DIFF_HEADS)),
    ]).astype(F32)
    return dict(w_all=w_all, q_up=q_up, w_knope=w_knope, w_v=w_v, gains=gains,
                b64=_block_ones(256, 64), b32=_block_ones(256, 32), b128=_block_ones(width, MLA_PAD))


def _pack_rwkv(P, l):
    mix = _pad_cols(P["rwkv_mix"][l], 1024)
    w_lora = jnp.zeros((256, 1280), F32)
    w_lora = w_lora.at[0:32, 0:256].set(P["rwkv_w2"][l, 0]).at[32:64, 256:512].set(P["rwkv_w2"][l, 1])
    w_lora = w_lora.at[64:96, 512:768].set(P["rwkv_a2"][l, 0]).at[96:128, 768:1024].set(P["rwkv_a2"][l, 1])
    w_lora = w_lora.at[128:192, 1024:1280].set(P["rwkv_g2"][l]).astype(BF16)
    vec = jnp.zeros((SUBLANES, 512), F32)
    vec = vec.at[0].set(P["rwkv_w0"][l].reshape(-1)).at[1].set(P["rwkv_a0"][l].reshape(-1))
    vec = vec.at[2].set(jnp.concatenate([P["rwkv_kk"][l], P["rwkv_ka"][l]]))
    vec = vec.at[3].set(P["rwkv_rk"][l].reshape(-1))
    return dict(mix=mix, w_lora=w_lora, vec=vec, b64=_block_ones(256, 64))


def _rope_table(seq, rot, lanes_per_tile_group, lane_offset=0):
    rows = seq // GRID_W
    row = jnp.repeat(jnp.arange(rows, dtype=F32), GRID_W)
    col = (jnp.arange(rows * GRID_W) % GRID_W).astype(F32)
    n_freq = rot // 4
    inv_freq = ROPE_BASE ** (-jnp.arange(n_freq, dtype=F32) / n_freq)
    ang = jnp.stack([row[:, None] * inv_freq, col[:, None] * inv_freq], axis=1)
    cos = jnp.cos(ang)[:, :, None, :]
    sin = jnp.sin(ang)[:, :, None, :]
    cos_g = jnp.broadcast_to(cos, (seq, 2, 2, n_freq)).reshape(seq, rot)
    sin_g = (jnp.broadcast_to(sin, (seq, 2, 2, n_freq))
             * jnp.asarray([-1.0, 1.0], F32)[None, None, :, None]).reshape(seq, rot)
    cos_t = jnp.ones((seq, LANES), F32)
    sin_t = jnp.zeros((seq, LANES), F32)
    for start in range(lane_offset, LANES, lanes_per_tile_group):
        cos_t = cos_t.at[:, start:start + rot].set(cos_g)
        sin_t = sin_t.at[:, start:start + rot].set(sin_g)
    real = jnp.concatenate([cos_t, sin_t], axis=1)
    ident = jnp.concatenate([jnp.ones((seq, LANES), F32), jnp.zeros((seq, LANES), F32)], axis=1)
    return jnp.stack([real, ident])


def _to_heads(x, groups, seq, heads):
    d = x.shape[1] // heads
    return x.reshape(groups, seq, heads, d).transpose(0, 2, 1, 3).astype(BF16)


def _from_heads(o):
    g, h, s, d = o.shape
    return o.transpose(0, 2, 1, 3).reshape(g * s, h * d).astype(BF16)


def kernel(x_prompt, x_sample, cache_gqa_k, cache_gqa_v, cache_mla_ckv, cache_mla_krope, cache_diff_k, cache_diff_v, state_rwkv, c, c_ctx, norm1_g, norm2_g, ada_w, ada_b, w_in, w_out, gqa_qn, gqa_kn, mla_qa_norm, mla_q_up, mla_kva_norm, mla_kv_up, mla_qn, mla_kn, diff_qn, diff_kn, diff_lam, diff_subln, rwkv_mix, rwkv_w0, rwkv_w2, rwkv_a0, rwkv_a2, rwkv_rk, rwkv_g2, rwkv_kk, rwkv_ka, rwkv_ln_g, rwkv_ln_b, router_w, router_b, moe_w1, moe_b1, moe_w2, moe_b2):
    P = dict(w_in=w_in, gqa_qn=gqa_qn, gqa_kn=gqa_kn, mla_qa_norm=mla_qa_norm, mla_q_up=mla_q_up,
             mla_kva_norm=mla_kva_norm, mla_kv_up=mla_kv_up, mla_qn=mla_qn, mla_kn=mla_kn,
             diff_qn=diff_qn, diff_kn=diff_kn, rwkv_mix=rwkv_mix, rwkv_w0=rwkv_w0, rwkv_w2=rwkv_w2,
             rwkv_a0=rwkv_a0, rwkv_a2=rwkv_a2, rwkv_rk=rwkv_rk, rwkv_g2=rwkv_g2, rwkv_kk=rwkv_kk,
             rwkv_ka=rwkv_ka)
    depth = norm1_g.shape[0]
    bc, tc, d = x_prompt.shape
    bl, tl, _ = x_sample.shape
    assert bc * tc == tl, "context tokens must fill exactly one latent-sequence group"
    assert tl % ROW_TILE == 0 and tc % ROW_TILE == 0 and tc % SCAN_CHUNK == 0
    n_lat = bl * tl
    n = n_lat + tl
    groups = bl + 1
    past = cache_gqa_k.shape[3]

    cond = jnp.concatenate([c, c_ctx[None], jnp.zeros((SUBLANES - groups, d), F32)], axis=0)
    mods_all = _adaln(cond, ada_w, ada_b)
    tables = dict(a=_rope_table(tl, HEAD_DIM, HEAD_DIM),
                  m=_rope_table(tl, MLA_ROPE, LANES, lane_offset=MLA_NOPE),
                  c=_rope_table(tl, DIFF_QK, DIFF_QK))
    x = jnp.concatenate([x_sample.reshape(n_lat, d), x_prompt.reshape(tl, d)], axis=0)
    ctx_rows = slice(n_lat, n)
    new = [[] for _ in range(7)]

    for l in range(depth):
        lam_init = 0.8 - 0.6 * math.exp(-0.3 * l)
        mods = mods_all[l]
        wpack = _pack_in_weights(P, l)
        rpack = _pack_rwkv(P, l)
        (qa, ka, va, qb, kb, vb, ckvn, kr, qc, kc, vc, pd) = _inproj(
            x, mods, norm1_g[l][None], wpack, tables, rows_per_group=tl, n_lat=n_lat)

        new[0].append(ka[ctx_rows].reshape(bc, tc, GQA_KV_HEADS, HEAD_DIM).transpose(0, 2, 1, 3))
        new[1].append(va[ctx_rows].reshape(bc, tc, GQA_KV_HEADS, HEAD_DIM).transpose(0, 2, 1, 3))
        new[2].append(ckvn[ctx_rows].reshape(bc, tc, MLA_KV_LORA))
        new[3].append(kr[ctx_rows, MLA_NOPE:MLA_NOPE + MLA_ROPE].reshape(bc, tc, MLA_ROPE))
        new[4].append(kc[ctx_rows].reshape(bc, tc, DIFF_HEADS, 2, DIFF_QK).transpose(0, 2, 3, 1, 4))
        new[5].append(vc[ctx_rows].reshape(bc, tc, DIFF_HEADS, DIFF_V).transpose(0, 2, 1, 3))

        o_a = _attention(_to_heads(qa, groups, tl, GQA_HEADS), _to_heads(ka, groups, tl, GQA_KV_HEADS),
                         _to_heads(va, groups, tl, GQA_KV_HEADS),
                         cache_gqa_k[:, l].astype(BF16), cache_gqa_v[:, l].astype(BF16),
                         n_req=bl, seq=tc, ctx_group=bl)
        krp_c = _place_heads(jnp.tile(cache_mla_krope[:, l].reshape(bl * past, MLA_ROPE), (1, MLA_HEADS)),
                             MLA_ROPE, MLA_NOPE)
        kb_c, vb_c = _mla_cache(cache_mla_ckv[:, l].reshape(bl * past, MLA_KV_LORA), krp_c, wpack)
        o_b = _attention(_to_heads(qb, groups, tl, MLA_HEADS), _to_heads(kb, groups, tl, MLA_HEADS),
                         _to_heads(vb, groups, tl, MLA_HEADS),
                         _to_heads(kb_c, bl, past, MLA_HEADS), _to_heads(vb_c, bl, past, MLA_HEADS),
                         n_req=bl, seq=tc, ctx_group=bl)
        kc_c = cache_diff_k[:, l].transpose(0, 1, 3, 2, 4).reshape(bl, DIFF_HEADS, past, 2 * DIFF_QK)
        o_c = _attention(_to_heads(qc, groups, tl, DIFF_HEADS), _to_heads(kc, groups, tl, DIFF_HEADS),
                         _to_heads(vc, groups, tl, DIFF_HEADS),
                         kc_c.astype(BF16), cache_diff_v[:, l].astype(BF16),
                         n_req=bl, seq=tc, ctx_group=bl,
                         diff_params=(diff_lam[l], diff_subln[l][None]), lam_init=lam_init)

        names = ("r", "v", "kk", "e0", "e1", "k0", "k1", "b0", "b1", "g", "bonus")
        rw = dict(zip(names, _rwkv_prep(pd, rpack, n_lat=n_lat, lat_seq=tl, ctx_seq=tc)))
        streams = {k_: rw[k_] for k_ in names[:9]}
        h0_lat = jnp.swapaxes(state_rwkv[:, l], -1, -2)
        y0l, y1l, _ = _rwkv_scan(streams, h0_lat, first_seq=0, seq_len=tl)
        h0_ctx = jnp.zeros((bc, 2, RWKV_HEADS, RWKV_N, RWKV_N), F32)
        y0c, y1c, h_ctx = _rwkv_scan(streams, h0_ctx, first_seq=n_lat // tc, seq_len=tc)
        new[6].append(jnp.swapaxes(h_ctx, -1, -2))
        y0 = jnp.concatenate([y0l.reshape(n_lat, RWKV_W), y0c.reshape(tl, RWKV_W)], axis=0)
        y1 = jnp.concatenate([y1l.reshape(n_lat, RWKV_W), y1c.reshape(tl, RWKV_W)], axis=0)

        ln = jnp.stack([rwkv_ln_g[l], rwkv_ln_b[l]])
        rw_f = _pad_cols(router_w[l], LANES)
        rw_hi = rw_f.astype(BF16)
        rw_lo = (rw_f - rw_hi.astype(F32)).astype(BF16)
        rb = _pad_cols(router_b[l][None], LANES)
        x1, h2, logits = _outproj(x, _from_heads(o_a), _from_heads(o_b), _from_heads(o_c), y0, y1,
                                  rw["bonus"], rw["g"], ln, rpack["b64"], w_out[l].astype(BF16), mods,
                                  norm2_g[l][None], rw_hi, rw_lo, rb, rows_per_group=tl)

        gates, buf_tok, block_e, nused, slot_dest = _route(logits[:, :N_EXPERTS])
        w1 = moe_w1[l]
        yb = _moe_experts(h2, block_e, nused, buf_tok,
                          w1[:, :, 0::2].astype(BF16), w1[:, :, 1::2].astype(BF16), moe_w2[l].astype(BF16),
                          moe_b1[l][:, None, 0::2], moe_b1[l][:, None, 1::2], moe_b2[l][:, None, :])
        x = _moe_combine(yb, x1, gates, slot_dest, mods, rows_per_group=tl)

    y_sample = x[:n_lat].reshape(bl, tl, d)
    y_prompt = x[n_lat:].reshape(bc, tc, d)
    return (y_prompt, y_sample) + tuple(jnp.stack(t, axis=1) for t in new)
```

```python
import functools
import math

import jax
import jax.numpy as jnp
import numpy as np
from jax import lax
from jax.experimental import pallas as pl
from jax.experimental.pallas import tpu as pltpu

F32 = jnp.float32
BF16 = jnp.bfloat16

GRID_W = 64
ROPE_BASE = 10000.0
EPS = 1e-6
HEAD_DIM = 64
GQA_HEADS, GQA_KV_HEADS = 4, 2
MLA_HEADS, MLA_Q_LORA, MLA_KV_LORA, MLA_NOPE, MLA_ROPE, MLA_V = 4, 192, 128, 64, 32, 64
MLA_QK = MLA_NOPE + MLA_ROPE
MLA_PAD = 128
DIFF_HEADS, DIFF_QK, DIFF_V = 4, 32, 64
RWKV_HEADS, RWKV_N = 4, 64
RWKV_W = RWKV_HEADS * RWKV_N
DECAY_LORA, AAA_LORA, GATE_LORA = 32, 32, 64
RWKV_GN_EPS = 64e-5
RWKV_COLS = 3 * RWKV_W + 2 * DECAY_LORA + 2 * AAA_LORA + GATE_LORA
N_EXPERTS, TOP_K = 32, 4
SWIGLU_ALPHA, SWIGLU_LIMIT = 1.702, 7.0

LANES = 128
SUBLANES = 8
VMEM_LIMIT = 48 * 1024 * 1024
ROW_TILE = 256
Q_TILE = 256
KV_CHUNK = 512
SCAN_CHUNK = 64
SCAN_SUB = 16
SCAN_SEQS = 4
MOE_ROWS = 256
COMB_ROWS = 128
ADA_COLS = 1536
DMA_UNROLL = 8


def _cparams(sem, vmem=VMEM_LIMIT):
    return pltpu.CompilerParams(dimension_semantics=sem, vmem_limit_bytes=vmem)


def _mm(a, b):
    return jnp.dot(a.astype(BF16), b.astype(BF16), preferred_element_type=F32)


def _mm_nt(a, b):
    return lax.dot_general(a.astype(BF16), b.astype(BF16), (((1,), (1,)), ((), ())),
                           preferred_element_type=F32)


def _split2(x):
    hi = x.astype(BF16)
    lo = (x - hi.astype(F32)).astype(BF16)
    return hi, lo


def _split3(x):
    hi = x.astype(BF16)
    r = x - hi.astype(F32)
    mid = r.astype(BF16)
    lo = (r - mid.astype(F32)).astype(BF16)
    return hi, mid, lo


def _mm_x2(x, ones_blk):
    hi, lo = _split2(x)
    return (jnp.dot(hi, ones_blk, preferred_element_type=F32)
            + jnp.dot(lo, ones_blk, preferred_element_type=F32))


def _mm3(a, b):
    ah, al = _split2(a)
    bh, bl = _split2(b)
    return (jnp.dot(ah, bh, preferred_element_type=F32)
            + jnp.dot(ah, bl, preferred_element_type=F32)
            + jnp.dot(al, bh, preferred_element_type=F32))


def _rope(x, cos, sin_signed, rot):
    n = x.shape[-1]
    q = rot // 4
    lane = lax.broadcasted_iota(jnp.int32, x.shape, 1)
    first = (lane & (rot // 2 - 1)) < q
    partner = jnp.where(first, pltpu.roll(x, n - q, 1), pltpu.roll(x, q, 1))
    return x * cos + partner * sin_signed


def _tile_lanes(x, reps):
    return x if reps == 1 else jnp.concatenate([x] * reps, axis=1)


def _ada_kernel(c_ref, w_ref, b_ref, o_ref):
    c = c_ref[...]
    o_ref[0] = _mm(c * jax.nn.sigmoid(c), w_ref[0]) + b_ref[0]


def _adaln(cond, ada_w, ada_b):
    depth, d, cols = ada_w.shape
    rows = cond.shape[0]
    return pl.pallas_call(
        _ada_kernel,
        out_shape=jax.ShapeDtypeStruct((depth, rows, cols), F32),
        grid=(depth, cols // ADA_COLS),
        in_specs=[
            pl.BlockSpec((rows, d), lambda l, j: (0, 0)),
            pl.BlockSpec((1, d, ADA_COLS), lambda l, j: (l, 0, j)),
            pl.BlockSpec((1, 1, ADA_COLS), lambda l, j: (l, 0, j)),
        ],
        out_specs=pl.BlockSpec((1, rows, ADA_COLS), lambda l, j: (l, 0, j)),
        compiler_params=_cparams(("arbitrary", "arbitrary")),
        name="adaln",
    )(cond, ada_w, ada_b.reshape(depth, 1, cols))


_C_AQ, _C_AK, _C_AV = 0, 256, 384
_C_CQ, _C_CKV, _C_KRP = 512, 768, 896
_C_DQ, _C_DK, _C_DV = 1408, 1664, 1920
_C_PD = 2176
_C_END = 3200


def _mla_kv(ckvn, krp, w_knope, w_v, kn_gain, blk128, cos, sin):
    kpre = _mm(ckvn, w_knope) + krp
    ss = _mm_x2(kpre * kpre, blk128) * (1.0 / MLA_QK)
    kb = kpre * lax.rsqrt(ss + EPS) * kn_gain
    kb = _rope(kb, cos, sin, MLA_ROPE)
    return kb, _mm(ckvn, w_v)


def _inproj_kernel(x_ref, sh_ref, sc_ref, g1_ref, w_ref, qup_ref, wkn_ref, wv_ref, gains_ref,
                   b64_ref, b32_ref, b128_ref, ra_ref, rm_ref, rc_ref,
                   qa_o, ka_o, va_o, qb_o, kb_o, vb_o, ckv_o, kr_o, qc_o, kc_o, vc_o, pd_o,
                   *, rows_per_group):
    i = pl.program_id(0)
    g = (i * ROW_TILE) // rows_per_group
    x = x_ref[...]
    ms = jnp.mean(x * x, axis=-1, keepdims=True)
    xn = x * lax.rsqrt(ms + EPS) * g1_ref[...]
    h = xn * (1.0 + sc_ref[pl.ds(g, 1), :]) + sh_ref[pl.ds(g, 1), :]
    p = _mm(h, w_ref[...])
    gains = gains_ref[...]
    b64, b32, b128 = b64_ref[...], b32_ref[...], b128_ref[...]

    ra = ra_ref[0]
    cos_a, sin_a = ra[:, :LANES], ra[:, LANES:]
    aq = p[:, _C_AQ:_C_AK]
    ssq = _mm_x2(aq * aq, b64) * (1.0 / HEAD_DIM)
    qa = aq * lax.rsqrt(ssq + EPS) * gains[0:1, :256]
    qa = _rope(qa, _tile_lanes(cos_a, 2), _tile_lanes(sin_a, 2), HEAD_DIM)
    qa_o[...] = qa * (HEAD_DIM ** -0.5)
    ak = p[:, _C_AK:_C_AV]
    ssk = _mm_x2(ak * ak, b64[:LANES, :LANES]) * (1.0 / HEAD_DIM)
    ka = ak * lax.rsqrt(ssk + EPS) * gains[1:2, :128]
    ka_o[...] = _rope(ka, cos_a, sin_a, HEAD_DIM)
    va_o[...] = p[:, _C_AV:_C_CQ]

    rm = rm_ref[0]
    cos_m, sin_m = _tile_lanes(rm[:, :LANES], MLA_HEADS), _tile_lanes(rm[:, LANES:], MLA_HEADS)
    cq = p[:, _C_CQ:_C_CKV]
    cqn = cq * lax.rsqrt(jnp.sum(cq * cq, axis=-1, keepdims=True) * (1.0 / MLA_Q_LORA) + EPS)
    qb = _mm(cqn * gains[2:3, :256], qup_ref[...])
    ssb = _mm_x2(qb * qb, b128) * (1.0 / MLA_QK)
    qb = qb * lax.rsqrt(ssb + EPS) * gains[4:5, :]
    qb_o[...] = _rope(qb, cos_m, sin_m, MLA_ROPE) * (MLA_QK ** -0.5)
    ckv = p[:, _C_CKV:_C_KRP]
    ckvn = ckv * lax.rsqrt(jnp.mean(ckv * ckv, axis=-1, keepdims=True) + EPS) * gains[3:4, :128]
    ckv_o[...] = ckvn
    krp = p[:, _C_KRP:_C_DQ]
    kr_o[...] = krp[:, :LANES]
    kb, vb = _mla_kv(ckvn, krp, wkn_ref[...], wv_ref[...], gains[5:6, :], b128, cos_m, sin_m)
    kb_o[...] = kb
    vb_o[...] = vb

    rc = rc_ref[0]
    cos_c, sin_c = _tile_lanes(rc[:, :LANES], 2), _tile_lanes(rc[:, LANES:], 2)
    dq = p[:, _C_DQ:_C_DK]
    ssd = _mm_x2(dq * dq, b32) * (1.0 / DIFF_QK)
    qc = dq * lax.rsqrt(ssd + EPS) * gains[6:7, :256]
    qc_o[...] = _rope(qc, cos_c, sin_c, DIFF_QK) * (DIFF_QK ** -0.5)
    dk = p[:, _C_DK:_C_DV]
    ssd = _mm_x2(dk * dk, b32) * (1.0 / DIFF_QK)
    kc = dk * lax.rsqrt(ssd + EPS) * gains[7:8, :256]
    kc_o[...] = _rope(kc, cos_c, sin_c, DIFF_QK)
    vc_o[...] = p[:, _C_DV:_C_PD]

    pd_o[...] = p[:, _C_PD:_C_END]


def _inproj(x, mods, g1, wpack, tables, *, rows_per_group, n_lat):
    n, d = x.shape
    steps = n // ROW_TILE
    lat_steps = n_lat // ROW_TILE
    pos_steps = rows_per_group // ROW_TILE

    def rope_map(i):
        is_ctx = i >= lat_steps
        return (jnp.where(is_ctx, 1, 0), jnp.where(is_ctx, 0, i % pos_steps), 0)

    full = lambda shape: pl.BlockSpec(shape, lambda i: (0,) * len(shape))
    row = lambda w: pl.BlockSpec((ROW_TILE, w), lambda i: (i, 0))
    widths = (256, 128, 128, 512, 512, 256, 128, 128, 256, 256, 256, 1024)
    return pl.pallas_call(
        functools.partial(_inproj_kernel, rows_per_group=rows_per_group),
        out_shape=tuple(jax.ShapeDtypeStruct((n, w), F32) for w in widths),
        grid=(steps,),
        in_specs=[
            row(d),
            pl.BlockSpec((SUBLANES, d), lambda i: (0, 0)),
            pl.BlockSpec((SUBLANES, d), lambda i: (0, 1)),
            full((1, d)),
            full(wpack["w_all"].shape), full(wpack["q_up"].shape), full(wpack["w_knope"].shape),
            full(wpack["w_v"].shape), full(wpack["gains"].shape),
            full(wpack["b64"].shape), full(wpack["b32"].shape), full(wpack["b128"].shape),
            pl.BlockSpec((1, ROW_TILE, 2 * LANES), rope_map),
            pl.BlockSpec((1, ROW_TILE, 2 * LANES), rope_map),
            pl.BlockSpec((1, ROW_TILE, 2 * LANES), rope_map),
        ],
        out_specs=tuple(row(w) for w in widths),
        compiler_params=_cparams(("arbitrary",)),
        name="inproj",
    )(x, mods, mods, g1, wpack["w_all"], wpack["q_up"], wpack["w_knope"], wpack["w_v"],
      wpack["gains"], wpack["b64"], wpack["b32"], wpack["b128"],
      tables["a"], tables["m"], tables["c"])


def _mla_cache_kernel(ckv_ref, krp_ref, wkn_ref, wv_ref, gains_ref, b128_ref, kb_o, vb_o):
    kn = gains_ref[...][5:6, :]
    one = jnp.ones((1, MLA_HEADS * MLA_PAD), F32)
    kb, vb = _mla_kv(ckv_ref[...], krp_ref[...], wkn_ref[...], wv_ref[...], kn, b128_ref[...],
                     one, jnp.zeros_like(one))
    kb_o[...] = kb
    vb_o[...] = vb


def _mla_cache(ckv, krp, wpack):
    rows = ckv.shape[0]
    tile = min(rows, ROW_TILE)
    full = lambda shape: pl.BlockSpec(shape, lambda i: (0,) * len(shape))
    row = lambda w: pl.BlockSpec((tile, w), lambda i: (i, 0))
    kw, vw = MLA_HEADS * MLA_PAD, MLA_HEADS * MLA_V
    return pl.pallas_call(
        _mla_cache_kernel,
        out_shape=(jax.ShapeDtypeStruct((rows, kw), F32), jax.ShapeDtypeStruct((rows, vw), F32)),
        grid=(rows // tile,),
        in_specs=[row(MLA_KV_LORA), row(kw), full(wpack["w_knope"].shape), full(wpack["w_v"].shape),
                  full(wpack["gains"].shape), full(wpack["b128"].shape)],
        out_specs=(row(kw), row(vw)),
        compiler_params=_cparams(("arbitrary",)),
        name="mla_cache",
    )(ckv, krp, wpack["w_knope"], wpack["w_v"], wpack["gains"], wpack["b128"])


def _softmax_streams(qs, kv_heads, segs, s_scr):
    pieces = []
    off = 0
    for k_ref, v_ref in segs:
        n_keys = k_ref.shape[2]
        step = min(KV_CHUNK, n_keys)
        for c0 in range(0, n_keys, step):
            pieces.append((k_ref, v_ref, c0, step, off + c0))
        off += n_keys
    row_max = []
    for j, q in enumerate(qs):
        m = None
        for k_ref, _, c0, step, col in pieces:
            s = _mm_nt(q, k_ref[0, kv_heads[j], c0:c0 + step, :])
            s_scr[j, :, col:col + step] = s
            for t0 in range(0, step, LANES):
                part = s[:, t0:t0 + LANES]
                m = part if m is None else jnp.maximum(m, part)
        row_max.append(jnp.max(m, axis=-1, keepdims=True))
    outs = []
    for j in range(len(qs)):
        acc = None
        for _, v_ref, c0, step, col in pieces:
            p = jnp.exp((s_scr[j, :, col:col + step] - row_max[j]).astype(BF16))
            pv = jnp.dot(p, v_ref[0, kv_heads[j], c0:c0 + step, :], preferred_element_type=F32)
            acc = pv if acc is None else acc + pv
        den = pltpu.roll(acc, HEAD_DIM, 1)
        outs.append((acc / den)[:, :HEAD_DIM])
    return outs


def _attn_kernel(*refs, has_cache, diff, lam_init, kv_heads):
    refs = list(refs)
    if diff:
        lam_ref, sub_ref = refs.pop(0), refs.pop(0)
    q_ref, k_ref, v_ref = refs[:3]
    o_ref, s_scr = refs[-2], refs[-1]
    segs = [(k_ref, v_ref)]
    if has_cache:
        segs.append((refs[3], refs[4]))
    if not diff:
        outs = _softmax_streams([q_ref[0, j] for j in range(q_ref.shape[1])], kv_heads, segs, s_scr)
        for j, o in enumerate(outs):
            o_ref[0, j] = o.astype(o_ref.dtype)
        return
    q = q_ref[0, 0]
    lane = lax.broadcasted_iota(jnp.int32, q.shape, 1)
    zero = jnp.zeros_like(q)
    o1, o2 = _softmax_streams([jnp.where(lane < DIFF_QK, q, zero), jnp.where(lane >= DIFF_QK, q, zero)],
                              kv_heads, segs, s_scr)
    lv = lam_ref[...]
    lam = (jnp.exp(jnp.sum(lv[0:1] * lv[1:2], axis=-1, keepdims=True))
           - jnp.exp(jnp.sum(lv[2:3] * lv[3:4], axis=-1, keepdims=True)) + lam_init)
    o = o1 - lam * o2
    o = o * lax.rsqrt(jnp.mean(o * o, axis=-1, keepdims=True) + EPS) * sub_ref[...]
    o_ref[0, 0] = (o * (1.0 - lam_init)).astype(o_ref.dtype)


def _attention(q, k, v, kc, vc, *, n_req, seq, ctx_group, diff_params=None, lam_init=0.0):
    g_all, hq, s_all, dq = q.shape
    hk, dk, dvp = k.shape[1], k.shape[3], v.shape[3]
    diff = diff_params is not None
    if diff:
        qh, kh, kv_heads = 1, 1, (0, 0)
    elif hq == 2 * hk:
        qh, kh, kv_heads = 2, 1, (0, 0)
    else:
        qh, kh, kv_heads = 2, 2, (0, 1)
    steps_h = hq // qh
    extra_in, extra_specs2, extra_specs3 = [], [], []
    if diff:
        extra_in = [diff_params[0], diff_params[1]]
        extra_specs3 = [pl.BlockSpec(diff_params[0].shape, lambda b, h, i: (0, 0)),
                        pl.BlockSpec(diff_params[1].shape, lambda b, h, i: (0, 0))]
        extra_specs2 = [pl.BlockSpec(diff_params[0].shape, lambda b, h: (0, 0)),
                        pl.BlockSpec(diff_params[1].shape, lambda b, h: (0, 0))]
    past = kc.shape[2]
    body = functools.partial(_attn_kernel, diff=diff, lam_init=lam_init, kv_heads=kv_heads)
    lat = pl.pallas_call(
        functools.partial(body, has_cache=True),
        out_shape=jax.ShapeDtypeStruct((n_req, hq, s_all, HEAD_DIM), BF16),
        grid=(n_req, steps_h, s_all // Q_TILE),
        in_specs=extra_specs3 + [
            pl.BlockSpec((1, qh, Q_TILE, dq), lambda b, h, i: (b, h, i, 0)),
            pl.BlockSpec((1, kh, s_all, dk), lambda b, h, i: (b, h, 0, 0)),
            pl.BlockSpec((1, kh, s_all, dvp), lambda b, h, i: (b, h, 0, 0)),
            pl.BlockSpec((1, kh, past, dk), lambda b, h, i: (b, h, 0, 0)),
            pl.BlockSpec((1, kh, past, dvp), lambda b, h, i: (b, h, 0, 0)),
        ],
        out_specs=pl.BlockSpec((1, qh, Q_TILE, HEAD_DIM), lambda b, h, i: (b, h, i, 0)),
        scratch_shapes=[pltpu.VMEM((2, Q_TILE, s_all + past), F32)],
        compiler_params=_cparams(("arbitrary", "arbitrary", "arbitrary")),
        name="attn_latent",
    )(*extra_in, q, k, v, kc, vc)
    n_seq = s_all // seq
    ctx = pl.pallas_call(
        functools.partial(body, has_cache=False),
        out_shape=jax.ShapeDtypeStruct((1, hq, s_all, HEAD_DIM), BF16),
        grid=(n_seq, steps_h),
        in_specs=extra_specs2 + [
            pl.BlockSpec((1, qh, seq, dq), lambda s, h: (ctx_group, h, s, 0)),
            pl.BlockSpec((1, kh, seq, dk), lambda s, h: (ctx_group, h, s, 0)),
            pl.BlockSpec((1, kh, seq, dvp), lambda s, h: (ctx_group, h, s, 0)),
        ],
        out_specs=pl.BlockSpec((1, qh, seq, HEAD_DIM), lambda s, h: (0, h, s, 0)),
        scratch_shapes=[pltpu.VMEM((2, seq, seq), F32)],
        compiler_params=_cparams(("arbitrary", "arbitrary")),
        name="attn_context",
    )(*extra_in, q, k, v)
    return jnp.concatenate([lat, ctx], axis=0)


def _rprep_kernel(p_ref, pp_ref, pn_ref, mix_ref, wl_ref, vec_ref, b64_ref,
                  r_o, v_o, kk_o, e0_o, e1_o, k0_o, k1_o, b0_o, b1_o, g_o, bon_o,
                  *, lat_steps, lat_seq_steps, ctx_seq_steps):
    i = pl.program_id(0)
    is_lat = i < lat_steps
    pos = jnp.where(is_lat, i % lat_seq_steps, (i - lat_steps) % ctx_seq_steps)
    last = jnp.where(is_lat, lat_seq_steps - 1, ctx_seq_steps - 1)
    at_start = pos == 0
    at_end = pos == last
    p = p_ref[...]
    rows = lax.broadcasted_iota(jnp.int32, p.shape, 0)
    prev_edge = jnp.where(at_start, 0.0, pp_ref[SUBLANES - 1:SUBLANES, :])
    next_edge = jnp.where(at_end, 0.0, pn_ref[0:1, :])
    prev = jnp.where(rows == 0, prev_edge, pltpu.roll(p, 1, 0))
    nxt = jnp.where(rows == ROW_TILE - 1, next_edge, pltpu.roll(p, ROW_TILE - 1, 0))
    mix = mix_ref[...]
    xs = p + (prev - p) * mix[0:1] + (nxt - p) * mix[1:2]
    r, k, v = xs[:, 0:256], xs[:, 256:512], xs[:, 512:768]
    lo = xs[:, 768:1024]
    lane = lax.broadcasted_iota(jnp.int32, lo.shape, 1)
    act = jnp.where(lane < 2 * DECAY_LORA, jnp.tanh(lo),
                    jnp.where(lane < 2 * (DECAY_LORA + AAA_LORA), lo, jax.nn.sigmoid(lo)))
    lora = _mm(act, wl_ref[...])
    vec = vec_ref[...]
    z = -(vec[0:1, :] + lora[:, 0:512])
    softplus = jnp.maximum(z, 0.0) + jnp.log1p(jnp.exp(-jnp.abs(z)))
    e = jnp.exp(-softplus - 0.5)
    a = jax.nn.sigmoid(vec[1:2, :] + lora[:, 512:1024])
    g_o[...] = lora[:, 1024:1280]
    b64 = b64_ref[...]
    kk = k * vec[2:3, 0:256]
    kk = kk * lax.rsqrt(_mm_x2(kk * kk, b64) + 1e-12)
    ka = vec[2:3, 256:512]
    r_o[...] = r
    v_o[...] = v
    kk_o[...] = kk
    bonus = None
    for d, (e_o, k_o, b_o) in enumerate(((e0_o, k0_o, b0_o), (e1_o, k1_o, b1_o))):
        a_d = a[:, 256 * d:256 * (d + 1)]
        k_d = k * (1.0 + (a_d - 1.0) * ka)
        e_o[...] = e[:, 256 * d:256 * (d + 1)]
        k_o[...] = k_d
        b_o[...] = kk * a_d
        term = _mm_x2(r * k_d * vec[3:4, 256 * d:256 * (d + 1)], b64) * v
        bonus = term if bonus is None else bonus + term
    bon_o[...] = bonus


def _rwkv_prep(pd, rpack, *, n_lat, lat_seq, ctx_seq):
    n, w = pd.shape
    steps = n // ROW_TILE
    halo_blocks = n // SUBLANES
    per = ROW_TILE // SUBLANES
    full = lambda shape: pl.BlockSpec(shape, lambda i: (0,) * len(shape))
    row = lambda width: pl.BlockSpec((ROW_TILE, width), lambda i: (i, 0))
    return pl.pallas_call(
        functools.partial(_rprep_kernel, lat_steps=n_lat // ROW_TILE,
                          lat_seq_steps=lat_seq // ROW_TILE, ctx_seq_steps=ctx_seq // ROW_TILE),
        out_shape=tuple(jax.ShapeDtypeStruct((n, RWKV_W), F32) for _ in range(11)),
        grid=(steps,),
        in_specs=[
            row(w),
            pl.BlockSpec((SUBLANES, w), lambda i: (jnp.maximum(i * per - 1, 0), 0)),
            pl.BlockSpec((SUBLANES, w), lambda i: (jnp.minimum((i + 1) * per, halo_blocks - 1), 0)),
            full(rpack["mix"].shape), full(rpack["w_lora"].shape), full(rpack["vec"].shape),
            full(rpack["b64"].shape),
        ],
        out_specs=tuple(row(RWKV_W) for _ in range(11)),
        compiler_params=_cparams(("arbitrary",)),
        name="rwkv_prep",
    )(pd, pd, pd, rpack["mix"], rpack["w_lora"], rpack["vec"], rpack["b64"])


def _bmm(a, b):
    return lax.dot_general(a.astype(BF16), b.astype(BF16), (((2,), (1,)), ((0,), (0,))),
                           preferred_element_type=F32)


def _bmm_nt(a, b):
    return lax.dot_general(a.astype(BF16), b.astype(BF16), (((2,), (2,)), ((0,), (0,))),
                           preferred_element_type=F32)


def _bmm3(a, b):
    ah, al = _split2(a)
    bh, bl = _split2(b)
    return _bmm(ah, bh) + _bmm(ah, bl) + _bmm(al, bh)


def _btranspose(x):
    return jnp.stack([x[i].T for i in range(x.shape[0])])


def _scan_chunks(r, v, kk, e, k, b, h0, rev):
    n, c, _ = r.shape
    ti = lax.broadcasted_iota(jnp.int32, (n, c, c), 1)
    si = lax.broadcasted_iota(jnp.int32, (n, c, c), 2)
    order = (si - ti) * jnp.where(rev, -1, 1)
    incl = order <= 0
    strict = order < 0
    e_hi, e_mid, e_lo = _split3(e)
    incl_b = jnp.where(incl, 1.0, 0.0).astype(BF16)
    cs = _bmm(incl_b, e_hi) + _bmm(incl_b, e_mid) + _bmm(incl_b, e_lo)
    g_prev = jnp.exp(e - cs)
    g_incl = jnp.exp(-cs)
    g_inv = jnp.exp(cs)
    kkg, rg, bq, kq = kk * g_prev, r * g_incl, b * g_inv, k * g_inv
    left = jnp.concatenate([kkg, rg], axis=1)
    sb = _bmm_nt(left, bq)
    sk = _bmm_nt(left, kq)
    zero = jnp.zeros((n, c, c), F32)
    a_m = jnp.where(strict, sb[:, :c], zero)
    b_m = jnp.where(strict, sk[:, :c], zero)
    a_p = jnp.where(incl, sb[:, c:], zero)
    b_p = jnp.where(incl, sk[:, c:], zero)

    eye = (ti == si).astype(F32)
    same = (ti // SCAN_SUB) == (si // SCAN_SUB)
    a_d = jnp.where(same, a_m, zero)
    a_off = a_m - a_d
    t_d = eye - a_d
    pw = _bmm3(a_d, a_d)
    steps = int(math.log2(SCAN_SUB))
    for j in range(1, steps):
        t_d = t_d + _bmm3(t_d, pw)
        if j + 1 < steps:
            pw = _bmm3(pw, pw)
    nil = _bmm3(t_d, a_off)
    inm = eye - nil
    nblk = c // SCAN_SUB
    if nblk > 2:
        nil2 = _bmm3(nil, nil)
        inm = inm + _bmm3(inm, nil2)
    t_full = _bmm3(inm, t_d) if nblk > 1 else t_d

    bmv = _bmm(b_m, v)
    w1 = _bmm(t_full, kkg)
    z = _bmm(t_full, bmv)
    cs_t = _btranspose(cs)
    tot_col = jnp.maximum(cs_t[:, :, 0:1], cs_t[:, :, c - 1:c])
    g_end_t = jnp.exp(cs_t - tot_col)
    kt_t = _btranspose(k) * g_end_t
    bt_t = _btranspose(b) * g_end_t
    u = _bmm(w1, h0) + z
    h_new = jnp.exp(-tot_col) * h0 + _bmm(kt_t, v) - _bmm(bt_t, u)
    y = _bmm(rg, h0) - _bmm(a_p, u) + _bmm(b_p, v)
    return y, h_new


def _scan_kernel(rf, vf, kkf, ef, kf, bf, rb, vb, kkb, eb, kb, bb, h0_ref,
                 y0_ref, y1_ref, hout_ref, h_scr, *, n_seq, n_chunks):
    c = pl.program_id(1)

    @pl.when(c == 0)
    def _():
        h_scr[...] = h0_ref[...]

    groups = ((rf, vf, kkf, ef, kf, bf), (rb, vb, kkb, eb, kb, bb))
    loaded = [[ref[...] for ref in refs] for refs in groups]
    ops = []
    for j in range(6):
        ops.append(jnp.stack([loaded[d][j][s, :, RWKV_N * h:RWKV_N * (h + 1)]
                              for s in range(n_seq) for d in range(2) for h in range(RWKV_HEADS)]))
    n = n_seq * 2 * RWKV_HEADS
    chain = lax.broadcasted_iota(jnp.int32, (n, 1, 1), 0)
    rev = ((chain // RWKV_HEADS) % 2) == 1
    h0 = h_scr[...].reshape(n, RWKV_N, RWKV_N)
    y, h_new = _scan_chunks(*ops, h0, rev)
    h_scr[...] = h_new.reshape(h_scr.shape)
    for s in range(n_seq):
        for d, y_ in enumerate((y0_ref, y1_ref)):
            base = (s * 2 + d) * RWKV_HEADS
            y_[s] = jnp.concatenate([y[base + h] for h in range(RWKV_HEADS)], axis=1)

    @pl.when(c == n_chunks - 1)
    def _():
        hout_ref[...] = h_scr[...]


def _rwkv_scan(streams, h0, *, first_seq, seq_len):
    n_chunks = seq_len // SCAN_CHUNK
    total = h0.shape[0]
    sb = min(SCAN_SEQS, total)
    assert total % sb == 0 and first_seq % sb == 0
    arrs = {name: a.reshape(-1, seq_len, RWKV_W) for name, a in streams.items()}
    blk0 = first_seq // sb
    blk = (sb, SCAN_CHUNK, RWKV_W)
    fwd = pl.BlockSpec(blk, lambda s, c: (blk0 + s, c, 0))
    bwd = pl.BlockSpec(blk, lambda s, c: (blk0 + s, n_chunks - 1 - c, 0))
    st_shape = (sb,) + h0.shape[1:]
    st = pl.BlockSpec(st_shape, lambda s, c: (s, 0, 0, 0, 0))
    y_shape = jax.ShapeDtypeStruct((total, seq_len, RWKV_W), F32)
    y0, y1, h_fin = pl.pallas_call(
        functools.partial(_scan_kernel, n_seq=sb, n_chunks=n_chunks),
        out_shape=(y_shape, y_shape, jax.ShapeDtypeStruct(h0.shape, F32)),
        grid=(total // sb, n_chunks),
        in_specs=[fwd] * 6 + [bwd] * 6 + [st],
        out_specs=(pl.BlockSpec(blk, lambda s, c: (s, c, 0)),
                   pl.BlockSpec(blk, lambda s, c: (s, n_chunks - 1 - c, 0)),
                   st),
        scratch_shapes=[pltpu.VMEM(st_shape, F32)],
        compiler_params=_cparams(("arbitrary", "arbitrary")),
        name="rwkv_scan",
    )(arrs["r"], arrs["v"], arrs["kk"], arrs["e0"], arrs["k0"], arrs["b0"],
      arrs["r"], arrs["v"], arrs["kk"], arrs["e1"], arrs["k1"], arrs["b1"], h0)
    return y0, y1, h_fin


def _outproj_kernel(x_ref, oa_ref, ob_ref, oc_ref, y0_ref, y1_ref, bon_ref, g_ref, ln_ref, b64_ref,
                    wo_ref, gate_ref, sh_ref, sc_ref, g2_ref, rwh_ref, rwl_ref, rb_ref,
                    x1_o, h2_o, lg_o, *, rows_per_group):
    i = pl.program_id(0)
    g = (i * ROW_TILE) // rows_per_group
    b64 = b64_ref[...]
    y = y0_ref[...] + y1_ref[...]
    mu = _mm_x2(y, b64) * (1.0 / RWKV_N)
    dy = y - mu
    var = _mm_x2(dy * dy, b64) * (1.0 / RWKV_N)
    ln = ln_ref[...]
    yn = dy * lax.rsqrt(var + RWKV_GN_EPS) * ln[0:1] + ln[1:2]
    od = (yn + bon_ref[...]) * g_ref[...]
    wo = wo_ref[...]
    mixed = (_mm(oa_ref[...], wo[0:256]) + _mm(ob_ref[...], wo[256:512])
             + _mm(oc_ref[...], wo[512:768]) + _mm(od, wo[768:1024]))
    x1 = x_ref[...] + gate_ref[pl.ds(g, 1), :] * mixed
    x1_o[...] = x1
    ms = jnp.mean(x1 * x1, axis=-1, keepdims=True)
    h2 = (x1 * lax.rsqrt(ms + EPS) * g2_ref[...]) * (1.0 + sc_ref[pl.ds(g, 1), :]) + sh_ref[pl.ds(g, 1), :]
    h2_o[...] = h2
    hh, hl = _split2(h2)
    rwh, rwl = rwh_ref[...], rwl_ref[...]
    lg_o[...] = (jnp.dot(hh, rwh, preferred_element_type=F32) + jnp.dot(hh, rwl, preferred_element_type=F32)
                 + jnp.dot(hl, rwh, preferred_element_type=F32)) + rb_ref[...]


def _outproj(x, oa, ob, oc, y0, y1, bonus, gate, ln, b64, wo, mods, g2, rw_hi, rw_lo, rb,
             *, rows_per_group):
    n, d = x.shape
    full = lambda shape: pl.BlockSpec(shape, lambda i: (0,) * len(shape))
    row = lambda w: pl.BlockSpec((ROW_TILE, w), lambda i: (i, 0))
    mod = lambda j: pl.BlockSpec((SUBLANES, d), lambda i: (0, j))
    return pl.pallas_call(
        functools.partial(_outproj_kernel, rows_per_group=rows_per_group),
        out_shape=(jax.ShapeDtypeStruct((n, d), F32), jax.ShapeDtypeStruct((n, d), F32),
                   jax.ShapeDtypeStruct((n, LANES), F32)),
        grid=(n // ROW_TILE,),
        in_specs=[row(d), row(256), row(256), row(256), row(256), row(256), row(256), row(256),
                  full(ln.shape), full(b64.shape), full(wo.shape), mod(2), mod(3), mod(4),
                  full(g2.shape), full(rw_hi.shape), full(rw_lo.shape), full(rb.shape)],
        out_specs=(row(d), row(d), row(LANES)),
        compiler_params=_cparams(("arbitrary",)),
        name="outproj",
    )(x, oa, ob, oc, y0, y1, bonus, gate, ln, b64, wo, mods, mods, mods, g2, rw_hi, rw_lo, rb)


def _row_copy(src_hbm, row, dst, sem):
    return pltpu.make_async_copy(src_hbm.at[pl.ds(row, 1), :], dst, sem)


def _issue_rows(n_rows, start_row):
    def body(j, carry):
        for u in range(DMA_UNROLL):
            start_row(j * DMA_UNROLL + u)
        return carry
    lax.fori_loop(0, n_rows // DMA_UNROLL, body, 0)


def _moe_kernel(be_ref, nused_ref, cur_ref, nxt_ref, h_hbm, w1_ref, w2_ref, b1_ref, b2_ref,
                out_ref, xbuf, sem):
    del be_ref
    i = pl.program_id(0)
    nused = nused_ref[0]
    slot = i % 2

    def issue(idx_ref, sl):
        _issue_rows(MOE_ROWS, lambda r: _row_copy(
            h_hbm, idx_ref[0, 0, r], xbuf.at[sl, pl.ds(r, 1), :], sem.at[sl]).start())

    def wait_slot(sl):
        pltpu.make_async_copy(h_hbm.at[pl.ds(0, MOE_ROWS), :], xbuf.at[sl], sem.at[sl]).wait()

    @pl.when(i == 0)
    def _():
        issue(cur_ref, 0)

    @pl.when(i < nused)
    def _():
        wait_slot(slot)
        x = xbuf[slot].astype(BF16)
        for r in range(MOE_ROWS):
            _row_copy(h_hbm, nxt_ref[0, 0, r], xbuf.at[1 - slot, pl.ds(r, 1), :], sem.at[1 - slot]).start()
        hcat = _mm(x, w1_ref[0]) + b1_ref[0]
        glu = jnp.minimum(hcat, SWIGLU_LIMIT)
        lin1 = jnp.clip(hcat, -SWIGLU_LIMIT, SWIGLU_LIMIT) + 1.0
        width = hcat.shape[1]
        act = glu * jax.nn.sigmoid(SWIGLU_ALPHA * glu) * pltpu.roll(lin1, width - 1, 1)
        out_ref[...] = _mm(act, w2_ref[0]) + b2_ref[0]

    @pl.when(i == nused - 1)
    def _():
        wait_slot(1 - slot)

    @pl.when(i >= nused)
    def _():
        out_ref[...] = jnp.zeros_like(out_ref)


def _moe_experts(h2, block_e, nused, buf_tok, w1, w2x, b1, b2):
    n, d = h2.shape
    n_blocks = block_e.shape[0]
    dff2 = w1.shape[2]
    idx = buf_tok.reshape(n_blocks, 1, MOE_ROWS)
    wspec = lambda s1, s2: pl.BlockSpec((1, s1, s2), lambda i, be, nu: (be[i], 0, 0))
    grid_spec = pltpu.PrefetchScalarGridSpec(
        num_scalar_prefetch=2,
        grid=(n_blocks,),
        in_specs=[
            pl.BlockSpec((1, 1, MOE_ROWS), lambda i, be, nu: (i, 0, 0), memory_space=pltpu.SMEM),
            pl.BlockSpec((1, 1, MOE_ROWS), lambda i, be, nu: (jnp.minimum(i + 1, n_blocks - 1), 0, 0),
                         memory_space=pltpu.SMEM),
            pl.BlockSpec(memory_space=pl.ANY),
            wspec(d, dff2), wspec(dff2, d), wspec(1, dff2), wspec(1, d),
        ],
        out_specs=pl.BlockSpec((MOE_ROWS, d), lambda i, be, nu: (i, 0)),
        scratch_shapes=[pltpu.VMEM((2, MOE_ROWS, d), F32), pltpu.SemaphoreType.DMA((2,))],
    )
    return pl.pallas_call(
        _moe_kernel,
        out_shape=jax.ShapeDtypeStruct((n_blocks * MOE_ROWS, d), F32),
        grid_spec=grid_spec,
        compiler_params=_cparams(("arbitrary",)),
        name="moe_experts",
    )(block_e, nused, idx, idx, h2, w1, w2x, b1, b2)


def _comb_kernel(cur_ref, nxt_ref, yb_hbm, x_ref, gates_ref, gate_ref, out_ref, buf, sem,
                 *, rows_per_group, n_steps):
    i = pl.program_id(0)
    slot = i % 2
    g = (i * COMB_ROWS) // rows_per_group

    def issue(idx_ref, sl):
        def start_token(t):
            for kx in range(TOP_K):
                _row_copy(yb_hbm, idx_ref[0, 0, t * TOP_K + kx], buf.at[sl, kx, pl.ds(t, 1), :],
                          sem.at[sl]).start()
        _issue_rows(COMB_ROWS, start_token)

    @pl.when(i == 0)
    def _():
        issue(cur_ref, 0)

    @pl.when(i + 1 < n_steps)
    def _():
        issue(nxt_ref, 1 - slot)

    for kx in range(TOP_K):
        pltpu.make_async_copy(yb_hbm.at[pl.ds(0, COMB_ROWS), :], buf.at[slot, kx], sem.at[slot]).wait()
    gates = gates_ref[...]
    acc = gates[:, 0:1] * buf[slot, 0]
    for kx in range(1, TOP_K):
        acc = acc + gates[:, kx:kx + 1] * buf[slot, kx]
    out_ref[...] = x_ref[...] + gate_ref[pl.ds(g, 1), :] * acc


def _moe_combine(yb, x1, gates, slot_dest, mods, *, rows_per_group):
    n, d = x1.shape
    steps = n // COMB_ROWS
    idx = slot_dest.reshape(steps, 1, COMB_ROWS * TOP_K)
    return pl.pallas_call(
        functools.partial(_comb_kernel, rows_per_group=rows_per_group, n_steps=steps),
        out_shape=jax.ShapeDtypeStruct((n, d), F32),
        grid=(steps,),
        in_specs=[
            pl.BlockSpec((1, 1, COMB_ROWS * TOP_K), lambda i: (i, 0, 0), memory_space=pltpu.SMEM),
            pl.BlockSpec((1, 1, COMB_ROWS * TOP_K), lambda i: (jnp.minimum(i + 1, steps - 1), 0, 0),
                         memory_space=pltpu.SMEM),
            pl.BlockSpec(memory_space=pl.ANY),
            pl.BlockSpec((COMB_ROWS, d), lambda i: (i, 0)),
            pl.BlockSpec((COMB_ROWS, TOP_K), lambda i: (i, 0)),
            pl.BlockSpec((SUBLANES, d), lambda i: (0, 5)),
        ],
        out_specs=pl.BlockSpec((COMB_ROWS, d), lambda i: (i, 0)),
        scratch_shapes=[pltpu.VMEM((2, TOP_K, COMB_ROWS, d), F32), pltpu.SemaphoreType.DMA((2,))],
        compiler_params=_cparams(("arbitrary",)),
        name="moe_combine",
    )(idx, idx, yb, x1, gates, mods)


def _route(logits):
    n = logits.shape[0]
    top_val, top_idx = lax.top_k(logits, TOP_K)
    gates = jax.nn.softmax(top_val, axis=-1)
    flat_e = top_idx.reshape(-1).astype(jnp.int32)
    nk = n * TOP_K
    order = jnp.argsort(flat_e).astype(jnp.int32)
    rank = jnp.argsort(order).astype(jnp.int32)
    experts = jnp.arange(N_EXPERTS, dtype=jnp.int32)
    counts = jnp.sum((flat_e[:, None] == experts[None, :]).astype(jnp.int32), axis=0)
    padded = (counts + MOE_ROWS - 1) // MOE_ROWS * MOE_ROWS
    start = jnp.cumsum(counts) - counts
    pad_end = jnp.cumsum(padded)
    pad_start = pad_end - padded
    n_blocks = (nk + N_EXPERTS * (MOE_ROWS - 1) + MOE_ROWS - 1) // MOE_ROWS
    cap = n_blocks * MOE_ROWS
    block_start = jnp.arange(n_blocks, dtype=jnp.int32) * MOE_ROWS
    block_e = jnp.minimum(jnp.sum((pad_end[None, :] <= block_start[:, None]).astype(jnp.int32), axis=1),
                          N_EXPERTS - 1)
    nused = (pad_end[-1] // MOE_ROWS).astype(jnp.int32).reshape(1)
    row = jnp.arange(cap, dtype=jnp.int32)
    row_e = jnp.repeat(block_e, MOE_ROWS)
    within = row - pad_start[row_e]
    valid = within < counts[row_e]
    src = jnp.clip(start[row_e] + within, 0, nk - 1)
    buf_tok = jnp.where(valid, order[src] // TOP_K, 0).astype(jnp.int32)
    slot_dest = (pad_start[flat_e] + rank - start[flat_e]).astype(jnp.int32)
    return gates, buf_tok, block_e, nused, slot_dest


def _block_ones(n, blk):
    idx = np.arange(n) // blk
    return jnp.asarray(idx[:, None] == idx[None, :], dtype=BF16)


def _pad_cols(w, width):
    return jnp.pad(w, ((0, 0), (0, width - w.shape[1])))


def _place_heads(w, per_head, offset, heads=MLA_HEADS, slot=MLA_PAD):
    rows = w.shape[0]
    w3 = w.reshape(rows, heads, per_head)
    out = jnp.zeros((rows, heads, slot), w.dtype).at[:, :, offset:offset + per_head].set(w3)
    return out.reshape(rows, heads * slot)


def _pack_in_weights(P, l):
    w_in = P["w_in"][l]
    sizes = (256, 128, 128, MLA_Q_LORA, MLA_KV_LORA, MLA_ROPE, 256, 256, 256, RWKV_COLS)
    offs = np.concatenate([[0], np.cumsum(sizes)])
    seg = [w_in[:, offs[j]:offs[j + 1]] for j in range(len(sizes))]
    krp = _place_heads(jnp.tile(seg[5], (1, MLA_HEADS)), MLA_ROPE, MLA_NOPE)
    w_all = jnp.concatenate([seg[0], seg[1], seg[2], _pad_cols(seg[3], 256), seg[4], krp,
                             seg[6], seg[7], seg[8], _pad_cols(seg[9], 1024)], axis=1).astype(BF16)
    q_up = _place_heads(P["mla_q_up"][l], MLA_QK, 0)
    q_up = jnp.pad(q_up, ((0, 256 - MLA_Q_LORA), (0, 0))).astype(BF16)
    kv_up = P["mla_kv_up"][l].reshape(MLA_KV_LORA, MLA_HEADS, MLA_NOPE + MLA_V)
    w_knope = _place_heads(kv_up[:, :, :MLA_NOPE].reshape(MLA_KV_LORA, -1), MLA_NOPE, 0).astype(BF16)
    w_v = kv_up[:, :, MLA_NOPE:].reshape(MLA_KV_LORA, MLA_HEADS * MLA_V).astype(BF16)
    width = MLA_HEADS * MLA_PAD

    def rowpad(v):
        return jnp.pad(v, (0, width - v.shape[0]))

    gains = jnp.stack([
        rowpad(jnp.tile(P["gqa_qn"][l], GQA_HEADS)),
        rowpad(jnp.tile(P["gqa_kn"][l], GQA_KV_HEADS)),
        rowpad(P["mla_qa_norm"][l]),
        rowpad(P["mla_kva_norm"][l]),
        _place_heads(jnp.tile(P["mla_qn"][l], MLA_HEADS)[None], MLA_QK, 0)[0],
        _place_heads(jnp.tile(P["mla_kn"][l], MLA_HEADS)[None], MLA_QK, 0)[0],
        rowpad(jnp.tile(P["diff_qn"][l], 2 * DIFF_HEADS)),
        rowpad(jnp.tile(P["diff_kn"][l], 2 * DIFF_HEADS)),
    ]).astype(F32)
    return dict(w_all=w_all, q_up=q_up, w_knope=w_knope, w_v=w_v, gains=gains,
                b64=_block_ones(256, 64), b32=_block_ones(256, 32), b128=_block_ones(width, MLA_PAD))


def _pack_rwkv(P, l):
    mix = _pad_cols(P["rwkv_mix"][l], 1024)
    w_lora = jnp.zeros((256, 1280), F32)
    w_lora = w_lora.at[0:32, 0:256].set(P["rwkv_w2"][l, 0]).at[32:64, 256:512].set(P["rwkv_w2"][l, 1])
    w_lora = w_lora.at[64:96, 512:768].set(P["rwkv_a2"][l, 0]).at[96:128, 768:1024].set(P["rwkv_a2"][l, 1])
    w_lora = w_lora.at[128:192, 1024:1280].set(P["rwkv_g2"][l]).astype(BF16)
    vec = jnp.zeros((SUBLANES, 512), F32)
    vec = vec.at[0].set(P["rwkv_w0"][l].reshape(-1)).at[1].set(P["rwkv_a0"][l].reshape(-1))
    vec = vec.at[2].set(jnp.concatenate([P["rwkv_kk"][l], P["rwkv_ka"][l]]))
    vec = vec.at[3].set(P["rwkv_rk"][l].reshape(-1))
    return dict(mix=mix, w_lora=w_lora, vec=vec, b64=_block_ones(256, 64))


def _rope_table(seq, rot, lanes_per_tile_group, lane_offset=0):
    rows = seq // GRID_W
    row = jnp.repeat(jnp.arange(rows, dtype=F32), GRID_W)
    col = (jnp.arange(rows * GRID_W) % GRID_W).astype(F32)
    n_freq = rot // 4
    inv_freq = ROPE_BASE ** (-jnp.arange(n_freq, dtype=F32) / n_freq)
    ang = jnp.stack([row[:, None] * inv_freq, col[:, None] * inv_freq], axis=1)
    cos = jnp.cos(ang)[:, :, None, :]
    sin = jnp.sin(ang)[:, :, None, :]
    cos_g = jnp.broadcast_to(cos, (seq, 2, 2, n_freq)).reshape(seq, rot)
    sin_g = (jnp.broadcast_to(sin, (seq, 2, 2, n_freq))
             * jnp.asarray([-1.0, 1.0], F32)[None, None, :, None]).reshape(seq, rot)
    cos_t = jnp.ones((seq, LANES), F32)
    sin_t = jnp.zeros((seq, LANES), F32)
    for start in range(lane_offset, LANES, lanes_per_tile_group):
        cos_t = cos_t.at[:, start:start + rot].set(cos_g)
        sin_t = sin_t.at[:, start:start + rot].set(sin_g)
    real = jnp.concatenate([cos_t, sin_t], axis=1)
    ident = jnp.concatenate([jnp.ones((seq, LANES), F32), jnp.zeros((seq, LANES), F32)], axis=1)
    return jnp.stack([real, ident])


def _to_heads(x, groups, seq, heads):
    d = x.shape[1] // heads
    return x.reshape(groups, seq, heads, d).transpose(0, 2, 1, 3).astype(BF16)


def _with_ones(v):
    return jnp.concatenate([v.astype(BF16), jnp.ones(v.shape, BF16)], axis=-1)


def _from_heads(o):
    g, h, s, d = o.shape
    return o.transpose(0, 2, 1, 3).reshape(g * s, h * d).astype(BF16)


def kernel(x_prompt, x_sample, cache_gqa_k, cache_gqa_v, cache_mla_ckv, cache_mla_krope, cache_diff_k, cache_diff_v, state_rwkv, c, c_ctx, norm1_g, norm2_g, ada_w, ada_b, w_in, w_out, gqa_qn, gqa_kn, mla_qa_norm, mla_q_up, mla_kva_norm, mla_kv_up, mla_qn, mla_kn, diff_qn, diff_kn, diff_lam, diff_subln, rwkv_mix, rwkv_w0, rwkv_w2, rwkv_a0, rwkv_a2, rwkv_rk, rwkv_g2, rwkv_kk, rwkv_ka, rwkv_ln_g, rwkv_ln_b, router_w, router_b, moe_w1, moe_b1, moe_w2, moe_b2):
    P = dict(w_in=w_in, gqa_qn=gqa_qn, gqa_kn=gqa_kn, mla_qa_norm=mla_qa_norm, mla_q_up=mla_q_up,
             mla_kva_norm=mla_kva_norm, mla_kv_up=mla_kv_up, mla_qn=mla_qn, mla_kn=mla_kn,
             diff_qn=diff_qn, diff_kn=diff_kn, rwkv_mix=rwkv_mix, rwkv_w0=rwkv_w0, rwkv_w2=rwkv_w2,
             rwkv_a0=rwkv_a0, rwkv_a2=rwkv_a2, rwkv_rk=rwkv_rk, rwkv_g2=rwkv_g2, rwkv_kk=rwkv_kk,
             rwkv_ka=rwkv_ka)
    depth = norm1_g.shape[0]
    bc, tc, d = x_prompt.shape
    bl, tl, _ = x_sample.shape
    assert bc * tc == tl, "context tokens must fill exactly one latent-sequence group"
    assert tl % ROW_TILE == 0 and tc % ROW_TILE == 0 and tc % SCAN_CHUNK == 0
    n_lat = bl * tl
    n = n_lat + tl
    groups = bl + 1
    past = cache_gqa_k.shape[3]

    cond = jnp.concatenate([c, c_ctx[None], jnp.zeros((SUBLANES - groups, d), F32)], axis=0)
    mods_all = _adaln(cond, ada_w, ada_b)
    tables = dict(a=_rope_table(tl, HEAD_DIM, HEAD_DIM),
                  m=_rope_table(tl, MLA_ROPE, LANES, lane_offset=MLA_NOPE),
                  c=_rope_table(tl, DIFF_QK, DIFF_QK))
    x = jnp.concatenate([x_sample.reshape(n_lat, d), x_prompt.reshape(tl, d)], axis=0)
    ctx_rows = slice(n_lat, n)
    new = [[] for _ in range(7)]

    for l in range(depth):
        lam_init = 0.8 - 0.6 * math.exp(-0.3 * l)
        mods = mods_all[l]
        wpack = _pack_in_weights(P, l)
        rpack = _pack_rwkv(P, l)
        (qa, ka, va, qb, kb, vb, ckvn, kr, qc, kc, vc, pd) = _inproj(
            x, mods, norm1_g[l][None], wpack, tables, rows_per_group=tl, n_lat=n_lat)

        new[0].append(ka[ctx_rows].reshape(bc, tc, GQA_KV_HEADS, HEAD_DIM).transpose(0, 2, 1, 3))
        new[1].append(va[ctx_rows].reshape(bc, tc, GQA_KV_HEADS, HEAD_DIM).transpose(0, 2, 1, 3))
        new[2].append(ckvn[ctx_rows].reshape(bc, tc, MLA_KV_LORA))
        new[3].append(kr[ctx_rows, MLA_NOPE:MLA_NOPE + MLA_ROPE].reshape(bc, tc, MLA_ROPE))
        new[4].append(kc[ctx_rows].reshape(bc, tc, DIFF_HEADS, 2, DIFF_QK).transpose(0, 2, 3, 1, 4))
        new[5].append(vc[ctx_rows].reshape(bc, tc, DIFF_HEADS, DIFF_V).transpose(0, 2, 1, 3))

        o_a = _attention(_to_heads(qa, groups, tl, GQA_HEADS), _to_heads(ka, groups, tl, GQA_KV_HEADS),
                         _with_ones(_to_heads(va, groups, tl, GQA_KV_HEADS)),
                         cache_gqa_k[:, l].astype(BF16), _with_ones(cache_gqa_v[:, l]),
                         n_req=bl, seq=tc, ctx_group=bl)
        krp_c = _place_heads(jnp.tile(cache_mla_krope[:, l].reshape(bl * past, MLA_ROPE), (1, MLA_HEADS)),
                             MLA_ROPE, MLA_NOPE)
        kb_c, vb_c = _mla_cache(cache_mla_ckv[:, l].reshape(bl * past, MLA_KV_LORA), krp_c, wpack)
        o_b = _attention(_to_heads(qb, groups, tl, MLA_HEADS), _to_heads(kb, groups, tl, MLA_HEADS),
                         _with_ones(_to_heads(vb, groups, tl, MLA_HEADS)),
                         _to_heads(kb_c, bl, past, MLA_HEADS), _with_ones(_to_heads(vb_c, bl, past, MLA_HEADS)),
                         n_req=bl, seq=tc, ctx_group=bl)
        kc_c = cache_diff_k[:, l].transpose(0, 1, 3, 2, 4).reshape(bl, DIFF_HEADS, past, 2 * DIFF_QK)
        o_c = _attention(_to_heads(qc, groups, tl, DIFF_HEADS), _to_heads(kc, groups, tl, DIFF_HEADS),
                         _with_ones(_to_heads(vc, groups, tl, DIFF_HEADS)),
                         kc_c.astype(BF16), _with_ones(cache_diff_v[:, l]),
                         n_req=bl, seq=tc, ctx_group=bl,
                         diff_params=(diff_lam[l], diff_subln[l][None]), lam_init=lam_init)

        names = ("r", "v", "kk", "e0", "e1", "k0", "k1", "b0", "b1", "g", "bonus")
        rw = dict(zip(names, _rwkv_prep(pd, rpack, n_lat=n_lat, lat_seq=tl, ctx_seq=tc)))
        streams = {k_: rw[k_] for k_ in names[:9]}
        h0_lat = jnp.swapaxes(state_rwkv[:, l], -1, -2)
        y0l, y1l, _ = _rwkv_scan(streams, h0_lat, first_seq=0, seq_len=tl)
        h0_ctx = jnp.zeros((bc, 2, RWKV_HEADS, RWKV_N, RWKV_N), F32)
        y0c, y1c, h_ctx = _rwkv_scan(streams, h0_ctx, first_seq=n_lat // tc, seq_len=tc)
        new[6].append(jnp.swapaxes(h_ctx, -1, -2))
        y0 = jnp.concatenate([y0l.reshape(n_lat, RWKV_W), y0c.reshape(tl, RWKV_W)], axis=0)
        y1 = jnp.concatenate([y1l.reshape(n_lat, RWKV_W), y1c.reshape(tl, RWKV_W)], axis=0)

        ln = jnp.stack([rwkv_ln_g[l], rwkv_ln_b[l]])
        rw_f = _pad_cols(router_w[l], LANES)
        rw_hi = rw_f.astype(BF16)
        rw_lo = (rw_f - rw_hi.astype(F32)).astype(BF16)
        rb = _pad_cols(router_b[l][None], LANES)
        x1, h2, logits = _outproj(x, _from_heads(o_a), _from_heads(o_b), _from_heads(o_c), y0, y1,
                                  rw["bonus"], rw["g"], ln, rpack["b64"], w_out[l].astype(BF16), mods,
                                  norm2_g[l][None], rw_hi, rw_lo, rb, rows_per_group=tl)

        gates, buf_tok, block_e, nused, slot_dest = _route(logits[:, :N_EXPERTS])
        w2 = moe_w2[l].astype(BF16)
        w2x = jnp.stack([w2, jnp.zeros_like(w2)], axis=2).reshape(N_EXPERTS, 2 * w2.shape[1], d)
        yb = _moe_experts(h2, block_e, nused, buf_tok, moe_w1[l].astype(BF16), w2x,
                          moe_b1[l][:, None, :], moe_b2[l][:, None, :])
        x = _moe_combine(yb, x1, gates, slot_dest, mods, rows_per_group=tl)

    y_sample = x[:n_lat].reshape(bl, tl, d)
    y_prompt = x[n_lat:].reshape(bc, tc, d)
    return (y_prompt, y_sample) + tuple(jnp.stack(t, axis=1) for t in new)
```

```python
import functools
import math

import jax
import jax.numpy as jnp
import numpy as np
from jax import lax
from jax.experimental import pallas as pl
from jax.experimental.pallas import tpu as pltpu

F32 = jnp.float32
BF16 = jnp.bfloat16

GRID_W = 64
ROPE_BASE = 10000.0
EPS = 1e-6
HEAD_DIM = 64
GQA_HEADS, GQA_KV_HEADS = 4, 2
MLA_HEADS, MLA_Q_LORA, MLA_KV_LORA, MLA_NOPE, MLA_ROPE, MLA_V = 4, 192, 128, 64, 32, 64
MLA_QK = MLA_NOPE + MLA_ROPE
MLA_PAD = 128
DIFF_HEADS, DIFF_QK, DIFF_V = 4, 32, 64
RWKV_HEADS, RWKV_N = 4, 64
RWKV_W = RWKV_HEADS * RWKV_N
DECAY_LORA, AAA_LORA, GATE_LORA = 32, 32, 64
RWKV_GN_EPS = 64e-5
RWKV_COLS = 3 * RWKV_W + 2 * DECAY_LORA + 2 * AAA_LORA + GATE_LORA
N_EXPERTS, TOP_K = 32, 4
SWIGLU_ALPHA, SWIGLU_LIMIT = 1.702, 7.0

LANES = 128
SUBLANES = 8
VMEM_LIMIT = 48 * 1024 * 1024
ROW_TILE = 256
Q_TILE = 256
KV_CHUNK = 512
SCAN_CHUNK = 64
SCAN_SUB = 16
SCAN_SEQS = 4
MOE_ROWS = 256
COMB_ROWS = 128
ADA_COLS = 1536
DMA_UNROLL = 8


def _cparams(sem, vmem=VMEM_LIMIT):
    return pltpu.CompilerParams(dimension_semantics=sem, vmem_limit_bytes=vmem)


def _mm(a, b):
    return jnp.dot(a.astype(BF16), b.astype(BF16), preferred_element_type=F32)


def _mm_nt(a, b):
    return lax.dot_general(a.astype(BF16), b.astype(BF16), (((1,), (1,)), ((), ())),
                           preferred_element_type=F32)


def _split2(x):
    hi = x.astype(BF16)
    lo = (x - hi.astype(F32)).astype(BF16)
    return hi, lo


def _split3(x):
    hi = x.astype(BF16)
    r = x - hi.astype(F32)
    mid = r.astype(BF16)
    lo = (r - mid.astype(F32)).astype(BF16)
    return hi, mid, lo


def _mm_x2(x, ones_blk):
    hi, lo = _split2(x)
    return (jnp.dot(hi, ones_blk, preferred_element_type=F32)
            + jnp.dot(lo, ones_blk, preferred_element_type=F32))


def _mm3(a, b):
    ah, al = _split2(a)
    bh, bl = _split2(b)
    return (jnp.dot(ah, bh, preferred_element_type=F32)
            + jnp.dot(ah, bl, preferred_element_type=F32)
            + jnp.dot(al, bh, preferred_element_type=F32))


def _rope(x, cos, sin_signed, rot):
    n = x.shape[-1]
    q = rot // 4
    lane = lax.broadcasted_iota(jnp.int32, x.shape, 1)
    first = (lane & (rot // 2 - 1)) < q
    partner = jnp.where(first, pltpu.roll(x, n - q, 1), pltpu.roll(x, q, 1))
    return x * cos + partner * sin_signed


def _tile_lanes(x, reps):
    return x if reps == 1 else jnp.concatenate([x] * reps, axis=1)


def _ada_kernel(c_ref, w_ref, b_ref, o_ref):
    c = c_ref[...]
    o_ref[0] = _mm(c * jax.nn.sigmoid(c), w_ref[0]) + b_ref[0]


def _adaln(cond, ada_w, ada_b):
    depth, d, cols = ada_w.shape
    rows = cond.shape[0]
    return pl.pallas_call(
        _ada_kernel,
        out_shape=jax.ShapeDtypeStruct((depth, rows, cols), F32),
        grid=(depth, cols // ADA_COLS),
        in_specs=[
            pl.BlockSpec((rows, d), lambda l, j: (0, 0)),
            pl.BlockSpec((1, d, ADA_COLS), lambda l, j: (l, 0, j)),
            pl.BlockSpec((1, 1, ADA_COLS), lambda l, j: (l, 0, j)),
        ],
        out_specs=pl.BlockSpec((1, rows, ADA_COLS), lambda l, j: (l, 0, j)),
        compiler_params=_cparams(("arbitrary", "arbitrary")),
        name="adaln",
    )(cond, ada_w, ada_b.reshape(depth, 1, cols))


_C_AQ, _C_AK, _C_AV = 0, 256, 384
_C_CQ, _C_CKV, _C_KRP = 512, 768, 896
_C_DQ, _C_DK, _C_DV = 1408, 1664, 1920
_C_PD = 2176
_C_END = 3200


def _mla_kv(ckvn, krp, w_knope, w_v, kn_gain, blk128, cos, sin):
    kpre = _mm(ckvn, w_knope) + krp
    ss = _mm_x2(kpre * kpre, blk128) * (1.0 / MLA_QK)
    kb = kpre * lax.rsqrt(ss + EPS) * kn_gain
    kb = _rope(kb, cos, sin, MLA_ROPE)
    return kb, _mm(ckvn, w_v)


def _store_heads(o_ref, x, width, ones=False):
    for h in range(o_ref.shape[1]):
        piece = x[:, width * h:width * (h + 1)]
        if ones:
            piece = jnp.concatenate([piece, jnp.ones_like(piece)], axis=1)
        o_ref[0, h] = piece.astype(o_ref.dtype)


def _inproj_kernel(x_ref, sh_ref, sc_ref, g1_ref, w_ref, qup_ref, wkn_ref, wv_ref, gains_ref,
                   b64_ref, b32_ref, b128_ref, ra_ref, rm_ref, rc_ref,
                   qa_h, ka_h, va_h, qb_h, kb_h, vb_h, qc_h, kc_h, vc_h,
                   ka_o, va_o, ckv_o, kr_o, kc_o, vc_o, pd_o,
                   *, rows_per_group):
    i = pl.program_id(0)
    g = (i * ROW_TILE) // rows_per_group
    x = x_ref[...]
    ms = jnp.mean(x * x, axis=-1, keepdims=True)
    xn = x * lax.rsqrt(ms + EPS) * g1_ref[...]
    h = xn * (1.0 + sc_ref[pl.ds(g, 1), :]) + sh_ref[pl.ds(g, 1), :]
    p = _mm(h, w_ref[...])
    gains = gains_ref[...]
    b64, b32, b128 = b64_ref[...], b32_ref[...], b128_ref[...]

    ra = ra_ref[0]
    cos_a, sin_a = ra[:, :LANES], ra[:, LANES:]
    aq = p[:, _C_AQ:_C_AK]
    ssq = _mm_x2(aq * aq, b64) * (1.0 / HEAD_DIM)
    qa = aq * lax.rsqrt(ssq + EPS) * gains[0:1, :256]
    qa = _rope(qa, _tile_lanes(cos_a, 2), _tile_lanes(sin_a, 2), HEAD_DIM)
    _store_heads(qa_h, qa * (HEAD_DIM ** -0.5), HEAD_DIM)
    ak = p[:, _C_AK:_C_AV]
    ssk = _mm_x2(ak * ak, b64[:LANES, :LANES]) * (1.0 / HEAD_DIM)
    ka = ak * lax.rsqrt(ssk + EPS) * gains[1:2, :128]
    ka = _rope(ka, cos_a, sin_a, HEAD_DIM)
    ka_o[...] = ka
    _store_heads(ka_h, ka, HEAD_DIM)
    va = p[:, _C_AV:_C_CQ]
    va_o[...] = va
    _store_heads(va_h, va, HEAD_DIM, ones=True)

    rm = rm_ref[0]
    cos_m, sin_m = _tile_lanes(rm[:, :LANES], MLA_HEADS), _tile_lanes(rm[:, LANES:], MLA_HEADS)
    cq = p[:, _C_CQ:_C_CKV]
    cqn = cq * lax.rsqrt(jnp.sum(cq * cq, axis=-1, keepdims=True) * (1.0 / MLA_Q_LORA) + EPS)
    qb = _mm(cqn * gains[2:3, :256], qup_ref[...])
    ssb = _mm_x2(qb * qb, b128) * (1.0 / MLA_QK)
    qb = qb * lax.rsqrt(ssb + EPS) * gains[4:5, :]
    _store_heads(qb_h, _rope(qb, cos_m, sin_m, MLA_ROPE) * (MLA_QK ** -0.5), MLA_PAD)
    ckv = p[:, _C_CKV:_C_KRP]
    ckvn = ckv * lax.rsqrt(jnp.mean(ckv * ckv, axis=-1, keepdims=True) + EPS) * gains[3:4, :128]
    ckv_o[...] = ckvn
    krp = p[:, _C_KRP:_C_DQ]
    kr_o[...] = krp[:, :LANES]
    kb, vb = _mla_kv(ckvn, krp, wkn_ref[...], wv_ref[...], gains[5:6, :], b128, cos_m, sin_m)
    _store_heads(kb_h, kb, MLA_PAD)
    _store_heads(vb_h, vb, MLA_V, ones=True)

    rc = rc_ref[0]
    cos_c, sin_c = _tile_lanes(rc[:, :LANES], 2), _tile_lanes(rc[:, LANES:], 2)
    dq = p[:, _C_DQ:_C_DK]
    ssd = _mm_x2(dq * dq, b32) * (1.0 / DIFF_QK)
    qc = dq * lax.rsqrt(ssd + EPS) * gains[6:7, :256]
    _store_heads(qc_h, _rope(qc, cos_c, sin_c, DIFF_QK) * (DIFF_QK ** -0.5), 2 * DIFF_QK)
    dk = p[:, _C_DK:_C_DV]
    ssd = _mm_x2(dk * dk, b32) * (1.0 / DIFF_QK)
    kc = dk * lax.rsqrt(ssd + EPS) * gains[7:8, :256]
    kc = _rope(kc, cos_c, sin_c, DIFF_QK)
    kc_o[...] = kc
    _store_heads(kc_h, kc, 2 * DIFF_QK)
    vc = p[:, _C_DV:_C_PD]
    vc_o[...] = vc
    _store_heads(vc_h, vc, DIFF_V, ones=True)

    pd_o[...] = p[:, _C_PD:_C_END]


def _inproj(x, mods, g1, wpack, tables, *, rows_per_group, n_lat):
    n, d = x.shape
    steps = n // ROW_TILE
    lat_steps = n_lat // ROW_TILE
    pos_steps = rows_per_group // ROW_TILE

    def rope_map(i):
        is_ctx = i >= lat_steps
        return (jnp.where(is_ctx, 1, 0), jnp.where(is_ctx, 0, i % pos_steps), 0)

    full = lambda shape: pl.BlockSpec(shape, lambda i: (0,) * len(shape))
    row = lambda w: pl.BlockSpec((ROW_TILE, w), lambda i: (i, 0))
    groups = n // rows_per_group
    head_shapes = ((GQA_HEADS, HEAD_DIM), (GQA_KV_HEADS, HEAD_DIM), (GQA_KV_HEADS, 2 * HEAD_DIM),
                   (MLA_HEADS, MLA_PAD), (MLA_HEADS, MLA_PAD), (MLA_HEADS, 2 * MLA_V),
                   (DIFF_HEADS, 2 * DIFF_QK), (DIFF_HEADS, 2 * DIFF_QK), (DIFF_HEADS, 2 * DIFF_V))
    heads = lambda hh, w: pl.BlockSpec((1, hh, ROW_TILE, w), lambda i: (i // pos_steps, 0, i % pos_steps, 0))
    widths = (128, 128, 128, 128, 256, 256, 1024)
    return pl.pallas_call(
        functools.partial(_inproj_kernel, rows_per_group=rows_per_group),
        out_shape=(tuple(jax.ShapeDtypeStruct((groups, hh, rows_per_group, w), BF16) for hh, w in head_shapes)
                   + tuple(jax.ShapeDtypeStruct((n, w), F32) for w in widths)),
        grid=(steps,),
        in_specs=[
            row(d),
            pl.BlockSpec((SUBLANES, d), lambda i: (0, 0)),
            pl.BlockSpec((SUBLANES, d), lambda i: (0, 1)),
            full((1, d)),
            full(wpack["w_all"].shape), full(wpack["q_up"].shape), full(wpack["w_knope"].shape),
            full(wpack["w_v"].shape), full(wpack["gains"].shape),
            full(wpack["b64"].shape), full(wpack["b32"].shape), full(wpack["b128"].shape),
            pl.BlockSpec((1, ROW_TILE, 2 * LANES), rope_map),
            pl.BlockSpec((1, ROW_TILE, 2 * LANES), rope_map),
            pl.BlockSpec((1, ROW_TILE, 2 * LANES), rope_map),
        ],
        out_specs=tuple(heads(hh, w) for hh, w in head_shapes) + tuple(row(w) for w in widths),
        compiler_params=_cparams(("arbitrary",)),
        name="inproj",
    )(x, mods, mods, g1, wpack["w_all"], wpack["q_up"], wpack["w_knope"], wpack["w_v"],
      wpack["gains"], wpack["b64"], wpack["b32"], wpack["b128"],
      tables["a"], tables["m"], tables["c"])


def _mla_cache_kernel(ckv_ref, krp_ref, wkn_ref, wv_ref, gains_ref, b128_ref, kb_o, vb_o):
    kn = gains_ref[...][5:6, :]
    one = jnp.ones((1, MLA_HEADS * MLA_PAD), F32)
    kb, vb = _mla_kv(ckv_ref[...], krp_ref[...], wkn_ref[...], wv_ref[...], kn, b128_ref[...],
                     one, jnp.zeros_like(one))
    kb_o[...] = kb
    vb_o[...] = vb


def _mla_cache(ckv, krp, wpack):
    rows = ckv.shape[0]
    tile = min(rows, ROW_TILE)
    full = lambda shape: pl.BlockSpec(shape, lambda i: (0,) * len(shape))
    row = lambda w: pl.BlockSpec((tile, w), lambda i: (i, 0))
    kw, vw = MLA_HEADS * MLA_PAD, MLA_HEADS * MLA_V
    return pl.pallas_call(
        _mla_cache_kernel,
        out_shape=(jax.ShapeDtypeStruct((rows, kw), F32), jax.ShapeDtypeStruct((rows, vw), F32)),
        grid=(rows // tile,),
        in_specs=[row(MLA_KV_LORA), row(kw), full(wpack["w_knope"].shape), full(wpack["w_v"].shape),
                  full(wpack["gains"].shape), full(wpack["b128"].shape)],
        out_specs=(row(kw), row(vw)),
        compiler_params=_cparams(("arbitrary",)),
        name="mla_cache",
    )(ckv, krp, wpack["w_knope"], wpack["w_v"], wpack["gains"], wpack["b128"])


def _softmax_streams(qs, kv_heads, segs, s_scr):
    pieces = []
    off = 0
    for k_ref, v_ref in segs:
        n_keys = k_ref.shape[2]
        step = min(KV_CHUNK, n_keys)
        for c0 in range(0, n_keys, step):
            pieces.append((k_ref, v_ref, c0, step, off + c0))
        off += n_keys
    row_max = []
    for j, q in enumerate(qs):
        m = None
        for k_ref, _, c0, step, col in pieces:
            s = _mm_nt(q, k_ref[0, kv_heads[j], c0:c0 + step, :])
            s_scr[j, :, col:col + step] = s
            for t0 in range(0, step, LANES):
                part = s[:, t0:t0 + LANES]
                m = part if m is None else jnp.maximum(m, part)
        row_max.append(jnp.max(m, axis=-1, keepdims=True))
    outs = []
    for j in range(len(qs)):
        acc = None
        for _, v_ref, c0, step, col in pieces:
            p = jnp.exp((s_scr[j, :, col:col + step] - row_max[j]).astype(BF16))
            pv = jnp.dot(p, v_ref[0, kv_heads[j], c0:c0 + step, :], preferred_element_type=F32)
            acc = pv if acc is None else acc + pv
        den = pltpu.roll(acc, HEAD_DIM, 1)
        outs.append((acc / den)[:, :HEAD_DIM])
    return outs


def _attn_kernel(*refs, has_cache, diff, lam_init, kv_heads):
    refs = list(refs)
    if diff:
        lam_ref, sub_ref = refs.pop(0), refs.pop(0)
    q_ref, k_ref, v_ref = refs[:3]
    o_ref, s_scr = refs[-2], refs[-1]
    segs = [(k_ref, v_ref)]
    if has_cache:
        segs.append((refs[3], refs[4]))
    if not diff:
        outs = _softmax_streams([q_ref[0, j] for j in range(q_ref.shape[1])], kv_heads, segs, s_scr)
        o_ref[...] = jnp.concatenate(outs, axis=1).astype(o_ref.dtype)
        return
    qs = []
    for j in range(q_ref.shape[1]):
        q = q_ref[0, j]
        lane = lax.broadcasted_iota(jnp.int32, q.shape, 1)
        zero = jnp.zeros_like(q)
        qs += [jnp.where(lane < DIFF_QK, q, zero), jnp.where(lane >= DIFF_QK, q, zero)]
    outs = _softmax_streams(qs, kv_heads, segs, s_scr)
    lv = lam_ref[...]
    lam = (jnp.exp(jnp.sum(lv[0:1] * lv[1:2], axis=-1, keepdims=True))
           - jnp.exp(jnp.sum(lv[2:3] * lv[3:4], axis=-1, keepdims=True)) + lam_init)
    heads = []
    for j in range(q_ref.shape[1]):
        o = outs[2 * j] - lam * outs[2 * j + 1]
        o = o * lax.rsqrt(jnp.mean(o * o, axis=-1, keepdims=True) + EPS) * sub_ref[...]
        heads.append(o * (1.0 - lam_init))
    o_ref[...] = jnp.concatenate(heads, axis=1).astype(o_ref.dtype)


def _attention(q, k, v, kc, vc, *, n_req, seq, ctx_group, diff_params=None, lam_init=0.0):
    g_all, hq, s_all, dq = q.shape
    hk, dk, dvp = k.shape[1], k.shape[3], v.shape[3]
    diff = diff_params is not None
    qh = 2
    if diff:
        kh, kv_heads = 2, (0, 0, 1, 1)
    elif hq == 2 * hk:
        kh, kv_heads = 1, (0, 0)
    else:
        kh, kv_heads = 2, (0, 1)
    n_streams = len(kv_heads)
    steps_h = hq // qh
    out_w = qh * HEAD_DIM
    q_tiles = s_all // Q_TILE
    extra_in, extra_specs2, extra_specs3 = [], [], []
    if diff:
        extra_in = [diff_params[0], diff_params[1]]
        extra_specs3 = [pl.BlockSpec(diff_params[0].shape, lambda b, h, i: (0, 0)),
                        pl.BlockSpec(diff_params[1].shape, lambda b, h, i: (0, 0))]
        extra_specs2 = [pl.BlockSpec(diff_params[0].shape, lambda b, h: (0, 0)),
                        pl.BlockSpec(diff_params[1].shape, lambda b, h: (0, 0))]
    past = kc.shape[2]
    body = functools.partial(_attn_kernel, diff=diff, lam_init=lam_init, kv_heads=kv_heads)
    lat = pl.pallas_call(
        functools.partial(body, has_cache=True),
        out_shape=jax.ShapeDtypeStruct((n_req * s_all, hq * HEAD_DIM), BF16),
        grid=(n_req, steps_h, q_tiles),
        in_specs=extra_specs3 + [
            pl.BlockSpec((1, qh, Q_TILE, dq), lambda b, h, i: (b, h, i, 0)),
            pl.BlockSpec((1, kh, s_all, dk), lambda b, h, i: (b, h, 0, 0)),
            pl.BlockSpec((1, kh, s_all, dvp), lambda b, h, i: (b, h, 0, 0)),
            pl.BlockSpec((1, kh, past, dk), lambda b, h, i: (b, h, 0, 0)),
            pl.BlockSpec((1, kh, past, dvp), lambda b, h, i: (b, h, 0, 0)),
        ],
        out_specs=pl.BlockSpec((Q_TILE, out_w), lambda b, h, i: (b * q_tiles + i, h)),
        scratch_shapes=[pltpu.VMEM((n_streams, Q_TILE, s_all + past), F32)],
        compiler_params=_cparams(("arbitrary", "arbitrary", "arbitrary")),
        name="attn_latent",
    )(*extra_in, q, k, v, kc, vc)
    n_seq = s_all // seq
    ctx = pl.pallas_call(
        functools.partial(body, has_cache=False),
        out_shape=jax.ShapeDtypeStruct((s_all, hq * HEAD_DIM), BF16),
        grid=(n_seq, steps_h),
        in_specs=extra_specs2 + [
            pl.BlockSpec((1, qh, seq, dq), lambda s, h: (ctx_group, h, s, 0)),
            pl.BlockSpec((1, kh, seq, dk), lambda s, h: (ctx_group, h, s, 0)),
            pl.BlockSpec((1, kh, seq, dvp), lambda s, h: (ctx_group, h, s, 0)),
        ],
        out_specs=pl.BlockSpec((seq, out_w), lambda s, h: (s, h)),
        scratch_shapes=[pltpu.VMEM((n_streams, seq, seq), F32)],
        compiler_params=_cparams(("arbitrary", "arbitrary")),
        name="attn_context",
    )(*extra_in, q, k, v)
    return jnp.concatenate([lat, ctx], axis=0)


def _rprep_kernel(p_ref, pp_ref, pn_ref, mix_ref, wl_ref, vec_ref, b64_ref,
                  r_o, v_o, kk_o, e0_o, e1_o, k0_o, k1_o, b0_o, b1_o, g_o, bon_o,
                  *, lat_steps, lat_seq_steps, ctx_seq_steps):
    i = pl.program_id(0)
    is_lat = i < lat_steps
    pos = jnp.where(is_lat, i % lat_seq_steps, (i - lat_steps) % ctx_seq_steps)
    last = jnp.where(is_lat, lat_seq_steps - 1, ctx_seq_steps - 1)
    at_start = pos == 0
    at_end = pos == last
    p = p_ref[...]
    rows = lax.broadcasted_iota(jnp.int32, p.shape, 0)
    prev_edge = jnp.where(at_start, 0.0, pp_ref[SUBLANES - 1:SUBLANES, :])
    next_edge = jnp.where(at_end, 0.0, pn_ref[0:1, :])
    prev = jnp.where(rows == 0, prev_edge, pltpu.roll(p, 1, 0))
    nxt = jnp.where(rows == ROW_TILE - 1, next_edge, pltpu.roll(p, ROW_TILE - 1, 0))
    mix = mix_ref[...]
    xs = p + (prev - p) * mix[0:1] + (nxt - p) * mix[1:2]
    r, k, v = xs[:, 0:256], xs[:, 256:512], xs[:, 512:768]
    lo = xs[:, 768:1024]
    lane = lax.broadcasted_iota(jnp.int32, lo.shape, 1)
    act = jnp.where(lane < 2 * DECAY_LORA, jnp.tanh(lo),
                    jnp.where(lane < 2 * (DECAY_LORA + AAA_LORA), lo, jax.nn.sigmoid(lo)))
    lora = _mm(act, wl_ref[...])
    vec = vec_ref[...]
    z = -(vec[0:1, :] + lora[:, 0:512])
    softplus = jnp.maximum(z, 0.0) + jnp.log1p(jnp.exp(-jnp.abs(z)))
    e = jnp.exp(-softplus - 0.5)
    a = jax.nn.sigmoid(vec[1:2, :] + lora[:, 512:1024])
    g_o[...] = lora[:, 1024:1280]
    b64 = b64_ref[...]
    kk = k * vec[2:3, 0:256]
    kk = kk * lax.rsqrt(_mm_x2(kk * kk, b64) + 1e-12)
    ka = vec[2:3, 256:512]
    r_o[...] = r
    v_o[...] = v
    kk_o[...] = kk
    bonus = None
    for d, (e_o, k_o, b_o) in enumerate(((e0_o, k0_o, b0_o), (e1_o, k1_o, b1_o))):
        a_d = a[:, 256 * d:256 * (d + 1)]
        k_d = k * (1.0 + (a_d - 1.0) * ka)
        e_o[...] = e[:, 256 * d:256 * (d + 1)]
        k_o[...] = k_d
        b_o[...] = kk * a_d
        term = _mm_x2(r * k_d * vec[3:4, 256 * d:256 * (d + 1)], b64) * v
        bonus = term if bonus is None else bonus + term
    bon_o[...] = bonus


def _rwkv_prep(pd, rpack, *, n_lat, lat_seq, ctx_seq):
    n, w = pd.shape
    steps = n // ROW_TILE
    halo_blocks = n // SUBLANES
    per = ROW_TILE // SUBLANES
    full = lambda shape: pl.BlockSpec(shape, lambda i: (0,) * len(shape))
    row = lambda width: pl.BlockSpec((ROW_TILE, width), lambda i: (i, 0))
    return pl.pallas_call(
        functools.partial(_rprep_kernel, lat_steps=n_lat // ROW_TILE,
                          lat_seq_steps=lat_seq // ROW_TILE, ctx_seq_steps=ctx_seq // ROW_TILE),
        out_shape=tuple(jax.ShapeDtypeStruct((n, RWKV_W), F32) for _ in range(11)),
        grid=(steps,),
        in_specs=[
            row(w),
            pl.BlockSpec((SUBLANES, w), lambda i: (jnp.maximum(i * per - 1, 0), 0)),
            pl.BlockSpec((SUBLANES, w), lambda i: (jnp.minimum((i + 1) * per, halo_blocks - 1), 0)),
            full(rpack["mix"].shape), full(rpack["w_lora"].shape), full(rpack["vec"].shape),
            full(rpack["b64"].shape),
        ],
        out_specs=tuple(row(RWKV_W) for _ in range(11)),
        compiler_params=_cparams(("arbitrary",)),
        name="rwkv_prep",
    )(pd, pd, pd, rpack["mix"], rpack["w_lora"], rpack["vec"], rpack["b64"])


def _bmm(a, b):
    return lax.dot_general(a.astype(BF16), b.astype(BF16), (((2,), (1,)), ((0,), (0,))),
                           preferred_element_type=F32)


def _bmm_nt(a, b):
    return lax.dot_general(a.astype(BF16), b.astype(BF16), (((2,), (2,)), ((0,), (0,))),
                           preferred_element_type=F32)


def _bmm3(a, b):
    ah, al = _split2(a)
    bh, bl = _split2(b)
    return _bmm(ah, bh) + _bmm(ah, bl) + _bmm(al, bh)


def _btranspose(x):
    return jnp.stack([x[i].T for i in range(x.shape[0])])


def _scan_chunks(r, v, kk, e, k, b, h0, rev):
    n, c, _ = r.shape
    ti = lax.broadcasted_iota(jnp.int32, (n, c, c), 1)
    si = lax.broadcasted_iota(jnp.int32, (n, c, c), 2)
    order = (si - ti) * jnp.where(rev, -1, 1)
    incl = order <= 0
    strict = order < 0
    e_hi, e_mid, e_lo = _split3(e)
    incl_b = jnp.where(incl, 1.0, 0.0).astype(BF16)
    cs = _bmm(incl_b, e_hi) + _bmm(incl_b, e_mid) + _bmm(incl_b, e_lo)
    g_prev = jnp.exp(e - cs)
    g_incl = jnp.exp(-cs)
    g_inv = jnp.exp(cs)
    kkg, rg, bq, kq = kk * g_prev, r * g_incl, b * g_inv, k * g_inv
    left = jnp.concatenate([kkg, rg], axis=1)
    sb = _bmm_nt(left, bq)
    sk = _bmm_nt(left, kq)
    zero = jnp.zeros((n, c, c), F32)
    a_m = jnp.where(strict, sb[:, :c], zero)
    b_m = jnp.where(strict, sk[:, :c], zero)
    a_p = jnp.where(incl, sb[:, c:], zero)
    b_p = jnp.where(incl, sk[:, c:], zero)

    eye = (ti == si).astype(F32)
    same = (ti // SCAN_SUB) == (si // SCAN_SUB)
    a_d = jnp.where(same, a_m, zero)
    a_off = a_m - a_d
    t_d = eye - a_d
    pw = _bmm3(a_d, a_d)
    steps = int(math.log2(SCAN_SUB))
    for j in range(1, steps):
        t_d = t_d + _bmm3(t_d, pw)
        if j + 1 < steps:
            pw = _bmm3(pw, pw)
    nil = _bmm3(t_d, a_off)
    inm = eye - nil
    nblk = c // SCAN_SUB
    if nblk > 2:
        nil2 = _bmm3(nil, nil)
        inm = inm + _bmm3(inm, nil2)
    t_full = _bmm3(inm, t_d) if nblk > 1 else t_d

    bmv = _bmm(b_m, v)
    w1 = _bmm(t_full, kkg)
    z = _bmm(t_full, bmv)
    cs_t = _btranspose(cs)
    tot_col = jnp.maximum(cs_t[:, :, 0:1], cs_t[:, :, c - 1:c])
    g_end_t = jnp.exp(cs_t - tot_col)
    kt_t = _btranspose(k) * g_end_t
    bt_t = _btranspose(b) * g_end_t
    u = _bmm(w1, h0) + z
    h_new = jnp.exp(-tot_col) * h0 + _bmm(kt_t, v) - _bmm(bt_t, u)
    y = _bmm(rg, h0) - _bmm(a_p, u) + _bmm(b_p, v)
    return y, h_new


def _scan_kernel(rf, vf, kkf, ef, kf, bf, rb, vb, kkb, eb, kb, bb, h0_ref,
                 y0_ref, y1_ref, hout_ref, h_scr, *, n_seq, n_chunks):
    c = pl.program_id(1)

    @pl.when(c == 0)
    def _():
        h_scr[...] = h0_ref[...]

    groups = ((rf, vf, kkf, ef, kf, bf), (rb, vb, kkb, eb, kb, bb))
    loaded = [[ref[...] for ref in refs] for refs in groups]
    ops = []
    for j in range(6):
        ops.append(jnp.stack([loaded[d][j][s, :, RWKV_N * h:RWKV_N * (h + 1)]
                              for s in range(n_seq) for d in range(2) for h in range(RWKV_HEADS)]))
    n = n_seq * 2 * RWKV_HEADS
    chain = lax.broadcasted_iota(jnp.int32, (n, 1, 1), 0)
    rev = ((chain // RWKV_HEADS) % 2) == 1
    h0 = h_scr[...].reshape(n, RWKV_N, RWKV_N)
    y, h_new = _scan_chunks(*ops, h0, rev)
    h_scr[...] = h_new.reshape(h_scr.shape)
    for s in range(n_seq):
        for d, y_ in enumerate((y0_ref, y1_ref)):
            base = (s * 2 + d) * RWKV_HEADS
            y_[s] = jnp.concatenate([y[base + h] for h in range(RWKV_HEADS)], axis=1)

    @pl.when(c == n_chunks - 1)
    def _():
        hout_ref[...] = h_scr[...]


def _rwkv_scan(streams, h0, *, first_seq, seq_len):
    n_chunks = seq_len // SCAN_CHUNK
    total = h0.shape[0]
    sb = min(SCAN_SEQS, total)
    assert total % sb == 0 and first_seq % sb == 0
    arrs = {name: a.reshape(-1, seq_len, RWKV_W) for name, a in streams.items()}
    blk0 = first_seq // sb
    blk = (sb, SCAN_CHUNK, RWKV_W)
    fwd = pl.BlockSpec(blk, lambda s, c: (blk0 + s, c, 0))
    bwd = pl.BlockSpec(blk, lambda s, c: (blk0 + s, n_chunks - 1 - c, 0))
    st_shape = (sb,) + h0.shape[1:]
    st = pl.BlockSpec(st_shape, lambda s, c: (s, 0, 0, 0, 0))
    y_shape = jax.ShapeDtypeStruct((total, seq_len, RWKV_W), F32)
    y0, y1, h_fin = pl.pallas_call(
        functools.partial(_scan_kernel, n_seq=sb, n_chunks=n_chunks),
        out_shape=(y_shape, y_shape, jax.ShapeDtypeStruct(h0.shape, F32)),
        grid=(total // sb, n_chunks),
        in_specs=[fwd] * 6 + [bwd] * 6 + [st],
        out_specs=(pl.BlockSpec(blk, lambda s, c: (s, c, 0)),
                   pl.BlockSpec(blk, lambda s, c: (s, n_chunks - 1 - c, 0)),
                   st),
        scratch_shapes=[pltpu.VMEM(st_shape, F32)],
        compiler_params=_cparams(("arbitrary", "arbitrary")),
        name="rwkv_scan",
    )(arrs["r"], arrs["v"], arrs["kk"], arrs["e0"], arrs["k0"], arrs["b0"],
      arrs["r"], arrs["v"], arrs["kk"], arrs["e1"], arrs["k1"], arrs["b1"], h0)
    return y0, y1, h_fin


def _store_token_tiles(ref, x):
    rows, width = x.shape
    tiles = width // LANES
    for j in range(tiles):
        ref[pl.ds(j, rows, stride=tiles), :] = x[:, LANES * j:LANES * (j + 1)]


def _load_token_tiles(ref, lead, rows, tiles):
    idx = tuple(lead)
    return jnp.concatenate([ref[idx + (pl.ds(j, rows, stride=tiles), slice(None))] for j in range(tiles)],
                           axis=1)


def _outproj_kernel(x_ref, oa_ref, ob_ref, oc_ref, y0_ref, y1_ref, bon_ref, g_ref, ln_ref, b64_ref,
                    wo_ref, gate_ref, sh_ref, sc_ref, g2_ref, rwh_ref, rwl_ref, rb_ref,
                    x1_o, h2_o, lg_o, *, rows_per_group):
    i = pl.program_id(0)
    g = (i * ROW_TILE) // rows_per_group
    b64 = b64_ref[...]
    y = y0_ref[...] + y1_ref[...]
    mu = _mm_x2(y, b64) * (1.0 / RWKV_N)
    dy = y - mu
    var = _mm_x2(dy * dy, b64) * (1.0 / RWKV_N)
    ln = ln_ref[...]
    yn = dy * lax.rsqrt(var + RWKV_GN_EPS) * ln[0:1] + ln[1:2]
    od = (yn + bon_ref[...]) * g_ref[...]
    wo = wo_ref[...]
    mixed = (_mm(oa_ref[...], wo[0:256]) + _mm(ob_ref[...], wo[256:512])
             + _mm(oc_ref[...], wo[512:768]) + _mm(od, wo[768:1024]))
    x1 = x_ref[...] + gate_ref[pl.ds(g, 1), :] * mixed
    x1_o[...] = x1
    ms = jnp.mean(x1 * x1, axis=-1, keepdims=True)
    h2 = (x1 * lax.rsqrt(ms + EPS) * g2_ref[...]) * (1.0 + sc_ref[pl.ds(g, 1), :]) + sh_ref[pl.ds(g, 1), :]
    _store_token_tiles(h2_o, h2)
    hh, hl = _split2(h2)
    rwh, rwl = rwh_ref[...], rwl_ref[...]
    lg_o[...] = (jnp.dot(hh, rwh, preferred_element_type=F32) + jnp.dot(hh, rwl, preferred_element_type=F32)
                 + jnp.dot(hl, rwh, preferred_element_type=F32)) + rb_ref[...]


def _outproj(x, oa, ob, oc, y0, y1, bonus, gate, ln, b64, wo, mods, g2, rw_hi, rw_lo, rb,
             *, rows_per_group):
    n, d = x.shape
    full = lambda shape: pl.BlockSpec(shape, lambda i: (0,) * len(shape))
    row = lambda w: pl.BlockSpec((ROW_TILE, w), lambda i: (i, 0))
    mod = lambda j: pl.BlockSpec((SUBLANES, d), lambda i: (0, j))
    return pl.pallas_call(
        functools.partial(_outproj_kernel, rows_per_group=rows_per_group),
        out_shape=(jax.ShapeDtypeStruct((n, d), F32), jax.ShapeDtypeStruct((n * (d // LANES), LANES), F32),
                   jax.ShapeDtypeStruct((n, LANES), F32)),
        grid=(n // ROW_TILE,),
        in_specs=[row(d), row(256), row(256), row(256), row(256), row(256), row(256), row(256),
                  full(ln.shape), full(b64.shape), full(wo.shape), mod(2), mod(3), mod(4),
                  full(g2.shape), full(rw_hi.shape), full(rw_lo.shape), full(rb.shape)],
        out_specs=(row(d), pl.BlockSpec((ROW_TILE * (d // LANES), LANES), lambda i: (i, 0)), row(LANES)),
        compiler_params=_cparams(("arbitrary",)),
        name="outproj",
    )(x, oa, ob, oc, y0, y1, bonus, gate, ln, b64, wo, mods, mods, mods, g2, rw_hi, rw_lo, rb)


TOKEN_TILES = 8


def _token_rows(token, count=1):
    start = token * TOKEN_TILES
    if not isinstance(token, int):
        start = pl.multiple_of(start, TOKEN_TILES)
    return pl.ds(start, count * TOKEN_TILES)


def _row_copy(src_hbm, token, dst, dst_row, sem):
    return pltpu.make_async_copy(src_hbm.at[_token_rows(token), :], dst.at[_token_rows(dst_row), :], sem)


def _issue_rows(n_rows, start_row):
    def body(j, carry):
        for u in range(DMA_UNROLL):
            start_row(j * DMA_UNROLL + u)
        return carry
    lax.fori_loop(0, n_rows // DMA_UNROLL, body, 0)


def _moe_kernel(be_ref, nused_ref, cur_ref, nxt_ref, h_hbm, w1_ref, w2_ref, b1_ref, b2_ref,
                out_ref, xbuf, sem):
    del be_ref
    i = pl.program_id(0)
    nused = nused_ref[0]
    slot = i % 2

    def issue(idx_ref, sl):
        _issue_rows(MOE_ROWS, lambda r: _row_copy(h_hbm, idx_ref[0, 0, r], xbuf.at[sl], r, sem.at[sl]).start())

    def wait_slot(sl):
        pltpu.make_async_copy(h_hbm.at[_token_rows(0, MOE_ROWS), :], xbuf.at[sl], sem.at[sl]).wait()

    @pl.when(i == 0)
    def _():
        issue(cur_ref, 0)

    @pl.when(i < nused)
    def _():
        wait_slot(slot)
        x = _load_token_tiles(xbuf, (slot,), MOE_ROWS, TOKEN_TILES).astype(BF16)
        for r in range(MOE_ROWS):
            _row_copy(h_hbm, nxt_ref[0, 0, r], xbuf.at[1 - slot], r, sem.at[1 - slot]).start()
        hcat = _mm(x, w1_ref[0]) + b1_ref[0]
        glu = jnp.minimum(hcat, SWIGLU_LIMIT)
        lin1 = jnp.clip(hcat, -SWIGLU_LIMIT, SWIGLU_LIMIT) + 1.0
        width = hcat.shape[1]
        act = glu * jax.nn.sigmoid(SWIGLU_ALPHA * glu) * pltpu.roll(lin1, width - 1, 1)
        _store_token_tiles(out_ref, _mm(act, w2_ref[0]) + b2_ref[0])

    @pl.when(i == nused - 1)
    def _():
        wait_slot(1 - slot)

    @pl.when(i >= nused)
    def _():
        out_ref[...] = jnp.zeros_like(out_ref)


def _moe_experts(h2, block_e, nused, buf_tok, w1, w2x, b1, b2):
    tiles = TOKEN_TILES
    d = tiles * LANES
    n_blocks = block_e.shape[0]
    dff2 = w1.shape[2]
    idx = buf_tok.reshape(n_blocks, 1, MOE_ROWS)
    wspec = lambda s1, s2: pl.BlockSpec((1, s1, s2), lambda i, be, nu: (be[i], 0, 0))
    grid_spec = pltpu.PrefetchScalarGridSpec(
        num_scalar_prefetch=2,
        grid=(n_blocks,),
        in_specs=[
            pl.BlockSpec((1, 1, MOE_ROWS), lambda i, be, nu: (i, 0, 0), memory_space=pltpu.SMEM),
            pl.BlockSpec((1, 1, MOE_ROWS), lambda i, be, nu: (jnp.minimum(i + 1, n_blocks - 1), 0, 0),
                         memory_space=pltpu.SMEM),
            pl.BlockSpec(memory_space=pl.ANY),
            wspec(d, dff2), wspec(dff2, d), wspec(1, dff2), wspec(1, d),
        ],
        out_specs=pl.BlockSpec((MOE_ROWS * tiles, LANES), lambda i, be, nu: (i, 0)),
        scratch_shapes=[pltpu.VMEM((2, MOE_ROWS * tiles, LANES), F32), pltpu.SemaphoreType.DMA((2,))],
    )
    return pl.pallas_call(
        _moe_kernel,
        out_shape=jax.ShapeDtypeStruct((n_blocks * MOE_ROWS * tiles, LANES), F32),
        grid_spec=grid_spec,
        compiler_params=_cparams(("arbitrary",)),
        name="moe_experts",
    )(block_e, nused, idx, idx, h2, w1, w2x, b1, b2)


def _comb_kernel(cur_ref, nxt_ref, yb_hbm, x_ref, gates_ref, gate_ref, out_ref, buf, sem,
                 *, rows_per_group, n_steps):
    i = pl.program_id(0)
    slot = i % 2
    g = (i * COMB_ROWS) // rows_per_group

    def issue(idx_ref, sl):
        def start_token(t):
            for kx in range(TOP_K):
                _row_copy(yb_hbm, idx_ref[0, 0, t * TOP_K + kx], buf.at[sl, kx], t, sem.at[sl]).start()
        _issue_rows(COMB_ROWS, start_token)

    @pl.when(i == 0)
    def _():
        issue(cur_ref, 0)

    @pl.when(i + 1 < n_steps)
    def _():
        issue(nxt_ref, 1 - slot)

    for kx in range(TOP_K):
        pltpu.make_async_copy(yb_hbm.at[_token_rows(0, COMB_ROWS), :], buf.at[slot, kx], sem.at[slot]).wait()
    gates = gates_ref[...]
    acc = gates[:, 0:1] * _load_token_tiles(buf, (slot, 0), COMB_ROWS, TOKEN_TILES)
    for kx in range(1, TOP_K):
        acc = acc + gates[:, kx:kx + 1] * _load_token_tiles(buf, (slot, kx), COMB_ROWS, TOKEN_TILES)
    out_ref[...] = x_ref[...] + gate_ref[pl.ds(g, 1), :] * acc


def _moe_combine(yb, x1, gates, slot_dest, mods, *, rows_per_group):
    n, d = x1.shape
    steps = n // COMB_ROWS
    idx = slot_dest.reshape(steps, 1, COMB_ROWS * TOP_K)
    return pl.pallas_call(
        functools.partial(_comb_kernel, rows_per_group=rows_per_group, n_steps=steps),
        out_shape=jax.ShapeDtypeStruct((n, d), F32),
        grid=(steps,),
        in_specs=[
            pl.BlockSpec((1, 1, COMB_ROWS * TOP_K), lambda i: (i, 0, 0), memory_space=pltpu.SMEM),
            pl.BlockSpec((1, 1, COMB_ROWS * TOP_K), lambda i: (jnp.minimum(i + 1, steps - 1), 0, 0),
                         memory_space=pltpu.SMEM),
            pl.BlockSpec(memory_space=pl.ANY),
            pl.BlockSpec((COMB_ROWS, d), lambda i: (i, 0)),
            pl.BlockSpec((COMB_ROWS, TOP_K), lambda i: (i, 0)),
            pl.BlockSpec((SUBLANES, d), lambda i: (0, 5)),
        ],
        out_specs=pl.BlockSpec((COMB_ROWS, d), lambda i: (i, 0)),
        scratch_shapes=[pltpu.VMEM((2, TOP_K, COMB_ROWS * TOKEN_TILES, LANES), F32),
                        pltpu.SemaphoreType.DMA((2,))],
        compiler_params=_cparams(("arbitrary",)),
        name="moe_combine",
    )(idx, idx, yb, x1, gates, mods)


def _route(logits):
    n = logits.shape[0]
    top_val, top_idx = lax.top_k(logits, TOP_K)
    gates = jax.nn.softmax(top_val, axis=-1)
    flat_e = top_idx.reshape(-1).astype(jnp.int32)
    nk = n * TOP_K
    order = jnp.argsort(flat_e).astype(jnp.int32)
    rank = jnp.argsort(order).astype(jnp.int32)
    experts = jnp.arange(N_EXPERTS, dtype=jnp.int32)
    counts = jnp.sum((flat_e[:, None] == experts[None, :]).astype(jnp.int32), axis=0)
    padded = (counts + MOE_ROWS - 1) // MOE_ROWS * MOE_ROWS
    start = jnp.cumsum(counts) - counts
    pad_end = jnp.cumsum(padded)
    pad_start = pad_end - padded
    n_blocks = (nk + N_EXPERTS * (MOE_ROWS - 1) + MOE_ROWS - 1) // MOE_ROWS
    cap = n_blocks * MOE_ROWS
    block_start = jnp.arange(n_blocks, dtype=jnp.int32) * MOE_ROWS
    block_e = jnp.minimum(jnp.sum((pad_end[None, :] <= block_start[:, None]).astype(jnp.int32), axis=1),
                          N_EXPERTS - 1)
    nused = (pad_end[-1] // MOE_ROWS).astype(jnp.int32).reshape(1)
    row = jnp.arange(cap, dtype=jnp.int32)
    row_e = jnp.repeat(block_e, MOE_ROWS)
    within = row - pad_start[row_e]
    valid = within < counts[row_e]
    src = jnp.clip(start[row_e] + within, 0, nk - 1)
    buf_tok = jnp.where(valid, order[src] // TOP_K, 0).astype(jnp.int32)
    slot_dest = (pad_start[flat_e] + rank - start[flat_e]).astype(jnp.int32)
    return gates, buf_tok, block_e, nused, slot_dest


def _block_ones(n, blk):
    idx = np.arange(n) // blk
    return jnp.asarray(idx[:, None] == idx[None, :], dtype=BF16)


def _pad_cols(w, width):
    return jnp.pad(w, ((0, 0), (0, width - w.shape[1])))


def _place_heads(w, per_head, offset, heads=MLA_HEADS, slot=MLA_PAD):
    rows = w.shape[0]
    w3 = w.reshape(rows, heads, per_head)
    out = jnp.zeros((rows, heads, slot), w.dtype).at[:, :, offset:offset + per_head].set(w3)
    return out.reshape(rows, heads * slot)


def _pack_in_weights(P, l):
    w_in = P["w_in"][l]
    sizes = (256, 128, 128, MLA_Q_LORA, MLA_KV_LORA, MLA_ROPE, 256, 256, 256, RWKV_COLS)
    offs = np.concatenate([[0], np.cumsum(sizes)])
    seg = [w_in[:, offs[j]:offs[j + 1]] for j in range(len(sizes))]
    krp = _place_heads(jnp.tile(seg[5], (1, MLA_HEADS)), MLA_ROPE, MLA_NOPE)
    w_all = jnp.concatenate([seg[0], seg[1], seg[2], _pad_cols(seg[3], 256), seg[4], krp,
                             seg[6], seg[7], seg[8], _pad_cols(seg[9], 1024)], axis=1).astype(BF16)
    q_up = _place_heads(P["mla_q_up"][l], MLA_QK, 0)
    q_up = jnp.pad(q_up, ((0, 256 - MLA_Q_LORA), (0, 0))).astype(BF16)
    kv_up = P["mla_kv_up"][l].reshape(MLA_KV_LORA, MLA_HEADS, MLA_NOPE + MLA_V)
    w_knope = _place_heads(kv_up[:, :, :MLA_NOPE].reshape(MLA_KV_LORA, -1), MLA_NOPE, 0).astype(BF16)
    w_v = kv_up[:, :, MLA_NOPE:].reshape(MLA_KV_LORA, MLA_HEADS * MLA_V).astype(BF16)
    width = MLA_HEADS * MLA_PAD

    def rowpad(v):
        return jnp.pad(v, (0, width - v.shape[0]))

    gains = jnp.stack([
        rowpad(jnp.tile(P["gqa_qn"][l], GQA_HEADS)),
        rowpad(jnp.tile(P["gqa_kn"][l], GQA_KV_HEADS)),
        rowpad(P["mla_qa_norm"][l]),
        rowpad(P["mla_kva_norm"][l]),
        _place_heads(jnp.tile(P["mla_qn"][l], MLA_HEADS)[None], MLA_QK, 0)[0],
        _place_heads(jnp.tile(P["mla_kn"][l], MLA_HEADS)[None], MLA_QK, 0)[0],
        rowpad(jnp.tile(P["diff_qn"][l], 2 * DIFF_HEADS)),
        rowpad(jnp.tile(P["diff_kn"][l], 2 * DIFF_HEADS)),
    ]).astype(F32)
    return dict(w_all=w_all, q_up=q_up, w_knope=w_knope, w_v=w_v, gains=gains,
                b64=_block_ones(256, 64), b32=_block_ones(256, 32), b128=_block_ones(width, MLA_PAD))


def _pack_rwkv(P, l):
    mix = _pad_cols(P["rwkv_mix"][l], 1024)
    w_lora = jnp.zeros((256, 1280), F32)
    w_lora = w_lora.at[0:32, 0:256].set(P["rwkv_w2"][l, 0]).at[32:64, 256:512].set(P["rwkv_w2"][l, 1])
    w_lora = w_lora.at[64:96, 512:768].set(P["rwkv_a2"][l, 0]).at[96:128, 768:1024].set(P["rwkv_a2"][l, 1])
    w_lora = w_lora.at[128:192, 1024:1280].set(P["rwkv_g2"][l]).astype(BF16)
    vec = jnp.zeros((SUBLANES, 512), F32)
    vec = vec.at[0].set(P["rwkv_w0"][l].reshape(-1)).at[1].set(P["rwkv_a0"][l].reshape(-1))
    vec = vec.at[2].set(jnp.concatenate([P["rwkv_kk"][l], P["rwkv_ka"][l]]))
    vec = vec.at[3].set(P["rwkv_rk"][l].reshape(-1))
    return dict(mix=mix, w_lora=w_lora, vec=vec, b64=_block_ones(256, 64))


def _rope_table(seq, rot, lanes_per_tile_group, lane_offset=0):
    rows = seq // GRID_W
    row = jnp.repeat(jnp.arange(rows, dtype=F32), GRID_W)
    col = (jnp.arange(rows * GRID_W) % GRID_W).astype(F32)
    n_freq = rot // 4
    inv_freq = ROPE_BASE ** (-jnp.arange(n_freq, dtype=F32) / n_freq)
    ang = jnp.stack([row[:, None] * inv_freq, col[:, None] * inv_freq], axis=1)
    cos = jnp.cos(ang)[:, :, None, :]
    sin = jnp.sin(ang)[:, :, None, :]
    cos_g = jnp.broadcast_to(cos, (seq, 2, 2, n_freq)).reshape(seq, rot)
    sin_g = (jnp.broadcast_to(sin, (seq, 2, 2, n_freq))
             * jnp.asarray([-1.0, 1.0], F32)[None, None, :, None]).reshape(seq, rot)
    cos_t = jnp.ones((seq, LANES), F32)
    sin_t = jnp.zeros((seq, LANES), F32)
    for start in range(lane_offset, LANES, lanes_per_tile_group):
        cos_t = cos_t.at[:, start:start + rot].set(cos_g)
        sin_t = sin_t.at[:, start:start + rot].set(sin_g)
    real = jnp.concatenate([cos_t, sin_t], axis=1)
    ident = jnp.concatenate([jnp.ones((seq, LANES), F32), jnp.zeros((seq, LANES), F32)], axis=1)
    return jnp.stack([real, ident])


def _to_heads(x, groups, seq, heads):
    d = x.shape[1] // heads
    return x.reshape(groups, seq, heads, d).transpose(0, 2, 1, 3).astype(BF16)


def _with_ones(v):
    return jnp.concatenate([v.astype(BF16), jnp.ones(v.shape, BF16)], axis=-1)


def _from_heads(o):
    g, h, s, d = o.shape
    return o.transpose(0, 2, 1, 3).reshape(g * s, h * d).astype(BF16)


def kernel(x_prompt, x_sample, cache_gqa_k, cache_gqa_v, cache_mla_ckv, cache_mla_krope, cache_diff_k, cache_diff_v, state_rwkv, c, c_ctx, norm1_g, norm2_g, ada_w, ada_b, w_in, w_out, gqa_qn, gqa_kn, mla_qa_norm, mla_q_up, mla_kva_norm, mla_kv_up, mla_qn, mla_kn, diff_qn, diff_kn, diff_lam, diff_subln, rwkv_mix, rwkv_w0, rwkv_w2, rwkv_a0, rwkv_a2, rwkv_rk, rwkv_g2, rwkv_kk, rwkv_ka, rwkv_ln_g, rwkv_ln_b, router_w, router_b, moe_w1, moe_b1, moe_w2, moe_b2):
    P = dict(w_in=w_in, gqa_qn=gqa_qn, gqa_kn=gqa_kn, mla_qa_norm=mla_qa_norm, mla_q_up=mla_q_up,
             mla_kva_norm=mla_kva_norm, mla_kv_up=mla_kv_up, mla_qn=mla_qn, mla_kn=mla_kn,
             diff_qn=diff_qn, diff_kn=diff_kn, rwkv_mix=rwkv_mix, rwkv_w0=rwkv_w0, rwkv_w2=rwkv_w2,
             rwkv_a0=rwkv_a0, rwkv_a2=rwkv_a2, rwkv_rk=rwkv_rk, rwkv_g2=rwkv_g2, rwkv_kk=rwkv_kk,
             rwkv_ka=rwkv_ka)
    depth = norm1_g.shape[0]
    bc, tc, d = x_prompt.shape
    bl, tl, _ = x_sample.shape
    assert bc * tc == tl, "context tokens must fill exactly one latent-sequence group"
    assert tl % ROW_TILE == 0 and tc % ROW_TILE == 0 and tc % SCAN_CHUNK == 0
    n_lat = bl * tl
    n = n_lat + tl
    groups = bl + 1
    past = cache_gqa_k.shape[3]

    cond = jnp.concatenate([c, c_ctx[None], jnp.zeros((SUBLANES - groups, d), F32)], axis=0)
    mods_all = _adaln(cond, ada_w, ada_b)
    tables = dict(a=_rope_table(tl, HEAD_DIM, HEAD_DIM),
                  m=_rope_table(tl, MLA_ROPE, LANES, lane_offset=MLA_NOPE),
                  c=_rope_table(tl, DIFF_QK, DIFF_QK))
    x = jnp.concatenate([x_sample.reshape(n_lat, d), x_prompt.reshape(tl, d)], axis=0)
    ctx_rows = slice(n_lat, n)
    new = [[] for _ in range(7)]

    for l in range(depth):
        lam_init = 0.8 - 0.6 * math.exp(-0.3 * l)
        mods = mods_all[l]
        wpack = _pack_in_weights(P, l)
        rpack = _pack_rwkv(P, l)
        (qa_h, ka_h, va_h, qb_h, kb_h, vb_h, qc_h, kc_h, vc_h, ka, va, ckvn, kr, kc, vc, pd) = _inproj(
            x, mods, norm1_g[l][None], wpack, tables, rows_per_group=tl, n_lat=n_lat)

        new[0].append(ka[ctx_rows].reshape(bc, tc, GQA_KV_HEADS, HEAD_DIM).transpose(0, 2, 1, 3))
        new[1].append(va[ctx_rows].reshape(bc, tc, GQA_KV_HEADS, HEAD_DIM).transpose(0, 2, 1, 3))
        new[2].append(ckvn[ctx_rows].reshape(bc, tc, MLA_KV_LORA))
        new[3].append(kr[ctx_rows, MLA_NOPE:MLA_NOPE + MLA_ROPE].reshape(bc, tc, MLA_ROPE))
        new[4].append(kc[ctx_rows].reshape(bc, tc, DIFF_HEADS, 2, DIFF_QK).transpose(0, 2, 3, 1, 4))
        new[5].append(vc[ctx_rows].reshape(bc, tc, DIFF_HEADS, DIFF_V).transpose(0, 2, 1, 3))

        o_a = _attention(qa_h, ka_h, va_h, cache_gqa_k[:, l].astype(BF16), _with_ones(cache_gqa_v[:, l]),
                         n_req=bl, seq=tc, ctx_group=bl)
        krp_c = _place_heads(jnp.tile(cache_mla_krope[:, l].reshape(bl * past, MLA_ROPE), (1, MLA_HEADS)),
                             MLA_ROPE, MLA_NOPE)
        kb_c, vb_c = _mla_cache(cache_mla_ckv[:, l].reshape(bl * past, MLA_KV_LORA), krp_c, wpack)
        o_b = _attention(qb_h, kb_h, vb_h,
                         _to_heads(kb_c, bl, past, MLA_HEADS), _with_ones(_to_heads(vb_c, bl, past, MLA_HEADS)),
                         n_req=bl, seq=tc, ctx_group=bl)
        kc_c = cache_diff_k[:, l].transpose(0, 1, 3, 2, 4).reshape(bl, DIFF_HEADS, past, 2 * DIFF_QK)
        o_c = _attention(qc_h, kc_h, vc_h, kc_c.astype(BF16), _with_ones(cache_diff_v[:, l]),
                         n_req=bl, seq=tc, ctx_group=bl,
                         diff_params=(diff_lam[l], diff_subln[l][None]), lam_init=lam_init)

        names = ("r", "v", "kk", "e0", "e1", "k0", "k1", "b0", "b1", "g", "bonus")
        rw = dict(zip(names, _rwkv_prep(pd, rpack, n_lat=n_lat, lat_seq=tl, ctx_seq=tc)))
        streams = {k_: rw[k_] for k_ in names[:9]}
        h0_lat = jnp.swapaxes(state_rwkv[:, l], -1, -2)
        y0l, y1l, _ = _rwkv_scan(streams, h0_lat, first_seq=0, seq_len=tl)
        h0_ctx = jnp.zeros((bc, 2, RWKV_HEADS, RWKV_N, RWKV_N), F32)
        y0c, y1c, h_ctx = _rwkv_scan(streams, h0_ctx, first_seq=n_lat // tc, seq_len=tc)
        new[6].append(jnp.swapaxes(h_ctx, -1, -2))
        y0 = jnp.concatenate([y0l.reshape(n_lat, RWKV_W), y0c.reshape(tl, RWKV_W)], axis=0)
        y1 = jnp.concatenate([y1l.reshape(n_lat, RWKV_W), y1c.reshape(tl, RWKV_W)], axis=0)

        ln = jnp.stack([rwkv_ln_g[l], rwkv_ln_b[l]])
        rw_f = _pad_cols(router_w[l], LANES)
        rw_hi = rw_f.astype(BF16)
        rw_lo = (rw_f - rw_hi.astype(F32)).astype(BF16)
        rb = _pad_cols(router_b[l][None], LANES)
        x1, h2, logits = _outproj(x, o_a, o_b, o_c, y0, y1,
                                  rw["bonus"], rw["g"], ln, rpack["b64"], w_out[l].astype(BF16), mods,
                                  norm2_g[l][None], rw_hi, rw_lo, rb, rows_per_group=tl)

        gates, buf_tok, block_e, nused, slot_dest = _route(logits[:, :N_EXPERTS])
        w2 = moe_w2[l].astype(BF16)
        w2x = jnp.stack([w2, jnp.zeros_like(w2)], axis=2).reshape(N_EXPERTS, 2 * w2.shape[1], d)
        yb = _moe_experts(h2, block_e, nused, buf_tok, moe_w1[l].astype(BF16), w2x,
                          moe_b1[l][:, None, :], moe_b2[l][:, None, :])
        x = _moe_combine(yb, x1, gates, slot_dest, mods, rows_per_group=tl)

    y_sample = x[:n_lat].reshape(bl, tl, d)
    y_prompt = x[n_lat:].reshape(bc, tc, d)
    return (y_prompt, y_sample) + tuple(jnp.stack(t, axis=1) for t in new)
```

```python
import functools
import math

import jax
import jax.numpy as jnp
import numpy as np
from jax import lax
from jax.experimental import pallas as pl
from jax.experimental.pallas import tpu as pltpu

F32 = jnp.float32
BF16 = jnp.bfloat16

GRID_W = 64
ROPE_BASE = 10000.0
EPS = 1e-6
HEAD_DIM = 64
GQA_HEADS, GQA_KV_HEADS = 4, 2
MLA_HEADS, MLA_Q_LORA, MLA_KV_LORA, MLA_NOPE, MLA_ROPE, MLA_V = 4, 192, 128, 64, 32, 64
MLA_QK = MLA_NOPE + MLA_ROPE
MLA_PAD = 128
DIFF_HEADS, DIFF_QK, DIFF_V = 4, 32, 64
RWKV_HEADS, RWKV_N = 4, 64
RWKV_W = RWKV_HEADS * RWKV_N
DECAY_LORA, AAA_LORA, GATE_LORA = 32, 32, 64
RWKV_GN_EPS = 64e-5
RWKV_COLS = 3 * RWKV_W + 2 * DECAY_LORA + 2 * AAA_LORA + GATE_LORA
N_EXPERTS, TOP_K = 32, 4
SWIGLU_ALPHA, SWIGLU_LIMIT = 1.702, 7.0

LANES = 128
SUBLANES = 8
VMEM_LIMIT = 48 * 1024 * 1024
MOE_VMEM_LIMIT = 56 * 1024 * 1024
ROW_TILE = 256
Q_TILE = 256
KV_CHUNK = 512
SCAN_CHUNK = 64
SCAN_SUB = 16
SCAN_SEQS = 4
MOE_ROWS = 256
MOE_SLOTS = 3
COMB_ROWS = 128
ADA_COLS = 1536
DMA_UNROLL = 8


def _cparams(sem, vmem=VMEM_LIMIT):
    return pltpu.CompilerParams(dimension_semantics=sem, vmem_limit_bytes=vmem)


def _mm(a, b):
    return jnp.dot(a.astype(BF16), b.astype(BF16), preferred_element_type=F32)


def _mm_nt(a, b):
    return lax.dot_general(a.astype(BF16), b.astype(BF16), (((1,), (1,)), ((), ())),
                           preferred_element_type=F32)


def _split2(x):
    hi = x.astype(BF16)
    lo = (x - hi.astype(F32)).astype(BF16)
    return hi, lo


def _split3(x):
    hi = x.astype(BF16)
    r = x - hi.astype(F32)
    mid = r.astype(BF16)
    lo = (r - mid.astype(F32)).astype(BF16)
    return hi, mid, lo


def _mm_x2(x, ones_blk):
    hi, lo = _split2(x)
    return (jnp.dot(hi, ones_blk, preferred_element_type=F32)
            + jnp.dot(lo, ones_blk, preferred_element_type=F32))


def _mm3(a, b):
    ah, al = _split2(a)
    bh, bl = _split2(b)
    return (jnp.dot(ah, bh, preferred_element_type=F32)
            + jnp.dot(ah, bl, preferred_element_type=F32)
            + jnp.dot(al, bh, preferred_element_type=F32))


def _rope(x, cos, sin_signed, rot):
    n = x.shape[-1]
    q = rot // 4
    lane = lax.broadcasted_iota(jnp.int32, x.shape, 1)
    first = (lane & (rot // 2 - 1)) < q
    partner = jnp.where(first, pltpu.roll(x, n - q, 1), pltpu.roll(x, q, 1))
    return x * cos + partner * sin_signed


def _tile_lanes(x, reps):
    return x if reps == 1 else jnp.concatenate([x] * reps, axis=1)


def _ada_kernel(c_ref, w_ref, b_ref, o_ref):
    c = c_ref[...]
    o_ref[0] = _mm(c * jax.nn.sigmoid(c), w_ref[0]) + b_ref[0]


def _adaln(cond, ada_w, ada_b):
    depth, d, cols = ada_w.shape
    rows = cond.shape[0]
    return pl.pallas_call(
        _ada_kernel,
        out_shape=jax.ShapeDtypeStruct((depth, rows, cols), F32),
        grid=(depth, cols // ADA_COLS),
        in_specs=[
            pl.BlockSpec((rows, d), lambda l, j: (0, 0)),
            pl.BlockSpec((1, d, ADA_COLS), lambda l, j: (l, 0, j)),
            pl.BlockSpec((1, 1, ADA_COLS), lambda l, j: (l, 0, j)),
        ],
        out_specs=pl.BlockSpec((1, rows, ADA_COLS), lambda l, j: (l, 0, j)),
        compiler_params=_cparams(("arbitrary", "arbitrary")),
        name="adaln",
    )(cond, ada_w, ada_b.reshape(depth, 1, cols))


_C_AQ, _C_AK, _C_AV = 0, 256, 384
_C_CQ, _C_CKV, _C_KRP = 512, 768, 896
_C_DQ, _C_DK, _C_DV = 1408, 1664, 1920
_C_PD = 2176
_C_END = 3200


def _mla_kv(ckvn, krp, w_knope, w_v, kn_gain, blk128, cos, sin):
    kpre = _mm(ckvn, w_knope) + krp
    ss = _mm_x2(kpre * kpre, blk128) * (1.0 / MLA_QK)
    kb = kpre * lax.rsqrt(ss + EPS) * kn_gain
    kb = _rope(kb, cos, sin, MLA_ROPE)
    return kb, _mm(ckvn, w_v)


def _store_heads(o_ref, x, width, ones=False):
    for h in range(o_ref.shape[1]):
        piece = x[:, width * h:width * (h + 1)]
        if ones:
            piece = jnp.concatenate([piece, jnp.ones_like(piece)], axis=1)
        o_ref[0, h] = piece.astype(o_ref.dtype)


def _inproj_kernel(x_ref, sh_ref, sc_ref, g1_ref, w_ref, qup_ref, wkn_ref, wv_ref, gains_ref,
                   b64_ref, b32_ref, b128_ref, ra_ref, rm_ref, rc_ref,
                   qa_h, ka_h, va_h, qb_h, kb_h, vb_h, qc_h, kc_h, vc_h,
                   ka_o, va_o, ckv_o, kr_o, kc_o, vc_o, pd_o,
                   *, rows_per_group):
    i = pl.program_id(0)
    g = (i * ROW_TILE) // rows_per_group
    x = x_ref[...]
    ms = jnp.mean(x * x, axis=-1, keepdims=True)
    xn = x * lax.rsqrt(ms + EPS) * g1_ref[...]
    h = xn * (1.0 + sc_ref[pl.ds(g, 1), :]) + sh_ref[pl.ds(g, 1), :]
    p = _mm(h, w_ref[...])
    gains = gains_ref[...]
    b64, b32, b128 = b64_ref[...], b32_ref[...], b128_ref[...]

    ra = ra_ref[0]
    cos_a, sin_a = ra[:, :LANES], ra[:, LANES:]
    aq = p[:, _C_AQ:_C_AK]
    ssq = _mm_x2(aq * aq, b64) * (1.0 / HEAD_DIM)
    qa = aq * lax.rsqrt(ssq + EPS) * gains[0:1, :256]
    qa = _rope(qa, _tile_lanes(cos_a, 2), _tile_lanes(sin_a, 2), HEAD_DIM)
    _store_heads(qa_h, qa * (HEAD_DIM ** -0.5), HEAD_DIM)
    ak = p[:, _C_AK:_C_AV]
    ssk = _mm_x2(ak * ak, b64[:LANES, :LANES]) * (1.0 / HEAD_DIM)
    ka = ak * lax.rsqrt(ssk + EPS) * gains[1:2, :128]
    ka = _rope(ka, cos_a, sin_a, HEAD_DIM)
    ka_o[...] = ka
    _store_heads(ka_h, ka, HEAD_DIM)
    va = p[:, _C_AV:_C_CQ]
    va_o[...] = va
    _store_heads(va_h, va, HEAD_DIM, ones=True)

    rm = rm_ref[0]
    cos_m, sin_m = _tile_lanes(rm[:, :LANES], MLA_HEADS), _tile_lanes(rm[:, LANES:], MLA_HEADS)
    cq = p[:, _C_CQ:_C_CKV]
    cqn = cq * lax.rsqrt(jnp.sum(cq * cq, axis=-1, keepdims=True) * (1.0 / MLA_Q_LORA) + EPS)
    qb = _mm(cqn * gains[2:3, :256], qup_ref[...])
    ssb = _mm_x2(qb * qb, b128) * (1.0 / MLA_QK)
    qb = qb * lax.rsqrt(ssb + EPS) * gains[4:5, :]
    _store_heads(qb_h, _rope(qb, cos_m, sin_m, MLA_ROPE) * (MLA_QK ** -0.5), MLA_PAD)
    ckv = p[:, _C_CKV:_C_KRP]
    ckvn = ckv * lax.rsqrt(jnp.mean(ckv * ckv, axis=-1, keepdims=True) + EPS) * gains[3:4, :128]
    ckv_o[...] = ckvn
    krp = p[:, _C_KRP:_C_DQ]
    kr_o[...] = krp[:, :LANES]
    kb, vb = _mla_kv(ckvn, krp, wkn_ref[...], wv_ref[...], gains[5:6, :], b128, cos_m, sin_m)
    _store_heads(kb_h, kb, MLA_PAD)
    _store_heads(vb_h, vb, MLA_V, ones=True)

    rc = rc_ref[0]
    cos_c, sin_c = _tile_lanes(rc[:, :LANES], 2), _tile_lanes(rc[:, LANES:], 2)
    dq = p[:, _C_DQ:_C_DK]
    ssd = _mm_x2(dq * dq, b32) * (1.0 / DIFF_QK)
    qc = dq * lax.rsqrt(ssd + EPS) * gains[6:7, :256]
    _store_heads(qc_h, _rope(qc, cos_c, sin_c, DIFF_QK) * (DIFF_QK ** -0.5), 2 * DIFF_QK)
    dk = p[:, _C_DK:_C_DV]
    ssd = _mm_x2(dk * dk, b32) * (1.0 / DIFF_QK)
    kc = dk * lax.rsqrt(ssd + EPS) * gains[7:8, :256]
    kc = _rope(kc, cos_c, sin_c, DIFF_QK)
    kc_o[...] = kc
    _store_heads(kc_h, kc, 2 * DIFF_QK)
    vc = p[:, _C_DV:_C_PD]
    vc_o[...] = vc
    _store_heads(vc_h, vc, DIFF_V, ones=True)

    pd_o[...] = p[:, _C_PD:_C_END]


def _inproj(x, mods, g1, wpack, tables, *, rows_per_group, n_lat):
    n, d = x.shape
    steps = n // ROW_TILE
    lat_steps = n_lat // ROW_TILE
    pos_steps = rows_per_group // ROW_TILE

    def rope_map(i):
        is_ctx = i >= lat_steps
        return (jnp.where(is_ctx, 1, 0), jnp.where(is_ctx, 0, i % pos_steps), 0)

    full = lambda shape: pl.BlockSpec(shape, lambda i: (0,) * len(shape))
    row = lambda w: pl.BlockSpec((ROW_TILE, w), lambda i: (i, 0))
    groups = n // rows_per_group
    head_shapes = ((GQA_HEADS, HEAD_DIM), (GQA_KV_HEADS, HEAD_DIM), (GQA_KV_HEADS, 2 * HEAD_DIM),
                   (MLA_HEADS, MLA_PAD), (MLA_HEADS, MLA_PAD), (MLA_HEADS, 2 * MLA_V),
                   (DIFF_HEADS, 2 * DIFF_QK), (DIFF_HEADS, 2 * DIFF_QK), (DIFF_HEADS, 2 * DIFF_V))
    heads = lambda hh, w: pl.BlockSpec((1, hh, ROW_TILE, w), lambda i: (i // pos_steps, 0, i % pos_steps, 0))
    widths = (128, 128, 128, 128, 256, 256, 1024)
    return pl.pallas_call(
        functools.partial(_inproj_kernel, rows_per_group=rows_per_group),
        out_shape=(tuple(jax.ShapeDtypeStruct((groups, hh, rows_per_group, w), BF16) for hh, w in head_shapes)
                   + tuple(jax.ShapeDtypeStruct((n, w), F32) for w in widths)),
        grid=(steps,),
        in_specs=[
            row(d),
            pl.BlockSpec((SUBLANES, d), lambda i: (0, 0)),
            pl.BlockSpec((SUBLANES, d), lambda i: (0, 1)),
            full((1, d)),
            full(wpack["w_all"].shape), full(wpack["q_up"].shape), full(wpack["w_knope"].shape),
            full(wpack["w_v"].shape), full(wpack["gains"].shape),
            full(wpack["b64"].shape), full(wpack["b32"].shape), full(wpack["b128"].shape),
            pl.BlockSpec((1, ROW_TILE, 2 * LANES), rope_map),
            pl.BlockSpec((1, ROW_TILE, 2 * LANES), rope_map),
            pl.BlockSpec((1, ROW_TILE, 2 * LANES), rope_map),
        ],
        out_specs=tuple(heads(hh, w) for hh, w in head_shapes) + tuple(row(w) for w in widths),
        compiler_params=_cparams(("arbitrary",)),
        name="inproj",
    )(x, mods, mods, g1, wpack["w_all"], wpack["q_up"], wpack["w_knope"], wpack["w_v"],
      wpack["gains"], wpack["b64"], wpack["b32"], wpack["b128"],
      tables["a"], tables["m"], tables["c"])


def _mla_cache_kernel(ckv_ref, krp_ref, wkn_ref, wv_ref, gains_ref, b128_ref, kb_o, vb_o):
    kn = gains_ref[...][5:6, :]
    one = jnp.ones((1, MLA_HEADS * MLA_PAD), F32)
    kb, vb = _mla_kv(ckv_ref[...], krp_ref[...], wkn_ref[...], wv_ref[...], kn, b128_ref[...],
                     one, jnp.zeros_like(one))
    kb_o[...] = kb
    vb_o[...] = vb


def _mla_cache(ckv, krp, wpack):
    rows = ckv.shape[0]
    tile = min(rows, ROW_TILE)
    full = lambda shape: pl.BlockSpec(shape, lambda i: (0,) * len(shape))
    row = lambda w: pl.BlockSpec((tile, w), lambda i: (i, 0))
    kw, vw = MLA_HEADS * MLA_PAD, MLA_HEADS * MLA_V
    return pl.pallas_call(
        _mla_cache_kernel,
        out_shape=(jax.ShapeDtypeStruct((rows, kw), F32), jax.ShapeDtypeStruct((rows, vw), F32)),
        grid=(rows // tile,),
        in_specs=[row(MLA_KV_LORA), row(kw), full(wpack["w_knope"].shape), full(wpack["w_v"].shape),
                  full(wpack["gains"].shape), full(wpack["b128"].shape)],
        out_specs=(row(kw), row(vw)),
        compiler_params=_cparams(("arbitrary",)),
        name="mla_cache",
    )(ckv, krp, wpack["w_knope"], wpack["w_v"], wpack["gains"], wpack["b128"])


def _softmax_streams(qs, kv_heads, segs, s_scr):
    pieces = []
    off = 0
    for k_ref, v_ref in segs:
        n_keys = k_ref.shape[2]
        step = min(KV_CHUNK, n_keys)
        for c0 in range(0, n_keys, step):
            pieces.append((k_ref, v_ref, c0, step, off + c0))
        off += n_keys
    row_max = []
    for j, q in enumerate(qs):
        m = None
        for k_ref, _, c0, step, col in pieces:
            s = _mm_nt(q, k_ref[0, kv_heads[j], c0:c0 + step, :])
            s_scr[j, :, col:col + step] = s
            for t0 in range(0, step, LANES):
                part = s[:, t0:t0 + LANES]
                m = part if m is None else jnp.maximum(m, part)
        row_max.append(jnp.max(m, axis=-1, keepdims=True))
    outs = []
    for j in range(len(qs)):
        acc = None
        for _, v_ref, c0, step, col in pieces:
            p = jnp.exp((s_scr[j, :, col:col + step] - row_max[j]).astype(BF16))
            pv = jnp.dot(p, v_ref[0, kv_heads[j], c0:c0 + step, :], preferred_element_type=F32)
            acc = pv if acc is None else acc + pv
        den = pltpu.roll(acc, HEAD_DIM, 1)
        outs.append((acc / den)[:, :HEAD_DIM])
    return outs


def _attn_kernel(*refs, has_cache, diff, lam_init, kv_heads):
    refs = list(refs)
    if diff:
        lam_ref, sub_ref = refs.pop(0), refs.pop(0)
    q_ref, k_ref, v_ref = refs[:3]
    o_ref, s_scr = refs[-2], refs[-1]
    segs = [(k_ref, v_ref)]
    if has_cache:
        segs.append((refs[3], refs[4]))
    if not diff:
        outs = _softmax_streams([q_ref[0, j] for j in range(q_ref.shape[1])], kv_heads, segs, s_scr)
        o_ref[...] = jnp.concatenate(outs, axis=1).astype(o_ref.dtype)
        return
    qs = []
    for j in range(q_ref.shape[1]):
        q = q_ref[0, j]
        lane = lax.broadcasted_iota(jnp.int32, q.shape, 1)
        zero = jnp.zeros_like(q)
        qs += [jnp.where(lane < DIFF_QK, q, zero), jnp.where(lane >= DIFF_QK, q, zero)]
    outs = _softmax_streams(qs, kv_heads, segs, s_scr)
    lv = lam_ref[...]
    lam = (jnp.exp(jnp.sum(lv[0:1] * lv[1:2], axis=-1, keepdims=True))
           - jnp.exp(jnp.sum(lv[2:3] * lv[3:4], axis=-1, keepdims=True)) + lam_init)
    heads = []
    for j in range(q_ref.shape[1]):
        o = outs[2 * j] - lam * outs[2 * j + 1]
        o = o * lax.rsqrt(jnp.mean(o * o, axis=-1, keepdims=True) + EPS) * sub_ref[...]
        heads.append(o * (1.0 - lam_init))
    o_ref[...] = jnp.concatenate(heads, axis=1).astype(o_ref.dtype)


def _attention(q, k, v, kc, vc, *, n_req, seq, ctx_group, diff_params=None, lam_init=0.0):
    g_all, hq, s_all, dq = q.shape
    hk, dk, dvp = k.shape[1], k.shape[3], v.shape[3]
    diff = diff_params is not None
    qh = 2
    if diff:
        kh, kv_heads = 2, (0, 0, 1, 1)
    elif hq == 2 * hk:
        kh, kv_heads = 1, (0, 0)
    else:
        kh, kv_heads = 2, (0, 1)
    n_streams = len(kv_heads)
    steps_h = hq // qh
    out_w = qh * HEAD_DIM
    q_tiles = s_all // Q_TILE
    extra_in, extra_specs2, extra_specs3 = [], [], []
    if diff:
        extra_in = [diff_params[0], diff_params[1]]
        extra_specs3 = [pl.BlockSpec(diff_params[0].shape, lambda b, h, i: (0, 0)),
                        pl.BlockSpec(diff_params[1].shape, lambda b, h, i: (0, 0))]
        extra_specs2 = [pl.BlockSpec(diff_params[0].shape, lambda b, h: (0, 0)),
                        pl.BlockSpec(diff_params[1].shape, lambda b, h: (0, 0))]
    past = kc.shape[2]
    body = functools.partial(_attn_kernel, diff=diff, lam_init=lam_init, kv_heads=kv_heads)
    lat = pl.pallas_call(
        functools.partial(body, has_cache=True),
        out_shape=jax.ShapeDtypeStruct((n_req * s_all, hq * HEAD_DIM), BF16),
        grid=(n_req, steps_h, q_tiles),
        in_specs=extra_specs3 + [
            pl.BlockSpec((1, qh, Q_TILE, dq), lambda b, h, i: (b, h, i, 0)),
            pl.BlockSpec((1, kh, s_all, dk), lambda b, h, i: (b, h, 0, 0)),
            pl.BlockSpec((1, kh, s_all, dvp), lambda b, h, i: (b, h, 0, 0)),
            pl.BlockSpec((1, kh, past, dk), lambda b, h, i: (b, h, 0, 0)),
            pl.BlockSpec((1, kh, past, dvp), lambda b, h, i: (b, h, 0, 0)),
        ],
        out_specs=pl.BlockSpec((Q_TILE, out_w), lambda b, h, i: (b * q_tiles + i, h)),
        scratch_shapes=[pltpu.VMEM((n_streams, Q_TILE, s_all + past), F32)],
        compiler_params=_cparams(("arbitrary", "arbitrary", "arbitrary")),
        name="attn_latent",
    )(*extra_in, q, k, v, kc, vc)
    n_seq = s_all // seq
    ctx = pl.pallas_call(
        functools.partial(body, has_cache=False),
        out_shape=jax.ShapeDtypeStruct((s_all, hq * HEAD_DIM), BF16),
        grid=(n_seq, steps_h),
        in_specs=extra_specs2 + [
            pl.BlockSpec((1, qh, seq, dq), lambda s, h: (ctx_group, h, s, 0)),
            pl.BlockSpec((1, kh, seq, dk), lambda s, h: (ctx_group, h, s, 0)),
            pl.BlockSpec((1, kh, seq, dvp), lambda s, h: (ctx_group, h, s, 0)),
        ],
        out_specs=pl.BlockSpec((seq, out_w), lambda s, h: (s, h)),
        scratch_shapes=[pltpu.VMEM((n_streams, seq, seq), F32)],
        compiler_params=_cparams(("arbitrary", "arbitrary")),
        name="attn_context",
    )(*extra_in, q, k, v)
    return jnp.concatenate([lat, ctx], axis=0)


def _rprep_kernel(p_ref, pp_ref, pn_ref, mix_ref, wl_ref, vec_ref, b64_ref,
                  r_o, v_o, kk_o, e0_o, e1_o, k0_o, k1_o, b0_o, b1_o, g_o, bon_o,
                  *, lat_steps, lat_seq_steps, ctx_seq_steps):
    i = pl.program_id(0)
    is_lat = i < lat_steps
    pos = jnp.where(is_lat, i % lat_seq_steps, (i - lat_steps) % ctx_seq_steps)
    last = jnp.where(is_lat, lat_seq_steps - 1, ctx_seq_steps - 1)
    at_start = pos == 0
    at_end = pos == last
    p = p_ref[...]
    rows = lax.broadcasted_iota(jnp.int32, p.shape, 0)
    prev_edge = jnp.where(at_start, 0.0, pp_ref[SUBLANES - 1:SUBLANES, :])
    next_edge = jnp.where(at_end, 0.0, pn_ref[0:1, :])
    prev = jnp.where(rows == 0, prev_edge, pltpu.roll(p, 1, 0))
    nxt = jnp.where(rows == ROW_TILE - 1, next_edge, pltpu.roll(p, ROW_TILE - 1, 0))
    mix = mix_ref[...]
    xs = p + (prev - p) * mix[0:1] + (nxt - p) * mix[1:2]
    r, k, v = xs[:, 0:256], xs[:, 256:512], xs[:, 512:768]
    lo = xs[:, 768:1024]
    lane = lax.broadcasted_iota(jnp.int32, lo.shape, 1)
    act = jnp.where(lane < 2 * DECAY_LORA, jnp.tanh(lo),
                    jnp.where(lane < 2 * (DECAY_LORA + AAA_LORA), lo, jax.nn.sigmoid(lo)))
    lora = _mm(act, wl_ref[...])
    vec = vec_ref[...]
    z = -(vec[0:1, :] + lora[:, 0:512])
    softplus = jnp.maximum(z, 0.0) + jnp.log1p(jnp.exp(-jnp.abs(z)))
    e = jnp.exp(-softplus - 0.5)
    a = jax.nn.sigmoid(vec[1:2, :] + lora[:, 512:1024])
    g_o[...] = lora[:, 1024:1280]
    b64 = b64_ref[...]
    kk = k * vec[2:3, 0:256]
    kk = kk * lax.rsqrt(_mm_x2(kk * kk, b64) + 1e-12)
    ka = vec[2:3, 256:512]
    r_o[...] = r
    v_o[...] = v
    kk_o[...] = kk
    bonus = None
    for d, (e_o, k_o, b_o) in enumerate(((e0_o, k0_o, b0_o), (e1_o, k1_o, b1_o))):
        a_d = a[:, 256 * d:256 * (d + 1)]
        k_d = k * (1.0 + (a_d - 1.0) * ka)
        e_o[...] = e[:, 256 * d:256 * (d + 1)]
        k_o[...] = k_d
        b_o[...] = kk * a_d
        term = _mm_x2(r * k_d * vec[3:4, 256 * d:256 * (d + 1)], b64) * v
        bonus = term if bonus is None else bonus + term
    bon_o[...] = bonus


def _rwkv_prep(pd, rpack, *, n_lat, lat_seq, ctx_seq):
    n, w = pd.shape
    steps = n // ROW_TILE
    halo_blocks = n // SUBLANES
    per = ROW_TILE // SUBLANES
    full = lambda shape: pl.BlockSpec(shape, lambda i: (0,) * len(shape))
    row = lambda width: pl.BlockSpec((ROW_TILE, width), lambda i: (i, 0))
    return pl.pallas_call(
        functools.partial(_rprep_kernel, lat_steps=n_lat // ROW_TILE,
                          lat_seq_steps=lat_seq // ROW_TILE, ctx_seq_steps=ctx_seq // ROW_TILE),
        out_shape=tuple(jax.ShapeDtypeStruct((n, RWKV_W), F32) for _ in range(11)),
        grid=(steps,),
        in_specs=[
            row(w),
            pl.BlockSpec((SUBLANES, w), lambda i: (jnp.maximum(i * per - 1, 0), 0)),
            pl.BlockSpec((SUBLANES, w), lambda i: (jnp.minimum((i + 1) * per, halo_blocks - 1), 0)),
            full(rpack["mix"].shape), full(rpack["w_lora"].shape), full(rpack["vec"].shape),
            full(rpack["b64"].shape),
        ],
        out_specs=tuple(row(RWKV_W) for _ in range(11)),
        compiler_params=_cparams(("arbitrary",)),
        name="rwkv_prep",
    )(pd, pd, pd, rpack["mix"], rpack["w_lora"], rpack["vec"], rpack["b64"])


def _bmm(a, b):
    return lax.dot_general(a.astype(BF16), b.astype(BF16), (((2,), (1,)), ((0,), (0,))),
                           preferred_element_type=F32)


def _bmm_nt(a, b):
    return lax.dot_general(a.astype(BF16), b.astype(BF16), (((2,), (2,)), ((0,), (0,))),
                           preferred_element_type=F32)


def _bmm3(a, b):
    ah, al = _split2(a)
    bh, bl = _split2(b)
    return _bmm(ah, bh) + _bmm(ah, bl) + _bmm(al, bh)


def _btranspose(x):
    return jnp.stack([x[i].T for i in range(x.shape[0])])


def _scan_chunks(r, v, kk, e, k, b, h0, rev):
    n, c, _ = r.shape
    ti = lax.broadcasted_iota(jnp.int32, (n, c, c), 1)
    si = lax.broadcasted_iota(jnp.int32, (n, c, c), 2)
    order = (si - ti) * jnp.where(rev, -1, 1)
    incl = order <= 0
    strict = order < 0
    e_hi, e_mid, e_lo = _split3(e)
    incl_b = jnp.where(incl, 1.0, 0.0).astype(BF16)
    cs = _bmm(incl_b, e_hi) + _bmm(incl_b, e_mid) + _bmm(incl_b, e_lo)
    g_prev = jnp.exp(e - cs)
    g_incl = jnp.exp(-cs)
    g_inv = jnp.exp(cs)
    kkg, rg, bq, kq = kk * g_prev, r * g_incl, b * g_inv, k * g_inv
    left = jnp.concatenate([kkg, rg], axis=1)
    sb = _bmm_nt(left, bq)
    sk = _bmm_nt(left, kq)
    zero = jnp.zeros((n, c, c), F32)
    a_m = jnp.where(strict, sb[:, :c], zero)
    b_m = jnp.where(strict, sk[:, :c], zero)
    a_p = jnp.where(incl, sb[:, c:], zero)
    b_p = jnp.where(incl, sk[:, c:], zero)

    eye = (ti == si).astype(F32)
    same = (ti // SCAN_SUB) == (si // SCAN_SUB)
    a_d = jnp.where(same, a_m, zero)
    a_off = a_m - a_d
    t_d = eye - a_d
    pw = _bmm3(a_d, a_d)
    steps = int(math.log2(SCAN_SUB))
    for j in range(1, steps):
        t_d = t_d + _bmm3(t_d, pw)
        if j + 1 < steps:
            pw = _bmm3(pw, pw)
    nil = _bmm3(t_d, a_off)
    inm = eye - nil
    nblk = c // SCAN_SUB
    if nblk > 2:
        nil2 = _bmm3(nil, nil)
        inm = inm + _bmm3(inm, nil2)
    t_full = _bmm3(inm, t_d) if nblk > 1 else t_d

    bmv = _bmm(b_m, v)
    w1 = _bmm(t_full, kkg)
    z = _bmm(t_full, bmv)
    cs_t = _btranspose(cs)
    tot_col = jnp.maximum(cs_t[:, :, 0:1], cs_t[:, :, c - 1:c])
    g_end_t = jnp.exp(cs_t - tot_col)
    kt_t = _btranspose(k) * g_end_t
    bt_t = _btranspose(b) * g_end_t
    u = _bmm(w1, h0) + z
    h_new = jnp.exp(-tot_col) * h0 + _bmm(kt_t, v) - _bmm(bt_t, u)
    y = _bmm(rg, h0) - _bmm(a_p, u) + _bmm(b_p, v)
    return y, h_new


def _scan_kernel(rf, vf, kkf, ef, kf, bf, rb, vb, kkb, eb, kb, bb, h0_ref,
                 y0_ref, y1_ref, hout_ref, h_scr, *, n_seq, n_chunks):
    c = pl.program_id(1)

    @pl.when(c == 0)
    def _():
        h_scr[...] = h0_ref[...]

    groups = ((rf, vf, kkf, ef, kf, bf), (rb, vb, kkb, eb, kb, bb))
    loaded = [[ref[...] for ref in refs] for refs in groups]
    ops = []
    for j in range(6):
        ops.append(jnp.stack([loaded[d][j][s, :, RWKV_N * h:RWKV_N * (h + 1)]
                              for s in range(n_seq) for d in range(2) for h in range(RWKV_HEADS)]))
    n = n_seq * 2 * RWKV_HEADS
    chain = lax.broadcasted_iota(jnp.int32, (n, 1, 1), 0)
    rev = ((chain // RWKV_HEADS) % 2) == 1
    h0 = h_scr[...].reshape(n, RWKV_N, RWKV_N)
    y, h_new = _scan_chunks(*ops, h0, rev)
    h_scr[...] = h_new.reshape(h_scr.shape)
    for s in range(n_seq):
        for d, y_ in enumerate((y0_ref, y1_ref)):
            base = (s * 2 + d) * RWKV_HEADS
            y_[s] = jnp.concatenate([y[base + h] for h in range(RWKV_HEADS)], axis=1)

    @pl.when(c == n_chunks - 1)
    def _():
        hout_ref[...] = h_scr[...]


def _rwkv_scan(streams, h0, *, first_seq, seq_len):
    n_chunks = seq_len // SCAN_CHUNK
    total = h0.shape[0]
    sb = min(SCAN_SEQS, total)
    assert total % sb == 0 and first_seq % sb == 0
    arrs = {name: a.reshape(-1, seq_len, RWKV_W) for name, a in streams.items()}
    blk0 = first_seq // sb
    blk = (sb, SCAN_CHUNK, RWKV_W)
    fwd = pl.BlockSpec(blk, lambda s, c: (blk0 + s, c, 0))
    bwd = pl.BlockSpec(blk, lambda s, c: (blk0 + s, n_chunks - 1 - c, 0))
    st_shape = (sb,) + h0.shape[1:]
    st = pl.BlockSpec(st_shape, lambda s, c: (s, 0, 0, 0, 0))
    y_shape = jax.ShapeDtypeStruct((total, seq_len, RWKV_W), F32)
    y0, y1, h_fin = pl.pallas_call(
        functools.partial(_scan_kernel, n_seq=sb, n_chunks=n_chunks),
        out_shape=(y_shape, y_shape, jax.ShapeDtypeStruct(h0.shape, F32)),
        grid=(total // sb, n_chunks),
        in_specs=[fwd] * 6 + [bwd] * 6 + [st],
        out_specs=(pl.BlockSpec(blk, lambda s, c: (s, c, 0)),
                   pl.BlockSpec(blk, lambda s, c: (s, n_chunks - 1 - c, 0)),
                   st),
        scratch_shapes=[pltpu.VMEM(st_shape, F32)],
        compiler_params=_cparams(("arbitrary", "arbitrary")),
        name="rwkv_scan",
    )(arrs["r"], arrs["v"], arrs["kk"], arrs["e0"], arrs["k0"], arrs["b0"],
      arrs["r"], arrs["v"], arrs["kk"], arrs["e1"], arrs["k1"], arrs["b1"], h0)
    return y0, y1, h_fin


def _store_token_tiles(ref, x):
    rows, width = x.shape
    tiles = width // LANES
    for j in range(tiles):
        ref[pl.ds(j, rows, stride=tiles), :] = x[:, LANES * j:LANES * (j + 1)]


def _load_token_tiles(ref, lead, rows, tiles):
    idx = tuple(lead)
    return jnp.concatenate([ref[idx + (pl.ds(j, rows, stride=tiles), slice(None))] for j in range(tiles)],
                           axis=1)


def _outproj_kernel(x_ref, oa_ref, ob_ref, oc_ref, y0_ref, y1_ref, bon_ref, g_ref, ln_ref, b64_ref,
                    wo_ref, gate_ref, sh_ref, sc_ref, g2_ref, rwh_ref, rwl_ref, rb_ref,
                    x1_o, h2_o, lg_o, *, rows_per_group):
    i = pl.program_id(0)
    g = (i * ROW_TILE) // rows_per_group
    b64 = b64_ref[...]
    y = y0_ref[...] + y1_ref[...]
    mu = _mm_x2(y, b64) * (1.0 / RWKV_N)
    dy = y - mu
    var = _mm_x2(dy * dy, b64) * (1.0 / RWKV_N)
    ln = ln_ref[...]
    yn = dy * lax.rsqrt(var + RWKV_GN_EPS) * ln[0:1] + ln[1:2]
    od = (yn + bon_ref[...]) * g_ref[...]
    wo = wo_ref[...]
    mixed = (_mm(oa_ref[...], wo[0:256]) + _mm(ob_ref[...], wo[256:512])
             + _mm(oc_ref[...], wo[512:768]) + _mm(od, wo[768:1024]))
    x1 = x_ref[...] + gate_ref[pl.ds(g, 1), :] * mixed
    x1_o[...] = x1
    ms = jnp.mean(x1 * x1, axis=-1, keepdims=True)
    h2 = (x1 * lax.rsqrt(ms + EPS) * g2_ref[...]) * (1.0 + sc_ref[pl.ds(g, 1), :]) + sh_ref[pl.ds(g, 1), :]
    _store_token_tiles(h2_o, h2)
    hh, hl = _split2(h2)
    rwh, rwl = rwh_ref[...], rwl_ref[...]
    lg_o[...] = (jnp.dot(hh, rwh, preferred_element_type=F32) + jnp.dot(hh, rwl, preferred_element_type=F32)
                 + jnp.dot(hl, rwh, preferred_element_type=F32)) + rb_ref[...]


def _outproj(x, oa, ob, oc, y0, y1, bonus, gate, ln, b64, wo, mods, g2, rw_hi, rw_lo, rb,
             *, rows_per_group):
    n, d = x.shape
    full = lambda shape: pl.BlockSpec(shape, lambda i: (0,) * len(shape))
    row = lambda w: pl.BlockSpec((ROW_TILE, w), lambda i: (i, 0))
    mod = lambda j: pl.BlockSpec((SUBLANES, d), lambda i: (0, j))
    return pl.pallas_call(
        functools.partial(_outproj_kernel, rows_per_group=rows_per_group),
        out_shape=(jax.ShapeDtypeStruct((n, d), F32), jax.ShapeDtypeStruct((n * (d // LANES), LANES), F32),
                   jax.ShapeDtypeStruct((n, LANES), F32)),
        grid=(n // ROW_TILE,),
        in_specs=[row(d), row(256), row(256), row(256), row(256), row(256), row(256), row(256),
                  full(ln.shape), full(b64.shape), full(wo.shape), mod(2), mod(3), mod(4),
                  full(g2.shape), full(rw_hi.shape), full(rw_lo.shape), full(rb.shape)],
        out_specs=(row(d), pl.BlockSpec((ROW_TILE * (d // LANES), LANES), lambda i: (i, 0)), row(LANES)),
        compiler_params=_cparams(("arbitrary",)),
        name="outproj",
    )(x, oa, ob, oc, y0, y1, bonus, gate, ln, b64, wo, mods, mods, mods, g2, rw_hi, rw_lo, rb)


TOKEN_TILES = 8


def _token_rows(token, count=1):
    start = token * TOKEN_TILES
    if not isinstance(token, int):
        start = pl.multiple_of(start, TOKEN_TILES)
    return pl.ds(start, count * TOKEN_TILES)


def _row_copy(src_hbm, token, dst, dst_row, sem):
    return pltpu.make_async_copy(src_hbm.at[_token_rows(token), :], dst.at[_token_rows(dst_row), :], sem)


def _issue_rows(n_rows, start_row):
    def body(j, carry):
        for u in range(DMA_UNROLL):
            start_row(j * DMA_UNROLL + u)
        return carry
    lax.fori_loop(0, n_rows // DMA_UNROLL, body, 0)


def _moe_kernel(be_ref, nused_ref, cur_ref, nxt_ref, nxt2_ref, h_hbm, w1_ref, w2_ref, b1_ref, b2_ref,
                out_ref, xbuf, w1b, w2xb, sem):
    i = pl.program_id(0)
    nused = nused_ref[0]
    slot = i % MOE_SLOTS
    ahead = (i + MOE_SLOTS - 1) % MOE_SLOTS

    def issue(idx_ref, sl):
        _issue_rows(MOE_ROWS, lambda r: _row_copy(h_hbm, idx_ref[0, 0, r], xbuf.at[sl], r, sem.at[sl]).start())

    def wait_slot(sl):
        pltpu.make_async_copy(h_hbm.at[_token_rows(0, MOE_ROWS), :], xbuf.at[sl], sem.at[sl]).wait()

    @pl.when(i == 0)
    def _():
        issue(cur_ref, 0)
        issue(nxt_ref, 1)

    @pl.when(i < nused)
    def _():
        wait_slot(slot)

        @pl.when(jnp.logical_or(i == 0, be_ref[i] != be_ref[jnp.maximum(i - 1, 0)]))
        def _():
            w1b[...] = w1_ref[0].astype(BF16)
            bits = pltpu.bitcast(w2_ref[0].astype(BF16).astype(F32), jnp.uint32)
            w2xb[...] = pltpu.bitcast(lax.shift_right_logical(bits, jnp.uint32(16)), BF16)

        x = _load_token_tiles(xbuf, (slot,), MOE_ROWS, TOKEN_TILES).astype(BF16)
        for r in range(MOE_ROWS):
            _row_copy(h_hbm, nxt2_ref[0, 0, r], xbuf.at[ahead], r, sem.at[ahead]).start()
        hcat = jnp.dot(x, w1b[...], preferred_element_type=F32) + b1_ref[0]
        glu = jnp.minimum(hcat, SWIGLU_LIMIT)
        lin1 = jnp.clip(hcat, -SWIGLU_LIMIT, SWIGLU_LIMIT) + 1.0
        width = hcat.shape[1]
        act = glu * jax.nn.sigmoid(SWIGLU_ALPHA * glu) * pltpu.roll(lin1, width - 1, 1)
        _store_token_tiles(out_ref, _mm(act, w2xb[...]) + b2_ref[0])

    @pl.when(i == nused - 1)
    def _():
        wait_slot((i + 1) % MOE_SLOTS)
        wait_slot(ahead)

    @pl.when(i >= nused)
    def _():
        out_ref[...] = jnp.zeros_like(out_ref)


def _moe_experts(h2, block_e, nused, buf_tok, w1, w2, b1, b2, layer):
    tiles = TOKEN_TILES
    d = tiles * LANES
    n_blocks = block_e.shape[0]
    dff2 = w1.shape[3]
    idx = buf_tok.reshape(n_blocks, 1, MOE_ROWS)
    wspec = lambda s1, s2: pl.BlockSpec((None, 1, s1, s2), lambda i, be, nu: (layer, be[i], 0, 0))
    grid_spec = pltpu.PrefetchScalarGridSpec(
        num_scalar_prefetch=2,
        grid=(n_blocks,),
        in_specs=[
            pl.BlockSpec((1, 1, MOE_ROWS), lambda i, be, nu: (i, 0, 0), memory_space=pltpu.SMEM),
            pl.BlockSpec((1, 1, MOE_ROWS), lambda i, be, nu: (jnp.minimum(i + 1, n_blocks - 1), 0, 0),
                         memory_space=pltpu.SMEM),
            pl.BlockSpec((1, 1, MOE_ROWS), lambda i, be, nu: (jnp.minimum(i + 2, n_blocks - 1), 0, 0),
                         memory_space=pltpu.SMEM),
            pl.BlockSpec(memory_space=pl.ANY),
            wspec(d, dff2), wspec(dff2 // 2, d), wspec(1, dff2), wspec(1, d),
        ],
        out_specs=pl.BlockSpec((MOE_ROWS * tiles, LANES), lambda i, be, nu: (i, 0)),
        scratch_shapes=[pltpu.VMEM((MOE_SLOTS, MOE_ROWS * tiles, LANES), F32), pltpu.VMEM((d, dff2), BF16),
                        pltpu.VMEM((dff2, d), BF16), pltpu.SemaphoreType.DMA((MOE_SLOTS,))],
    )
    return pl.pallas_call(
        _moe_kernel,
        out_shape=jax.ShapeDtypeStruct((n_blocks * MOE_ROWS * tiles, LANES), F32),
        grid_spec=grid_spec,
        compiler_params=_cparams(("arbitrary",), vmem=MOE_VMEM_LIMIT),
        name="moe_experts",
    )(block_e, nused, idx, idx, idx, h2, w1, w2, b1, b2)


def _comb_kernel(cur_ref, nxt_ref, yb_hbm, x_ref, gates_ref, gate_ref, out_ref, buf, sem,
                 *, rows_per_group, n_steps):
    i = pl.program_id(0)
    slot = i % 2
    g = (i * COMB_ROWS) // rows_per_group

    def issue(idx_ref, sl):
        def start_token(t):
            for kx in range(TOP_K):
                _row_copy(yb_hbm, idx_ref[0, 0, t * TOP_K + kx], buf.at[sl, kx], t, sem.at[sl]).start()
        _issue_rows(COMB_ROWS, start_token)

    @pl.when(i == 0)
    def _():
        issue(cur_ref, 0)

    @pl.when(i + 1 < n_steps)
    def _():
        issue(nxt_ref, 1 - slot)

    for kx in range(TOP_K):
        pltpu.make_async_copy(yb_hbm.at[_token_rows(0, COMB_ROWS), :], buf.at[slot, kx], sem.at[slot]).wait()
    gates = gates_ref[...]
    acc = gates[:, 0:1] * _load_token_tiles(buf, (slot, 0), COMB_ROWS, TOKEN_TILES)
    for kx in range(1, TOP_K):
        acc = acc + gates[:, kx:kx + 1] * _load_token_tiles(buf, (slot, kx), COMB_ROWS, TOKEN_TILES)
    out_ref[...] = x_ref[...] + gate_ref[pl.ds(g, 1), :] * acc


def _moe_combine(yb, x1, gates, slot_dest, mods, *, rows_per_group):
    n, d = x1.shape
    steps = n // COMB_ROWS
    idx = slot_dest.reshape(steps, 1, COMB_ROWS * TOP_K)
    return pl.pallas_call(
        functools.partial(_comb_kernel, rows_per_group=rows_per_group, n_steps=steps),
        out_shape=jax.ShapeDtypeStruct((n, d), F32),
        grid=(steps,),
        in_specs=[
            pl.BlockSpec((1, 1, COMB_ROWS * TOP_K), lambda i: (i, 0, 0), memory_space=pltpu.SMEM),
            pl.BlockSpec((1, 1, COMB_ROWS * TOP_K), lambda i: (jnp.minimum(i + 1, steps - 1), 0, 0),
                         memory_space=pltpu.SMEM),
            pl.BlockSpec(memory_space=pl.ANY),
            pl.BlockSpec((COMB_ROWS, d), lambda i: (i, 0)),
            pl.BlockSpec((COMB_ROWS, TOP_K), lambda i: (i, 0)),
            pl.BlockSpec((SUBLANES, d), lambda i: (0, 5)),
        ],
        out_specs=pl.BlockSpec((COMB_ROWS, d), lambda i: (i, 0)),
        scratch_shapes=[pltpu.VMEM((2, TOP_K, COMB_ROWS * TOKEN_TILES, LANES), F32),
                        pltpu.SemaphoreType.DMA((2,))],
        compiler_params=_cparams(("arbitrary",)),
        name="moe_combine",
    )(idx, idx, yb, x1, gates, mods)


def _route(logits):
    n = logits.shape[0]
    top_val, top_idx = lax.top_k(logits, TOP_K)
    gates = jax.nn.softmax(top_val, axis=-1)
    flat_e = top_idx.reshape(-1).astype(jnp.int32)
    nk = n * TOP_K
    order = jnp.argsort(flat_e).astype(jnp.int32)
    rank = jnp.argsort(order).astype(jnp.int32)
    experts = jnp.arange(N_EXPERTS, dtype=jnp.int32)
    counts = jnp.sum((flat_e[:, None] == experts[None, :]).astype(jnp.int32), axis=0)
    padded = (counts + MOE_ROWS - 1) // MOE_ROWS * MOE_ROWS
    start = jnp.cumsum(counts) - counts
    pad_end = jnp.cumsum(padded)
    pad_start = pad_end - padded
    n_blocks = (nk + N_EXPERTS * (MOE_ROWS - 1) + MOE_ROWS - 1) // MOE_ROWS
    cap = n_blocks * MOE_ROWS
    block_start = jnp.arange(n_blocks, dtype=jnp.int32) * MOE_ROWS
    block_e = jnp.minimum(jnp.sum((pad_end[None, :] <= block_start[:, None]).astype(jnp.int32), axis=1),
                          N_EXPERTS - 1)
    nused = (pad_end[-1] // MOE_ROWS).astype(jnp.int32).reshape(1)
    row = jnp.arange(cap, dtype=jnp.int32)
    row_e = jnp.repeat(block_e, MOE_ROWS)
    within = row - pad_start[row_e]
    valid = within < counts[row_e]
    src = jnp.clip(start[row_e] + within, 0, nk - 1)
    buf_tok = jnp.where(valid, order[src] // TOP_K, 0).astype(jnp.int32)
    slot_dest = (pad_start[flat_e] + rank - start[flat_e]).astype(jnp.int32)
    return gates, buf_tok, block_e, nused, slot_dest


def _block_ones(n, blk):
    idx = np.arange(n) // blk
    return jnp.asarray(idx[:, None] == idx[None, :], dtype=BF16)


def _pad_cols(w, width):
    return jnp.pad(w, ((0, 0), (0, width - w.shape[1])))


def _place_heads(w, per_head, offset, heads=MLA_HEADS, slot=MLA_PAD):
    rows = w.shape[0]
    w3 = w.reshape(rows, heads, per_head)
    out = jnp.zeros((rows, heads, slot), w.dtype).at[:, :, offset:offset + per_head].set(w3)
    return out.reshape(rows, heads * slot)


def _pack_in_weights(P, l):
    w_in = P["w_in"][l]
    sizes = (256, 128, 128, MLA_Q_LORA, MLA_KV_LORA, MLA_ROPE, 256, 256, 256, RWKV_COLS)
    offs = np.concatenate([[0], np.cumsum(sizes)])
    seg = [w_in[:, offs[j]:offs[j + 1]] for j in range(len(sizes))]
    krp = _place_heads(jnp.tile(seg[5], (1, MLA_HEADS)), MLA_ROPE, MLA_NOPE)
    w_all = jnp.concatenate([seg[0], seg[1], seg[2], _pad_cols(seg[3], 256), seg[4], krp,
                             seg[6], seg[7], seg[8], _pad_cols(seg[9], 1024)], axis=1).astype(BF16)
    q_up = _place_heads(P["mla_q_up"][l], MLA_QK, 0)
    q_up = jnp.pad(q_up, ((0, 256 - MLA_Q_LORA), (0, 0))).astype(BF16)
    kv_up = P["mla_kv_up"][l].reshape(MLA_KV_LORA, MLA_HEADS, MLA_NOPE + MLA_V)
    w_knope = _place_heads(kv_up[:, :, :MLA_NOPE].reshape(MLA_KV_LORA, -1), MLA_NOPE, 0).astype(BF16)
    w_v = kv_up[:, :, MLA_NOPE:].reshape(MLA_KV_LORA, MLA_HEADS * MLA_V).astype(BF16)
    width = MLA_HEADS * MLA_PAD

    def rowpad(v):
        return jnp.pad(v, (0, width - v.shape[0]))

    gains = jnp.stack([
        rowpad(jnp.tile(P["gqa_qn"][l], GQA_HEADS)),
        rowpad(jnp.tile(P["gqa_kn"][l], GQA_KV_HEADS)),
        rowpad(P["mla_qa_norm"][l]),
        rowpad(P["mla_kva_norm"][l]),
        _place_heads(jnp.tile(P["mla_qn"][l], MLA_HEADS)[None], MLA_QK, 0)[0],
        _place_heads(jnp.tile(P["mla_kn"][l], MLA_HEADS)[None], MLA_QK, 0)[0],
        rowpad(jnp.tile(P["diff_qn"][l], 2 * DIFF_HEADS)),
        rowpad(jnp.tile(P["diff_kn"][l], 2 * DIFF_HEADS)),
    ]).astype(F32)
    return dict(w_all=w_all, q_up=q_up, w_knope=w_knope, w_v=w_v, gains=gains,
                b64=_block_ones(256, 64), b32=_block_ones(256, 32), b128=_block_ones(width, MLA_PAD))


def _pack_rwkv(P, l):
    mix = _pad_cols(P["rwkv_mix"][l], 1024)
    w_lora = jnp.zeros((256, 1280), F32)
    w_lora = w_lora.at[0:32, 0:256].set(P["rwkv_w2"][l, 0]).at[32:64, 256:512].set(P["rwkv_w2"][l, 1])
    w_lora = w_lora.at[64:96, 512:768].set(P["rwkv_a2"][l, 0]).at[96:128, 768:1024].set(P["rwkv_a2"][l, 1])
    w_lora = w_lora.at[128:192, 1024:1280].set(P["rwkv_g2"][l]).astype(BF16)
    vec = jnp.zeros((SUBLANES, 512), F32)
    vec = vec.at[0].set(P["rwkv_w0"][l].reshape(-1)).at[1].set(P["rwkv_a0"][l].reshape(-1))
    vec = vec.at[2].set(jnp.concatenate([P["rwkv_kk"][l], P["rwkv_ka"][l]]))
    vec = vec.at[3].set(P["rwkv_rk"][l].reshape(-1))
    return dict(mix=mix, w_lora=w_lora, vec=vec, b64=_block_ones(256, 64))


def _rope_table(seq, rot, lanes_per_tile_group, lane_offset=0):
    rows = seq // GRID_W
    row = jnp.repeat(jnp.arange(rows, dtype=F32), GRID_W)
    col = (jnp.arange(rows * GRID_W) % GRID_W).astype(F32)
    n_freq = rot // 4
    inv_freq = ROPE_BASE ** (-jnp.arange(n_freq, dtype=F32) / n_freq)
    ang = jnp.stack([row[:, None] * inv_freq, col[:, None] * inv_freq], axis=1)
    cos = jnp.cos(ang)[:, :, None, :]
    sin = jnp.sin(ang)[:, :, None, :]
    cos_g = jnp.broadcast_to(cos, (seq, 2, 2, n_freq)).reshape(seq, rot)
    sin_g = (jnp.broadcast_to(sin, (seq, 2, 2, n_freq))
             * jnp.asarray([-1.0, 1.0], F32)[None, None, :, None]).reshape(seq, rot)
    cos_t = jnp.ones((seq, LANES), F32)
    sin_t = jnp.zeros((seq, LANES), F32)
    for start in range(lane_offset, LANES, lanes_per_tile_group):
        cos_t = cos_t.at[:, start:start + rot].set(cos_g)
        sin_t = sin_t.at[:, start:start + rot].set(sin_g)
    real = jnp.concatenate([cos_t, sin_t], axis=1)
    ident = jnp.concatenate([jnp.ones((seq, LANES), F32), jnp.zeros((seq, LANES), F32)], axis=1)
    return jnp.stack([real, ident])


def _to_heads(x, groups, seq, heads):
    d = x.shape[1] // heads
    return x.reshape(groups, seq, heads, d).transpose(0, 2, 1, 3).astype(BF16)


def _with_ones(v):
    return jnp.concatenate([v.astype(BF16), jnp.ones(v.shape, BF16)], axis=-1)


def _from_heads(o):
    g, h, s, d = o.shape
    return o.transpose(0, 2, 1, 3).reshape(g * s, h * d).astype(BF16)


def kernel(x_prompt, x_sample, cache_gqa_k, cache_gqa_v, cache_mla_ckv, cache_mla_krope, cache_diff_k, cache_diff_v, state_rwkv, c, c_ctx, norm1_g, norm2_g, ada_w, ada_b, w_in, w_out, gqa_qn, gqa_kn, mla_qa_norm, mla_q_up, mla_kva_norm, mla_kv_up, mla_qn, mla_kn, diff_qn, diff_kn, diff_lam, diff_subln, rwkv_mix, rwkv_w0, rwkv_w2, rwkv_a0, rwkv_a2, rwkv_rk, rwkv_g2, rwkv_kk, rwkv_ka, rwkv_ln_g, rwkv_ln_b, router_w, router_b, moe_w1, moe_b1, moe_w2, moe_b2):
    P = dict(w_in=w_in, gqa_qn=gqa_qn, gqa_kn=gqa_kn, mla_qa_norm=mla_qa_norm, mla_q_up=mla_q_up,
             mla_kva_norm=mla_kva_norm, mla_kv_up=mla_kv_up, mla_qn=mla_qn, mla_kn=mla_kn,
             diff_qn=diff_qn, diff_kn=diff_kn, rwkv_mix=rwkv_mix, rwkv_w0=rwkv_w0, rwkv_w2=rwkv_w2,
             rwkv_a0=rwkv_a0, rwkv_a2=rwkv_a2, rwkv_rk=rwkv_rk, rwkv_g2=rwkv_g2, rwkv_kk=rwkv_kk,
             rwkv_ka=rwkv_ka)
    depth = norm1_g.shape[0]
    bc, tc, d = x_prompt.shape
    bl, tl, _ = x_sample.shape
    assert bc * tc == tl, "context tokens must fill exactly one latent-sequence group"
    assert tl % ROW_TILE == 0 and tc % ROW_TILE == 0 and tc % SCAN_CHUNK == 0
    n_lat = bl * tl
    n = n_lat + tl
    groups = bl + 1
    past = cache_gqa_k.shape[3]

    cond = jnp.concatenate([c, c_ctx[None], jnp.zeros((SUBLANES - groups, d), F32)], axis=0)
    mods_all = _adaln(cond, ada_w, ada_b)
    tables = dict(a=_rope_table(tl, HEAD_DIM, HEAD_DIM),
                  m=_rope_table(tl, MLA_ROPE, LANES, lane_offset=MLA_NOPE),
                  c=_rope_table(tl, DIFF_QK, DIFF_QK))
    x = jnp.concatenate([x_sample.reshape(n_lat, d), x_prompt.reshape(tl, d)], axis=0)
    ctx_rows = slice(n_lat, n)
    new = [[] for _ in range(7)]

    for l in range(depth):
        lam_init = 0.8 - 0.6 * math.exp(-0.3 * l)
        mods = mods_all[l]
        wpack = _pack_in_weights(P, l)
        rpack = _pack_rwkv(P, l)
        (qa_h, ka_h, va_h, qb_h, kb_h, vb_h, qc_h, kc_h, vc_h, ka, va, ckvn, kr, kc, vc, pd) = _inproj(
            x, mods, norm1_g[l][None], wpack, tables, rows_per_group=tl, n_lat=n_lat)

        new[0].append(ka[ctx_rows].reshape(bc, tc, GQA_KV_HEADS, HEAD_DIM).transpose(0, 2, 1, 3))
        new[1].append(va[ctx_rows].reshape(bc, tc, GQA_KV_HEADS, HEAD_DIM).transpose(0, 2, 1, 3))
        new[2].append(ckvn[ctx_rows].reshape(bc, tc, MLA_KV_LORA))
        new[3].append(kr[ctx_rows, MLA_NOPE:MLA_NOPE + MLA_ROPE].reshape(bc, tc, MLA_ROPE))
        new[4].append(kc[ctx_rows].reshape(bc, tc, DIFF_HEADS, 2, DIFF_QK).transpose(0, 2, 3, 1, 4))
        new[5].append(vc[ctx_rows].reshape(bc, tc, DIFF_HEADS, DIFF_V).transpose(0, 2, 1, 3))

        o_a = _attention(qa_h, ka_h, va_h, cache_gqa_k[:, l].astype(BF16), _with_ones(cache_gqa_v[:, l]),
                         n_req=bl, seq=tc, ctx_group=bl)
        krp_c = _place_heads(jnp.tile(cache_mla_krope[:, l].reshape(bl * past, MLA_ROPE), (1, MLA_HEADS)),
                             MLA_ROPE, MLA_NOPE)
        kb_c, vb_c = _mla_cache(cache_mla_ckv[:, l].reshape(bl * past, MLA_KV_LORA), krp_c, wpack)
        o_b = _attention(qb_h, kb_h, vb_h,
                         _to_heads(kb_c, bl, past, MLA_HEADS), _with_ones(_to_heads(vb_c, bl, past, MLA_HEADS)),
                         n_req=bl, seq=tc, ctx_group=bl)
        kc_c = cache_diff_k[:, l].transpose(0, 1, 3, 2, 4).reshape(bl, DIFF_HEADS, past, 2 * DIFF_QK)
        o_c = _attention(qc_h, kc_h, vc_h, kc_c.astype(BF16), _with_ones(cache_diff_v[:, l]),
                         n_req=bl, seq=tc, ctx_group=bl,
                         diff_params=(diff_lam[l], diff_subln[l][None]), lam_init=lam_init)

        names = ("r", "v", "kk", "e0", "e1", "k0", "k1", "b0", "b1", "g", "bonus")
        rw = dict(zip(names, _rwkv_prep(pd, rpack, n_lat=n_lat, lat_seq=tl, ctx_seq=tc)))
        streams = {k_: rw[k_] for k_ in names[:9]}
        h0_lat = jnp.swapaxes(state_rwkv[:, l], -1, -2)
        y0l, y1l, _ = _rwkv_scan(streams, h0_lat, first_seq=0, seq_len=tl)
        h0_ctx = jnp.zeros((bc, 2, RWKV_HEADS, RWKV_N, RWKV_N), F32)
        y0c, y1c, h_ctx = _rwkv_scan(streams, h0_ctx, first_seq=n_lat // tc, seq_len=tc)
        new[6].append(jnp.swapaxes(h_ctx, -1, -2))
        y0 = jnp.concatenate([y0l.reshape(n_lat, RWKV_W), y0c.reshape(tl, RWKV_W)], axis=0)
        y1 = jnp.concatenate([y1l.reshape(n_lat, RWKV_W), y1c.reshape(tl, RWKV_W)], axis=0)

        ln = jnp.stack([rwkv_ln_g[l], rwkv_ln_b[l]])
        rw_f = _pad_cols(router_w[l], LANES)
        rw_hi = rw_f.astype(BF16)
        rw_lo = (rw_f - rw_hi.astype(F32)).astype(BF16)
        rb = _pad_cols(router_b[l][None], LANES)
        x1, h2, logits = _outproj(x, o_a, o_b, o_c, y0, y1,
                                  rw["bonus"], rw["g"], ln, rpack["b64"], w_out[l].astype(BF16), mods,
                                  norm2_g[l][None], rw_hi, rw_lo, rb, rows_per_group=tl)

        gates, buf_tok, block_e, nused, slot_dest = _route(logits[:, :N_EXPERTS])
        yb = _moe_experts(h2, block_e, nused, buf_tok, moe_w1, moe_w2,
                          moe_b1[:, :, None, :], moe_b2[:, :, None, :], l)
        x = _moe_combine(yb, x1, gates, slot_dest, mods, rows_per_group=tl)

    y_sample = x[:n_lat].reshape(bl, tl, d)
    y_prompt = x[n_lat:].reshape(bc, tc, d)
    return (y_prompt, y_sample) + tuple(jnp.stack(t, axis=1) for t in new)
```

```python
import functools
import math

import jax
import jax.numpy as jnp
import numpy as np
from jax import lax
from jax.experimental import pallas as pl
from jax.experimental.pallas import tpu as pltpu

F32 = jnp.float32
BF16 = jnp.bfloat16

GRID_W = 64
ROPE_BASE = 10000.0
EPS = 1e-6
HEAD_DIM = 64
GQA_HEADS, GQA_KV_HEADS = 4, 2
MLA_HEADS, MLA_Q_LORA, MLA_KV_LORA, MLA_NOPE, MLA_ROPE, MLA_V = 4, 192, 128, 64, 32, 64
MLA_QK = MLA_NOPE + MLA_ROPE
MLA_PAD = 128
DIFF_HEADS, DIFF_QK, DIFF_V = 4, 32, 64
RWKV_HEADS, RWKV_N = 4, 64
RWKV_W = RWKV_HEADS * RWKV_N
DECAY_LORA, AAA_LORA, GATE_LORA = 32, 32, 64
RWKV_GN_EPS = 64e-5
RWKV_COLS = 3 * RWKV_W + 2 * DECAY_LORA + 2 * AAA_LORA + GATE_LORA
N_EXPERTS, TOP_K = 32, 4
SWIGLU_ALPHA, SWIGLU_LIMIT = 1.702, 7.0

LANES = 128
SUBLANES = 8
VMEM_LIMIT = 48 * 1024 * 1024
MOE_VMEM_LIMIT = 56 * 1024 * 1024
ROW_TILE = 256
Q_TILE = 256
KV_CHUNK = 512
SCAN_CHUNK = 64
SCAN_SUB = 16
SCAN_SEQS = 4
MOE_ROWS = 256
MOE_SLOTS = 3
COMB_ROWS = 128
ADA_COLS = 1536
DMA_UNROLL = 8


def _cparams(sem, vmem=VMEM_LIMIT):
    return pltpu.CompilerParams(dimension_semantics=sem, vmem_limit_bytes=vmem)


def _mm(a, b):
    return jnp.dot(a.astype(BF16), b.astype(BF16), preferred_element_type=F32)


def _mm_nt(a, b):
    return lax.dot_general(a.astype(BF16), b.astype(BF16), (((1,), (1,)), ((), ())),
                           preferred_element_type=F32)


def _split2(x):
    hi = x.astype(BF16)
    lo = (x - hi.astype(F32)).astype(BF16)
    return hi, lo


def _split3(x):
    hi = x.astype(BF16)
    r = x - hi.astype(F32)
    mid = r.astype(BF16)
    lo = (r - mid.astype(F32)).astype(BF16)
    return hi, mid, lo


def _mm_x2(x, ones_blk):
    hi, lo = _split2(x)
    return (jnp.dot(hi, ones_blk, preferred_element_type=F32)
            + jnp.dot(lo, ones_blk, preferred_element_type=F32))


def _mm3(a, b):
    ah, al = _split2(a)
    bh, bl = _split2(b)
    return (jnp.dot(ah, bh, preferred_element_type=F32)
            + jnp.dot(ah, bl, preferred_element_type=F32)
            + jnp.dot(al, bh, preferred_element_type=F32))


def _rope(x, cos, sin_signed, rot):
    n = x.shape[-1]
    q = rot // 4
    lane = lax.broadcasted_iota(jnp.int32, x.shape, 1)
    first = (lane & (rot // 2 - 1)) < q
    partner = jnp.where(first, pltpu.roll(x, n - q, 1), pltpu.roll(x, q, 1))
    return x * cos + partner * sin_signed


def _tile_lanes(x, reps):
    return x if reps == 1 else jnp.concatenate([x] * reps, axis=1)


def _ada_kernel(c_ref, w_ref, b_ref, o_ref):
    c = c_ref[...]
    o_ref[0] = _mm(c * jax.nn.sigmoid(c), w_ref[0]) + b_ref[0]


def _adaln(cond, ada_w, ada_b):
    depth, d, cols = ada_w.shape
    rows = cond.shape[0]
    return pl.pallas_call(
        _ada_kernel,
        out_shape=jax.ShapeDtypeStruct((depth, rows, cols), F32),
        grid=(depth, cols // ADA_COLS),
        in_specs=[
            pl.BlockSpec((rows, d), lambda l, j: (0, 0)),
            pl.BlockSpec((1, d, ADA_COLS), lambda l, j: (l, 0, j)),
            pl.BlockSpec((1, 1, ADA_COLS), lambda l, j: (l, 0, j)),
        ],
        out_specs=pl.BlockSpec((1, rows, ADA_COLS), lambda l, j: (l, 0, j)),
        compiler_params=_cparams(("arbitrary", "arbitrary")),
        name="adaln",
    )(cond, ada_w, ada_b.reshape(depth, 1, cols))


_C_AQ, _C_AK, _C_AV = 0, 256, 384
_C_CQ, _C_CKV, _C_KRP = 512, 768, 896
_C_DQ, _C_DK, _C_DV = 1408, 1664, 1920
_C_PD = 2176
_C_END = 3200


def _mla_kv(ckvn, krp, w_knope, w_v, kn_gain, blk128, cos, sin):
    kpre = _mm(ckvn, w_knope) + krp
    ss = _mm_x2(kpre * kpre, blk128) * (1.0 / MLA_QK)
    kb = kpre * lax.rsqrt(ss + EPS) * kn_gain
    kb = _rope(kb, cos, sin, MLA_ROPE)
    return kb, _mm(ckvn, w_v)


def _store_heads(o_ref, x, width, ones=False):
    for h in range(o_ref.shape[1]):
        piece = x[:, width * h:width * (h + 1)]
        if ones:
            piece = jnp.concatenate([piece, jnp.ones_like(piece)], axis=1)
        o_ref[0, h] = piece.astype(o_ref.dtype)


def _inproj_kernel(x_ref, sh_ref, sc_ref, g1_ref, w_ref, qup_ref, wkn_ref, wv_ref, gains_ref,
                   b64_ref, b32_ref, b128_ref, ra_ref, rm_ref, rc_ref,
                   qa_h, ka_h, va_h, qb_h, kb_h, vb_h, qc_h, kc_h, vc_h,
                   ka_o, va_o, ckv_o, kr_o, kc_o, vc_o, pd_o,
                   *, rows_per_group):
    i = pl.program_id(0)
    g = (i * ROW_TILE) // rows_per_group
    x = x_ref[...]
    ms = jnp.mean(x * x, axis=-1, keepdims=True)
    xn = x * lax.rsqrt(ms + EPS) * g1_ref[...]
    h = xn * (1.0 + sc_ref[pl.ds(g, 1), :]) + sh_ref[pl.ds(g, 1), :]
    p = _mm(h, w_ref[...])
    gains = gains_ref[...]
    b64, b32, b128 = b64_ref[...], b32_ref[...], b128_ref[...]

    ra = ra_ref[0]
    cos_a, sin_a = ra[:, :LANES], ra[:, LANES:]
    aq = p[:, _C_AQ:_C_AK]
    ssq = _mm_x2(aq * aq, b64) * (1.0 / HEAD_DIM)
    qa = aq * lax.rsqrt(ssq + EPS) * gains[0:1, :256]
    qa = _rope(qa, _tile_lanes(cos_a, 2), _tile_lanes(sin_a, 2), HEAD_DIM)
    _store_heads(qa_h, qa * (HEAD_DIM ** -0.5), HEAD_DIM)
    ak = p[:, _C_AK:_C_AV]
    ssk = _mm_x2(ak * ak, b64[:LANES, :LANES]) * (1.0 / HEAD_DIM)
    ka = ak * lax.rsqrt(ssk + EPS) * gains[1:2, :128]
    ka = _rope(ka, cos_a, sin_a, HEAD_DIM)
    ka_o[...] = ka
    _store_heads(ka_h, ka, HEAD_DIM)
    va = p[:, _C_AV:_C_CQ]
    va_o[...] = va
    _store_heads(va_h, va, HEAD_DIM, ones=True)

    rm = rm_ref[0]
    cos_m, sin_m = _tile_lanes(rm[:, :LANES], MLA_HEADS), _tile_lanes(rm[:, LANES:], MLA_HEADS)
    cq = p[:, _C_CQ:_C_CKV]
    cqn = cq * lax.rsqrt(jnp.sum(cq * cq, axis=-1, keepdims=True) * (1.0 / MLA_Q_LORA) + EPS)
    qb = _mm(cqn * gains[2:3, :256], qup_ref[...])
    ssb = _mm_x2(qb * qb, b128) * (1.0 / MLA_QK)
    qb = qb * lax.rsqrt(ssb + EPS) * gains[4:5, :]
    _store_heads(qb_h, _rope(qb, cos_m, sin_m, MLA_ROPE) * (MLA_QK ** -0.5), MLA_PAD)
    ckv = p[:, _C_CKV:_C_KRP]
    ckvn = ckv * lax.rsqrt(jnp.mean(ckv * ckv, axis=-1, keepdims=True) + EPS) * gains[3:4, :128]
    ckv_o[...] = ckvn
    krp = p[:, _C_KRP:_C_DQ]
    kr_o[...] = krp[:, :LANES]
    kb, vb = _mla_kv(ckvn, krp, wkn_ref[...], wv_ref[...], gains[5:6, :], b128, cos_m, sin_m)
    _store_heads(kb_h, kb, MLA_PAD)
    _store_heads(vb_h, vb, MLA_V, ones=True)

    rc = rc_ref[0]
    cos_c, sin_c = _tile_lanes(rc[:, :LANES], 2), _tile_lanes(rc[:, LANES:], 2)
    dq = p[:, _C_DQ:_C_DK]
    ssd = _mm_x2(dq * dq, b32) * (1.0 / DIFF_QK)
    qc = dq * lax.rsqrt(ssd + EPS) * gains[6:7, :256]
    _store_heads(qc_h, _rope(qc, cos_c, sin_c, DIFF_QK) * (DIFF_QK ** -0.5), 2 * DIFF_QK)
    dk = p[:, _C_DK:_C_DV]
    ssd = _mm_x2(dk * dk, b32) * (1.0 / DIFF_QK)
    kc = dk * lax.rsqrt(ssd + EPS) * gains[7:8, :256]
    kc = _rope(kc, cos_c, sin_c, DIFF_QK)
    kc_o[...] = kc
    _store_heads(kc_h, kc, 2 * DIFF_QK)
    vc = p[:, _C_DV:_C_PD]
    vc_o[...] = vc
    _store_heads(vc_h, vc, DIFF_V, ones=True)

    pd_o[...] = p[:, _C_PD:_C_END]


def _inproj(x, mods, g1, wpack, tables, *, rows_per_group, n_lat):
    n, d = x.shape
    steps = n // ROW_TILE
    lat_steps = n_lat // ROW_TILE
    pos_steps = rows_per_group // ROW_TILE

    def rope_map(i):
        is_ctx = i >= lat_steps
        return (jnp.where(is_ctx, 1, 0), jnp.where(is_ctx, 0, i % pos_steps), 0)

    full = lambda shape: pl.BlockSpec(shape, lambda i: (0,) * len(shape))
    row = lambda w: pl.BlockSpec((ROW_TILE, w), lambda i: (i, 0))
    groups = n // rows_per_group
    head_shapes = ((GQA_HEADS, HEAD_DIM), (GQA_KV_HEADS, HEAD_DIM), (GQA_KV_HEADS, 2 * HEAD_DIM),
                   (MLA_HEADS, MLA_PAD), (MLA_HEADS, MLA_PAD), (MLA_HEADS, 2 * MLA_V),
                   (DIFF_HEADS, 2 * DIFF_QK), (DIFF_HEADS, 2 * DIFF_QK), (DIFF_HEADS, 2 * DIFF_V))
    heads = lambda hh, w: pl.BlockSpec((1, hh, ROW_TILE, w), lambda i: (i // pos_steps, 0, i % pos_steps, 0))
    widths = (128, 128, 128, 128, 256, 256, 1024)
    return pl.pallas_call(
        functools.partial(_inproj_kernel, rows_per_group=rows_per_group),
        out_shape=(tuple(jax.ShapeDtypeStruct((groups, hh, rows_per_group, w), BF16) for hh, w in head_shapes)
                   + tuple(jax.ShapeDtypeStruct((n, w), F32) for w in widths)),
        grid=(steps,),
        in_specs=[
            row(d),
            pl.BlockSpec((SUBLANES, d), lambda i: (0, 0)),
            pl.BlockSpec((SUBLANES, d), lambda i: (0, 1)),
            full((1, d)),
            full(wpack["w_all"].shape), full(wpack["q_up"].shape), full(wpack["w_knope"].shape),
            full(wpack["w_v"].shape), full(wpack["gains"].shape),
            full(wpack["b64"].shape), full(wpack["b32"].shape), full(wpack["b128"].shape),
            pl.BlockSpec((1, ROW_TILE, 2 * LANES), rope_map),
            pl.BlockSpec((1, ROW_TILE, 2 * LANES), rope_map),
            pl.BlockSpec((1, ROW_TILE, 2 * LANES), rope_map),
        ],
        out_specs=tuple(heads(hh, w) for hh, w in head_shapes) + tuple(row(w) for w in widths),
        compiler_params=_cparams(("arbitrary",)),
        name="inproj",
    )(x, mods, mods, g1, wpack["w_all"], wpack["q_up"], wpack["w_knope"], wpack["w_v"],
      wpack["gains"], wpack["b64"], wpack["b32"], wpack["b128"],
      tables["a"], tables["m"], tables["c"])


def _mla_cache_kernel(ckv_ref, krp_ref, wkn_ref, wv_ref, gains_ref, b128_ref, kb_o, vb_o):
    kn = gains_ref[...][5:6, :]
    one = jnp.ones((1, MLA_HEADS * MLA_PAD), F32)
    kb, vb = _mla_kv(ckv_ref[...], krp_ref[...], wkn_ref[...], wv_ref[...], kn, b128_ref[...],
                     one, jnp.zeros_like(one))
    kb_o[...] = kb
    vb_o[...] = vb


def _mla_cache(ckv, krp, wpack):
    rows = ckv.shape[0]
    tile = min(rows, ROW_TILE)
    full = lambda shape: pl.BlockSpec(shape, lambda i: (0,) * len(shape))
    row = lambda w: pl.BlockSpec((tile, w), lambda i: (i, 0))
    kw, vw = MLA_HEADS * MLA_PAD, MLA_HEADS * MLA_V
    return pl.pallas_call(
        _mla_cache_kernel,
        out_shape=(jax.ShapeDtypeStruct((rows, kw), F32), jax.ShapeDtypeStruct((rows, vw), F32)),
        grid=(rows // tile,),
        in_specs=[row(MLA_KV_LORA), row(kw), full(wpack["w_knope"].shape), full(wpack["w_v"].shape),
                  full(wpack["gains"].shape), full(wpack["b128"].shape)],
        out_specs=(row(kw), row(vw)),
        compiler_params=_cparams(("arbitrary",)),
        name="mla_cache",
    )(ckv, krp, wpack["w_knope"], wpack["w_v"], wpack["gains"], wpack["b128"])


def _softmax_streams(qs, kv_heads, segs, s_scr):
    pieces = []
    off = 0
    for k_ref, v_ref in segs:
        n_keys = k_ref.shape[2]
        step = min(KV_CHUNK, n_keys)
        for c0 in range(0, n_keys, step):
            pieces.append((k_ref, v_ref, c0, step, off + c0))
        off += n_keys
    row_max = []
    for j, q in enumerate(qs):
        m = None
        for k_ref, _, c0, step, col in pieces:
            s = _mm_nt(q, k_ref[0, kv_heads[j], c0:c0 + step, :])
            s_scr[j, :, col:col + step] = s
            for t0 in range(0, step, LANES):
                part = s[:, t0:t0 + LANES]
                m = part if m is None else jnp.maximum(m, part)
        row_max.append(jnp.max(m, axis=-1, keepdims=True))
    outs = []
    for j in range(len(qs)):
        acc = None
        for _, v_ref, c0, step, col in pieces:
            p = jnp.exp((s_scr[j, :, col:col + step] - row_max[j]).astype(BF16))
            pv = jnp.dot(p, v_ref[0, kv_heads[j], c0:c0 + step, :], preferred_element_type=F32)
            acc = pv if acc is None else acc + pv
        den = pltpu.roll(acc, HEAD_DIM, 1)
        outs.append((acc / den)[:, :HEAD_DIM])
    return outs


def _attn_kernel(*refs, has_cache, diff, lam_init, kv_heads):
    refs = list(refs)
    if diff:
        lam_ref, sub_ref = refs.pop(0), refs.pop(0)
    q_ref, k_ref, v_ref = refs[:3]
    o_ref, s_scr = refs[-2], refs[-1]
    segs = [(k_ref, v_ref)]
    if has_cache:
        segs.append((refs[3], refs[4]))
    if not diff:
        outs = _softmax_streams([q_ref[0, j] for j in range(q_ref.shape[1])], kv_heads, segs, s_scr)
        o_ref[...] = jnp.concatenate(outs, axis=1).astype(o_ref.dtype)
        return
    qs = []
    for j in range(q_ref.shape[1]):
        q = q_ref[0, j]
        lane = lax.broadcasted_iota(jnp.int32, q.shape, 1)
        zero = jnp.zeros_like(q)
        qs += [jnp.where(lane < DIFF_QK, q, zero), jnp.where(lane >= DIFF_QK, q, zero)]
    outs = _softmax_streams(qs, kv_heads, segs, s_scr)
    lv = lam_ref[...]
    lam = (jnp.exp(jnp.sum(lv[0:1] * lv[1:2], axis=-1, keepdims=True))
           - jnp.exp(jnp.sum(lv[2:3] * lv[3:4], axis=-1, keepdims=True)) + lam_init)
    heads = []
    for j in range(q_ref.shape[1]):
        o = outs[2 * j] - lam * outs[2 * j + 1]
        o = o * lax.rsqrt(jnp.mean(o * o, axis=-1, keepdims=True) + EPS) * sub_ref[...]
        heads.append(o * (1.0 - lam_init))
    o_ref[...] = jnp.concatenate(heads, axis=1).astype(o_ref.dtype)


def _attention(q, k, v, kc, vc, *, n_req, seq, ctx_group, diff_params=None, lam_init=0.0):
    g_all, hq, s_all, dq = q.shape
    hk, dk, dvp = k.shape[1], k.shape[3], v.shape[3]
    diff = diff_params is not None
    qh = 2
    if diff:
        kh, kv_heads = 2, (0, 0, 1, 1)
    elif hq == 2 * hk:
        kh, kv_heads = 1, (0, 0)
    else:
        kh, kv_heads = 2, (0, 1)
    n_streams = len(kv_heads)
    steps_h = hq // qh
    out_w = qh * HEAD_DIM
    q_tiles = s_all // Q_TILE
    extra_in, extra_specs2, extra_specs3 = [], [], []
    if diff:
        extra_in = [diff_params[0], diff_params[1]]
        extra_specs3 = [pl.BlockSpec(diff_params[0].shape, lambda b, h, i: (0, 0)),
                        pl.BlockSpec(diff_params[1].shape, lambda b, h, i: (0, 0))]
        extra_specs2 = [pl.BlockSpec(diff_params[0].shape, lambda b, h: (0, 0)),
                        pl.BlockSpec(diff_params[1].shape, lambda b, h: (0, 0))]
    past = kc.shape[2]
    body = functools.partial(_attn_kernel, diff=diff, lam_init=lam_init, kv_heads=kv_heads)
    lat = pl.pallas_call(
        functools.partial(body, has_cache=True),
        out_shape=jax.ShapeDtypeStruct((n_req * s_all, hq * HEAD_DIM), BF16),
        grid=(n_req, steps_h, q_tiles),
        in_specs=extra_specs3 + [
            pl.BlockSpec((1, qh, Q_TILE, dq), lambda b, h, i: (b, h, i, 0)),
            pl.BlockSpec((1, kh, s_all, dk), lambda b, h, i: (b, h, 0, 0)),
            pl.BlockSpec((1, kh, s_all, dvp), lambda b, h, i: (b, h, 0, 0)),
            pl.BlockSpec((1, kh, past, dk), lambda b, h, i: (b, h, 0, 0)),
            pl.BlockSpec((1, kh, past, dvp), lambda b, h, i: (b, h, 0, 0)),
        ],
        out_specs=pl.BlockSpec((Q_TILE, out_w), lambda b, h, i: (b * q_tiles + i, h)),
        scratch_shapes=[pltpu.VMEM((n_streams, Q_TILE, s_all + past), F32)],
        compiler_params=_cparams(("arbitrary", "arbitrary", "arbitrary")),
        name="attn_latent",
    )(*extra_in, q, k, v, kc, vc)
    n_seq = s_all // seq
    ctx = pl.pallas_call(
        functools.partial(body, has_cache=False),
        out_shape=jax.ShapeDtypeStruct((s_all, hq * HEAD_DIM), BF16),
        grid=(n_seq, steps_h),
        in_specs=extra_specs2 + [
            pl.BlockSpec((1, qh, seq, dq), lambda s, h: (ctx_group, h, s, 0)),
            pl.BlockSpec((1, kh, seq, dk), lambda s, h: (ctx_group, h, s, 0)),
            pl.BlockSpec((1, kh, seq, dvp), lambda s, h: (ctx_group, h, s, 0)),
        ],
        out_specs=pl.BlockSpec((seq, out_w), lambda s, h: (s, h)),
        scratch_shapes=[pltpu.VMEM((n_streams, seq, seq), F32)],
        compiler_params=_cparams(("arbitrary", "arbitrary")),
        name="attn_context",
    )(*extra_in, q, k, v)
    return jnp.concatenate([lat, ctx], axis=0)


def _rprep_kernel(p_ref, pp_ref, pn_ref, mix_ref, wl_ref, vec_ref, b64_ref,
                  r_o, v_o, kk_o, e0_o, e1_o, k0_o, k1_o, b0_o, b1_o, g_o, bon_o,
                  *, lat_steps, lat_seq_steps, ctx_seq_steps):
    i = pl.program_id(0)
    is_lat = i < lat_steps
    pos = jnp.where(is_lat, i % lat_seq_steps, (i - lat_steps) % ctx_seq_steps)
    last = jnp.where(is_lat, lat_seq_steps - 1, ctx_seq_steps - 1)
    at_start = pos == 0
    at_end = pos == last
    p = p_ref[...]
    rows = lax.broadcasted_iota(jnp.int32, p.shape, 0)
    prev_edge = jnp.where(at_start, 0.0, pp_ref[SUBLANES - 1:SUBLANES, :])
    next_edge = jnp.where(at_end, 0.0, pn_ref[0:1, :])
    prev = jnp.where(rows == 0, prev_edge, pltpu.roll(p, 1, 0))
    nxt = jnp.where(rows == ROW_TILE - 1, next_edge, pltpu.roll(p, ROW_TILE - 1, 0))
    mix = mix_ref[...]
    xs = p + (prev - p) * mix[0:1] + (nxt - p) * mix[1:2]
    r, k, v = xs[:, 0:256], xs[:, 256:512], xs[:, 512:768]
    lo = xs[:, 768:1024]
    lane = lax.broadcasted_iota(jnp.int32, lo.shape, 1)
    act = jnp.where(lane < 2 * DECAY_LORA, jnp.tanh(lo),
                    jnp.where(lane < 2 * (DECAY_LORA + AAA_LORA), lo, jax.nn.sigmoid(lo)))
    lora = _mm(act, wl_ref[...])
    vec = vec_ref[...]
    z = -(vec[0:1, :] + lora[:, 0:512])
    softplus = jnp.maximum(z, 0.0) + jnp.log1p(jnp.exp(-jnp.abs(z)))
    e = jnp.exp(-softplus - 0.5)
    a = jax.nn.sigmoid(vec[1:2, :] + lora[:, 512:1024])
    g_o[...] = lora[:, 1024:1280]
    b64 = b64_ref[...]
    kk = k * vec[2:3, 0:256]
    kk = kk * lax.rsqrt(_mm_x2(kk * kk, b64) + 1e-12)
    ka = vec[2:3, 256:512]
    r_o[...] = r
    v_o[...] = v
    kk_o[...] = kk
    bonus = None
    for d, (e_o, k_o, b_o) in enumerate(((e0_o, k0_o, b0_o), (e1_o, k1_o, b1_o))):
        a_d = a[:, 256 * d:256 * (d + 1)]
        k_d = k * (1.0 + (a_d - 1.0) * ka)
        e_o[...] = e[:, 256 * d:256 * (d + 1)]
        k_o[...] = k_d
        b_o[...] = kk * a_d
        term = _mm_x2(r * k_d * vec[3:4, 256 * d:256 * (d + 1)], b64) * v
        bonus = term if bonus is None else bonus + term
    bon_o[...] = bonus


def _rwkv_prep(pd, rpack, *, n_lat, lat_seq, ctx_seq):
    n, w = pd.shape
    steps = n // ROW_TILE
    halo_blocks = n // SUBLANES
    per = ROW_TILE // SUBLANES
    full = lambda shape: pl.BlockSpec(shape, lambda i: (0,) * len(shape))
    row = lambda width: pl.BlockSpec((ROW_TILE, width), lambda i: (i, 0))
    return pl.pallas_call(
        functools.partial(_rprep_kernel, lat_steps=n_lat // ROW_TILE,
                          lat_seq_steps=lat_seq // ROW_TILE, ctx_seq_steps=ctx_seq // ROW_TILE),
        out_shape=tuple(jax.ShapeDtypeStruct((n, RWKV_W), F32) for _ in range(11)),
        grid=(steps,),
        in_specs=[
            row(w),
            pl.BlockSpec((SUBLANES, w), lambda i: (jnp.maximum(i * per - 1, 0), 0)),
            pl.BlockSpec((SUBLANES, w), lambda i: (jnp.minimum((i + 1) * per, halo_blocks - 1), 0)),
            full(rpack["mix"].shape), full(rpack["w_lora"].shape), full(rpack["vec"].shape),
            full(rpack["b64"].shape),
        ],
        out_specs=tuple(row(RWKV_W) for _ in range(11)),
        compiler_params=_cparams(("arbitrary",)),
        name="rwkv_prep",
    )(pd, pd, pd, rpack["mix"], rpack["w_lora"], rpack["vec"], rpack["b64"])


def _bmm(a, b):
    return lax.dot_general(a.astype(BF16), b.astype(BF16), (((2,), (1,)), ((0,), (0,))),
                           preferred_element_type=F32)


def _bmm_nt(a, b):
    return lax.dot_general(a.astype(BF16), b.astype(BF16), (((2,), (2,)), ((0,), (0,))),
                           preferred_element_type=F32)


def _bmm3(a, b):
    ah, al = _split2(a)
    bh, bl = _split2(b)
    return _bmm(ah, bh) + _bmm(ah, bl) + _bmm(al, bh)


def _btranspose(x):
    return jnp.stack([x[i].T for i in range(x.shape[0])])


def _scan_chunks(r, v, kk, e, k, b, h0, rev):
    n, c, _ = r.shape
    ti = lax.broadcasted_iota(jnp.int32, (n, c, c), 1)
    si = lax.broadcasted_iota(jnp.int32, (n, c, c), 2)
    order = (si - ti) * jnp.where(rev, -1, 1)
    incl = order <= 0
    strict = order < 0
    e_hi, e_mid, e_lo = _split3(e)
    incl_b = jnp.where(incl, 1.0, 0.0).astype(BF16)
    cs = _bmm(incl_b, e_hi) + _bmm(incl_b, e_mid) + _bmm(incl_b, e_lo)
    g_prev = jnp.exp(e - cs)
    g_incl = jnp.exp(-cs)
    g_inv = jnp.exp(cs)
    kkg, rg, bq, kq = kk * g_prev, r * g_incl, b * g_inv, k * g_inv
    left = jnp.concatenate([kkg, rg], axis=1)
    sb = _bmm_nt(left, bq)
    sk = _bmm_nt(left, kq)
    zero = jnp.zeros((n, c, c), F32)
    a_m = jnp.where(strict, sb[:, :c], zero)
    b_m = jnp.where(strict, sk[:, :c], zero)
    a_p = jnp.where(incl, sb[:, c:], zero)
    b_p = jnp.where(incl, sk[:, c:], zero)

    eye = (ti == si).astype(F32)
    same = (ti // SCAN_SUB) == (si // SCAN_SUB)
    a_d = jnp.where(same, a_m, zero)
    a_off = a_m - a_d
    t_d = eye - a_d
    pw = _bmm3(a_d, a_d)
    steps = int(math.log2(SCAN_SUB))
    for j in range(1, steps):
        t_d = t_d + _bmm3(t_d, pw)
        if j + 1 < steps:
            pw = _bmm3(pw, pw)
    nil = _bmm3(t_d, a_off)
    inm = eye - nil
    nblk = c // SCAN_SUB
    if nblk > 2:
        nil2 = _bmm3(nil, nil)
        inm = inm + _bmm3(inm, nil2)
    t_full = _bmm3(inm, t_d) if nblk > 1 else t_d

    bmv = _bmm(b_m, v)
    w1 = _bmm(t_full, kkg)
    z = _bmm(t_full, bmv)
    cs_t = _btranspose(cs)
    tot_col = jnp.maximum(cs_t[:, :, 0:1], cs_t[:, :, c - 1:c])
    g_end_t = jnp.exp(cs_t - tot_col)
    kt_t = _btranspose(k) * g_end_t
    bt_t = _btranspose(b) * g_end_t
    u = _bmm(w1, h0) + z
    h_new = jnp.exp(-tot_col) * h0 + _bmm(kt_t, v) - _bmm(bt_t, u)
    y = _bmm(rg, h0) - _bmm(a_p, u) + _bmm(b_p, v)
    return y, h_new


def _scan_kernel(rf, vf, kkf, ef, kf, bf, rb, vb, kkb, eb, kb, bb, h0_ref,
                 y0_ref, y1_ref, hout_ref, h_scr, *, n_seq, n_chunks):
    c = pl.program_id(1)

    @pl.when(c == 0)
    def _():
        h_scr[...] = h0_ref[...]

    groups = ((rf, vf, kkf, ef, kf, bf), (rb, vb, kkb, eb, kb, bb))
    loaded = [[ref[...] for ref in refs] for refs in groups]
    ops = []
    for j in range(6):
        ops.append(jnp.stack([loaded[d][j][s, :, RWKV_N * h:RWKV_N * (h + 1)]
                              for s in range(n_seq) for d in range(2) for h in range(RWKV_HEADS)]))
    n = n_seq * 2 * RWKV_HEADS
    chain = lax.broadcasted_iota(jnp.int32, (n, 1, 1), 0)
    rev = ((chain // RWKV_HEADS) % 2) == 1
    h0 = h_scr[...].reshape(n, RWKV_N, RWKV_N)
    y, h_new = _scan_chunks(*ops, h0, rev)
    h_scr[...] = h_new.reshape(h_scr.shape)
    for s in range(n_seq):
        for d, y_ in enumerate((y0_ref, y1_ref)):
            base = (s * 2 + d) * RWKV_HEADS
            y_[s] = jnp.concatenate([y[base + h] for h in range(RWKV_HEADS)], axis=1)

    @pl.when(c == n_chunks - 1)
    def _():
        hout_ref[...] = h_scr[...]


def _rwkv_scan(streams, h0, *, first_seq, seq_len):
    n_chunks = seq_len // SCAN_CHUNK
    total = h0.shape[0]
    sb = min(SCAN_SEQS, total)
    assert total % sb == 0 and first_seq % sb == 0
    arrs = {name: a.reshape(-1, seq_len, RWKV_W) for name, a in streams.items()}
    blk0 = first_seq // sb
    blk = (sb, SCAN_CHUNK, RWKV_W)
    fwd = pl.BlockSpec(blk, lambda s, c: (blk0 + s, c, 0))
    bwd = pl.BlockSpec(blk, lambda s, c: (blk0 + s, n_chunks - 1 - c, 0))
    st_shape = (sb,) + h0.shape[1:]
    st = pl.BlockSpec(st_shape, lambda s, c: (s, 0, 0, 0, 0))
    y_shape = jax.ShapeDtypeStruct((total, seq_len, RWKV_W), F32)
    y0, y1, h_fin = pl.pallas_call(
        functools.partial(_scan_kernel, n_seq=sb, n_chunks=n_chunks),
        out_shape=(y_shape, y_shape, jax.ShapeDtypeStruct(h0.shape, F32)),
        grid=(total // sb, n_chunks),
        in_specs=[fwd] * 6 + [bwd] * 6 + [st],
        out_specs=(pl.BlockSpec(blk, lambda s, c: (s, c, 0)),
                   pl.BlockSpec(blk, lambda s, c: (s, n_chunks - 1 - c, 0)),
                   st),
        scratch_shapes=[pltpu.VMEM(st_shape, F32)],
        compiler_params=_cparams(("arbitrary", "arbitrary")),
        name="rwkv_scan",
    )(arrs["r"], arrs["v"], arrs["kk"], arrs["e0"], arrs["k0"], arrs["b0"],
      arrs["r"], arrs["v"], arrs["kk"], arrs["e1"], arrs["k1"], arrs["b1"], h0)
    return y0, y1, h_fin


def _store_token_tiles(ref, x):
    rows, width = x.shape
    tiles = width // LANES
    for j in range(tiles):
        ref[pl.ds(j, rows, stride=tiles), :] = x[:, LANES * j:LANES * (j + 1)]


def _load_token_tiles(ref, lead, rows, tiles):
    idx = tuple(lead)
    return jnp.concatenate([ref[idx + (pl.ds(j, rows, stride=tiles), slice(None))] for j in range(tiles)],
                           axis=1)


def _outproj_kernel(x_ref, oa_ref, ob_ref, oc_ref, y0_ref, y1_ref, bon_ref, g_ref, ln_ref, b64_ref,
                    wo_ref, gate_ref, sh_ref, sc_ref, g2_ref, rwh_ref, rwl_ref, rb_ref,
                    x1_o, h2_o, lg_o, *, rows_per_group):
    i = pl.program_id(0)
    g = (i * ROW_TILE) // rows_per_group
    b64 = b64_ref[...]
    y = y0_ref[...] + y1_ref[...]
    mu = _mm_x2(y, b64) * (1.0 / RWKV_N)
    dy = y - mu
    var = _mm_x2(dy * dy, b64) * (1.0 / RWKV_N)
    ln = ln_ref[...]
    yn = dy * lax.rsqrt(var + RWKV_GN_EPS) * ln[0:1] + ln[1:2]
    od = (yn + bon_ref[...]) * g_ref[...]
    wo = wo_ref[...]
    mixed = (_mm(oa_ref[...], wo[0:256]) + _mm(ob_ref[...], wo[256:512])
             + _mm(oc_ref[...], wo[512:768]) + _mm(od, wo[768:1024]))
    x1 = x_ref[...] + gate_ref[pl.ds(g, 1), :] * mixed
    x1_o[...] = x1
    ms = jnp.mean(x1 * x1, axis=-1, keepdims=True)
    h2 = (x1 * lax.rsqrt(ms + EPS) * g2_ref[...]) * (1.0 + sc_ref[pl.ds(g, 1), :]) + sh_ref[pl.ds(g, 1), :]
    _store_token_tiles(h2_o, h2)
    hh, hl = _split2(h2)
    rwh, rwl = rwh_ref[...], rwl_ref[...]
    lg_o[...] = (jnp.dot(hh, rwh, preferred_element_type=F32) + jnp.dot(hh, rwl, preferred_element_type=F32)
                 + jnp.dot(hl, rwh, preferred_element_type=F32)) + rb_ref[...]


def _outproj(x, oa, ob, oc, y0, y1, bonus, gate, ln, b64, wo, mods, g2, rw_hi, rw_lo, rb,
             *, rows_per_group):
    n, d = x.shape
    full = lambda shape: pl.BlockSpec(shape, lambda i: (0,) * len(shape))
    row = lambda w: pl.BlockSpec((ROW_TILE, w), lambda i: (i, 0))
    mod = lambda j: pl.BlockSpec((SUBLANES, d), lambda i: (0, j))
    return pl.pallas_call(
        functools.partial(_outproj_kernel, rows_per_group=rows_per_group),
        out_shape=(jax.ShapeDtypeStruct((n, d), F32), jax.ShapeDtypeStruct((n * (d // LANES), LANES), F32),
                   jax.ShapeDtypeStruct((n, LANES), F32)),
        grid=(n // ROW_TILE,),
        in_specs=[row(d), row(256), row(256), row(256), row(256), row(256), row(256), row(256),
                  full(ln.shape), full(b64.shape), full(wo.shape), mod(2), mod(3), mod(4),
                  full(g2.shape), full(rw_hi.shape), full(rw_lo.shape), full(rb.shape)],
        out_specs=(row(d), pl.BlockSpec((ROW_TILE * (d // LANES), LANES), lambda i: (i, 0)), row(LANES)),
        compiler_params=_cparams(("arbitrary",)),
        name="outproj",
    )(x, oa, ob, oc, y0, y1, bonus, gate, ln, b64, wo, mods, mods, mods, g2, rw_hi, rw_lo, rb)


TOKEN_TILES = 8


def _token_rows(token, count=1):
    start = token * TOKEN_TILES
    if not isinstance(token, int):
        start = pl.multiple_of(start, TOKEN_TILES)
    return pl.ds(start, count * TOKEN_TILES)


def _row_copy(src_hbm, token, dst, dst_row, sem):
    return pltpu.make_async_copy(src_hbm.at[_token_rows(token), :], dst.at[_token_rows(dst_row), :], sem)


def _issue_rows(n_rows, start_row):
    def body(j, carry):
        for u in range(DMA_UNROLL):
            start_row(j * DMA_UNROLL + u)
        return carry
    lax.fori_loop(0, n_rows // DMA_UNROLL, body, 0)


def _moe_kernel(be_ref, nused_ref, cur_ref, nxt_ref, nxt2_ref, h_hbm, w1_ref, w2_ref, b1_ref, b2_ref,
                out_ref, xbuf, w1b, w2xb, sem):
    i = pl.program_id(0)
    nused = nused_ref[0]
    slot = i % MOE_SLOTS
    ahead = (i + MOE_SLOTS - 1) % MOE_SLOTS

    def issue(idx_ref, sl):
        _issue_rows(MOE_ROWS, lambda r: _row_copy(h_hbm, idx_ref[0, 0, r], xbuf.at[sl], r, sem.at[sl]).start())

    def wait_slot(sl):
        pltpu.make_async_copy(h_hbm.at[_token_rows(0, MOE_ROWS), :], xbuf.at[sl], sem.at[sl]).wait()

    @pl.when(i == 0)
    def _():
        issue(cur_ref, 0)
        issue(nxt_ref, 1)

    @pl.when(i < nused)
    def _():
        wait_slot(slot)

        @pl.when(jnp.logical_or(i == 0, be_ref[i] != be_ref[jnp.maximum(i - 1, 0)]))
        def _():
            w1b[...] = w1_ref[0].astype(BF16)
            half = LANES // 2
            for p in range(w2_ref.shape[1] // LANES):
                xb = pltpu.bitcast(w2_ref[0, LANES * p:LANES * p + half, :].astype(BF16).astype(F32), jnp.uint32)
                yb = pltpu.bitcast(w2_ref[0, LANES * p + half:LANES * (p + 1), :].astype(BF16).astype(F32),
                                   jnp.uint32)
                word = lax.shift_right_logical(xb, jnp.uint32(16)) | (yb & jnp.uint32(0xFFFF0000))
                w2xb[LANES * p:LANES * (p + 1), :] = pltpu.bitcast(word, BF16)

        x = _load_token_tiles(xbuf, (slot,), MOE_ROWS, TOKEN_TILES).astype(BF16)
        for r in range(MOE_ROWS):
            _row_copy(h_hbm, nxt2_ref[0, 0, r], xbuf.at[ahead], r, sem.at[ahead]).start()
        hcat = jnp.dot(x, w1b[...], preferred_element_type=F32) + b1_ref[0]
        glu = jnp.minimum(hcat, SWIGLU_LIMIT)
        lin1 = jnp.clip(hcat, -SWIGLU_LIMIT, SWIGLU_LIMIT) + 1.0
        width = hcat.shape[1]
        spread = glu * jax.nn.sigmoid(SWIGLU_ALPHA * glu) * pltpu.roll(lin1, width - 1, 1)
        even = (lax.broadcasted_iota(jnp.int32, (MOE_ROWS, LANES), 1) & 1) == 0
        act = jnp.concatenate(
            [jnp.where(even, spread[:, LANES * a:LANES * (a + 1)],
                       pltpu.roll(spread[:, LANES * (a + 1):LANES * (a + 2)], 1, 1))
             for a in range(0, width // LANES, 2)], axis=1)
        _store_token_tiles(out_ref, _mm(act, w2xb[...]) + b2_ref[0])

    @pl.when(i == nused - 1)
    def _():
        wait_slot((i + 1) % MOE_SLOTS)
        wait_slot(ahead)

    @pl.when(i >= nused)
    def _():
        out_ref[...] = jnp.zeros_like(out_ref)


def _moe_experts(h2, block_e, nused, buf_tok, w1, w2, b1, b2, layer):
    tiles = TOKEN_TILES
    d = tiles * LANES
    n_blocks = block_e.shape[0]
    dff2 = w1.shape[3]
    idx = buf_tok.reshape(n_blocks, 1, MOE_ROWS)
    wspec = lambda s1, s2: pl.BlockSpec((None, 1, s1, s2), lambda i, be, nu: (layer, be[i], 0, 0))
    grid_spec = pltpu.PrefetchScalarGridSpec(
        num_scalar_prefetch=2,
        grid=(n_blocks,),
        in_specs=[
            pl.BlockSpec((1, 1, MOE_ROWS), lambda i, be, nu: (i, 0, 0), memory_space=pltpu.SMEM),
            pl.BlockSpec((1, 1, MOE_ROWS), lambda i, be, nu: (jnp.minimum(i + 1, n_blocks - 1), 0, 0),
                         memory_space=pltpu.SMEM),
            pl.BlockSpec((1, 1, MOE_ROWS), lambda i, be, nu: (jnp.minimum(i + 2, n_blocks - 1), 0, 0),
                         memory_space=pltpu.SMEM),
            pl.BlockSpec(memory_space=pl.ANY),
            wspec(d, dff2), wspec(dff2 // 2, d), wspec(1, dff2), wspec(1, d),
        ],
        out_specs=pl.BlockSpec((MOE_ROWS * tiles, LANES), lambda i, be, nu: (i, 0)),
        scratch_shapes=[pltpu.VMEM((MOE_SLOTS, MOE_ROWS * tiles, LANES), F32), pltpu.VMEM((d, dff2), BF16),
                        pltpu.VMEM((dff2 // 2, d), BF16), pltpu.SemaphoreType.DMA((MOE_SLOTS,))],
    )
    return pl.pallas_call(
        _moe_kernel,
        out_shape=jax.ShapeDtypeStruct((n_blocks * MOE_ROWS * tiles, LANES), F32),
        grid_spec=grid_spec,
        compiler_params=_cparams(("arbitrary",), vmem=MOE_VMEM_LIMIT),
        name="moe_experts",
    )(block_e, nused, idx, idx, idx, h2, w1, w2, b1, b2)


def _comb_kernel(cur_ref, nxt_ref, yb_hbm, x_ref, gates_ref, gate_ref, out_ref, buf, sem,
                 *, rows_per_group, n_steps):
    i = pl.program_id(0)
    slot = i % 2
    g = (i * COMB_ROWS) // rows_per_group

    def issue(idx_ref, sl):
        def start_token(t):
            for kx in range(TOP_K):
                _row_copy(yb_hbm, idx_ref[0, 0, t * TOP_K + kx], buf.at[sl, kx], t, sem.at[sl]).start(
                    priority=kx % 2)
        _issue_rows(COMB_ROWS, start_token)

    @pl.when(i == 0)
    def _():
        issue(cur_ref, 0)

    @pl.when(i + 1 < n_steps)
    def _():
        issue(nxt_ref, 1 - slot)

    for kx in range(TOP_K):
        pltpu.make_async_copy(yb_hbm.at[_token_rows(0, COMB_ROWS), :], buf.at[slot, kx], sem.at[slot]).wait()
    gates = gates_ref[...]
    acc = gates[:, 0:1] * _load_token_tiles(buf, (slot, 0), COMB_ROWS, TOKEN_TILES)
    for kx in range(1, TOP_K):
        acc = acc + gates[:, kx:kx + 1] * _load_token_tiles(buf, (slot, kx), COMB_ROWS, TOKEN_TILES)
    out_ref[...] = x_ref[...] + gate_ref[pl.ds(g, 1), :] * acc


def _moe_combine(yb, x1, gates, slot_dest, mods, *, rows_per_group):
    n, d = x1.shape
    steps = n // COMB_ROWS
    idx = slot_dest.reshape(steps, 1, COMB_ROWS * TOP_K)
    return pl.pallas_call(
        functools.partial(_comb_kernel, rows_per_group=rows_per_group, n_steps=steps),
        out_shape=jax.ShapeDtypeStruct((n, d), F32),
        grid=(steps,),
        in_specs=[
            pl.BlockSpec((1, 1, COMB_ROWS * TOP_K), lambda i: (i, 0, 0), memory_space=pltpu.SMEM),
            pl.BlockSpec((1, 1, COMB_ROWS * TOP_K), lambda i: (jnp.minimum(i + 1, steps - 1), 0, 0),
                         memory_space=pltpu.SMEM),
            pl.BlockSpec(memory_space=pl.ANY),
            pl.BlockSpec((COMB_ROWS, d), lambda i: (i, 0)),
            pl.BlockSpec((COMB_ROWS, TOP_K), lambda i: (i, 0)),
            pl.BlockSpec((SUBLANES, d), lambda i: (0, 5)),
        ],
        out_specs=pl.BlockSpec((COMB_ROWS, d), lambda i: (i, 0)),
        scratch_shapes=[pltpu.VMEM((2, TOP_K, COMB_ROWS * TOKEN_TILES, LANES), F32),
                        pltpu.SemaphoreType.DMA((2,))],
        compiler_params=_cparams(("arbitrary",)),
        name="moe_combine",
    )(idx, idx, yb, x1, gates, mods)


def _route(logits):
    n = logits.shape[0]
    top_val, top_idx = lax.top_k(logits, TOP_K)
    gates = jax.nn.softmax(top_val, axis=-1)
    flat_e = top_idx.reshape(-1).astype(jnp.int32)
    nk = n * TOP_K
    order = jnp.argsort(flat_e).astype(jnp.int32)
    rank = jnp.argsort(order).astype(jnp.int32)
    experts = jnp.arange(N_EXPERTS, dtype=jnp.int32)
    counts = jnp.sum((flat_e[:, None] == experts[None, :]).astype(jnp.int32), axis=0)
    padded = (counts + MOE_ROWS - 1) // MOE_ROWS * MOE_ROWS
    start = jnp.cumsum(counts) - counts
    pad_end = jnp.cumsum(padded)
    pad_start = pad_end - padded
    n_blocks = (nk + N_EXPERTS * (MOE_ROWS - 1) + MOE_ROWS - 1) // MOE_ROWS
    cap = n_blocks * MOE_ROWS
    block_start = jnp.arange(n_blocks, dtype=jnp.int32) * MOE_ROWS
    block_e = jnp.minimum(jnp.sum((pad_end[None, :] <= block_start[:, None]).astype(jnp.int32), axis=1),
                          N_EXPERTS - 1)
    nused = (pad_end[-1] // MOE_ROWS).astype(jnp.int32).reshape(1)
    row = jnp.arange(cap, dtype=jnp.int32)
    row_e = jnp.repeat(block_e, MOE_ROWS)
    within = row - pad_start[row_e]
    valid = within < counts[row_e]
    src = jnp.clip(start[row_e] + within, 0, nk - 1)
    buf_tok = jnp.where(valid, order[src] // TOP_K, 0).astype(jnp.int32)
    slot_dest = (pad_start[flat_e] + rank - start[flat_e]).astype(jnp.int32)
    return gates, buf_tok, block_e, nused, slot_dest


def _block_ones(n, blk):
    idx = np.arange(n) // blk
    return jnp.asarray(idx[:, None] == idx[None, :], dtype=BF16)


def _pad_cols(w, width):
    return jnp.pad(w, ((0, 0), (0, width - w.shape[1])))


def _place_heads(w, per_head, offset, heads=MLA_HEADS, slot=MLA_PAD):
    rows = w.shape[0]
    w3 = w.reshape(rows, heads, per_head)
    out = jnp.zeros((rows, heads, slot), w.dtype).at[:, :, offset:offset + per_head].set(w3)
    return out.reshape(rows, heads * slot)


def _pack_in_weights(P, l):
    w_in = P["w_in"][l]
    sizes = (256, 128, 128, MLA_Q_LORA, MLA_KV_LORA, MLA_ROPE, 256, 256, 256, RWKV_COLS)
    offs = np.concatenate([[0], np.cumsum(sizes)])
    seg = [w_in[:, offs[j]:offs[j + 1]] for j in range(len(sizes))]
    krp = _place_heads(jnp.tile(seg[5], (1, MLA_HEADS)), MLA_ROPE, MLA_NOPE)
    w_all = jnp.concatenate([seg[0], seg[1], seg[2], _pad_cols(seg[3], 256), seg[4], krp,
                             seg[6], seg[7], seg[8], _pad_cols(seg[9], 1024)], axis=1).astype(BF16)
    q_up = _place_heads(P["mla_q_up"][l], MLA_QK, 0)
    q_up = jnp.pad(q_up, ((0, 256 - MLA_Q_LORA), (0, 0))).astype(BF16)
    kv_up = P["mla_kv_up"][l].reshape(MLA_KV_LORA, MLA_HEADS, MLA_NOPE + MLA_V)
    w_knope = _place_heads(kv_up[:, :, :MLA_NOPE].reshape(MLA_KV_LORA, -1), MLA_NOPE, 0).astype(BF16)
    w_v = kv_up[:, :, MLA_NOPE:].reshape(MLA_KV_LORA, MLA_HEADS * MLA_V).astype(BF16)
    width = MLA_HEADS * MLA_PAD

    def rowpad(v):
        return jnp.pad(v, (0, width - v.shape[0]))

    gains = jnp.stack([
        rowpad(jnp.tile(P["gqa_qn"][l], GQA_HEADS)),
        rowpad(jnp.tile(P["gqa_kn"][l], GQA_KV_HEADS)),
        rowpad(P["mla_qa_norm"][l]),
        rowpad(P["mla_kva_norm"][l]),
        _place_heads(jnp.tile(P["mla_qn"][l], MLA_HEADS)[None], MLA_QK, 0)[0],
        _place_heads(jnp.tile(P["mla_kn"][l], MLA_HEADS)[None], MLA_QK, 0)[0],
        rowpad(jnp.tile(P["diff_qn"][l], 2 * DIFF_HEADS)),
        rowpad(jnp.tile(P["diff_kn"][l], 2 * DIFF_HEADS)),
    ]).astype(F32)
    return dict(w_all=w_all, q_up=q_up, w_knope=w_knope, w_v=w_v, gains=gains,
                b64=_block_ones(256, 64), b32=_block_ones(256, 32), b128=_block_ones(width, MLA_PAD))


def _pack_rwkv(P, l):
    mix = _pad_cols(P["rwkv_mix"][l], 1024)
    w_lora = jnp.zeros((256, 1280), F32)
    w_lora = w_lora.at[0:32, 0:256].set(P["rwkv_w2"][l, 0]).at[32:64, 256:512].set(P["rwkv_w2"][l, 1])
    w_lora = w_lora.at[64:96, 512:768].set(P["rwkv_a2"][l, 0]).at[96:128, 768:1024].set(P["rwkv_a2"][l, 1])
    w_lora = w_lora.at[128:192, 1024:1280].set(P["rwkv_g2"][l]).astype(BF16)
    vec = jnp.zeros((SUBLANES, 512), F32)
    vec = vec.at[0].set(P["rwkv_w0"][l].reshape(-1)).at[1].set(P["rwkv_a0"][l].reshape(-1))
    vec = vec.at[2].set(jnp.concatenate([P["rwkv_kk"][l], P["rwkv_ka"][l]]))
    vec = vec.at[3].set(P["rwkv_rk"][l].reshape(-1))
    return dict(mix=mix, w_lora=w_lora, vec=vec, b64=_block_ones(256, 64))


def _rope_table(seq, rot, lanes_per_tile_group, lane_offset=0):
    rows = seq // GRID_W
    row = jnp.repeat(jnp.arange(rows, dtype=F32), GRID_W)
    col = (jnp.arange(rows * GRID_W) % GRID_W).astype(F32)
    n_freq = rot // 4
    inv_freq = ROPE_BASE ** (-jnp.arange(n_freq, dtype=F32) / n_freq)
    ang = jnp.stack([row[:, None] * inv_freq, col[:, None] * inv_freq], axis=1)
    cos = jnp.cos(ang)[:, :, None, :]
    sin = jnp.sin(ang)[:, :, None, :]
    cos_g = jnp.broadcast_to(cos, (seq, 2, 2, n_freq)).reshape(seq, rot)
    sin_g = (jnp.broadcast_to(sin, (seq, 2, 2, n_freq))
             * jnp.asarray([-1.0, 1.0], F32)[None, None, :, None]).reshape(seq, rot)
    cos_t = jnp.ones((seq, LANES), F32)
    sin_t = jnp.zeros((seq, LANES), F32)
    for start in range(lane_offset, LANES, lanes_per_tile_group):
        cos_t = cos_t.at[:, start:start + rot].set(cos_g)
        sin_t = sin_t.at[:, start:start + rot].set(sin_g)
    real = jnp.concatenate([cos_t, sin_t], axis=1)
    ident = jnp.concatenate([jnp.ones((seq, LANES), F32), jnp.zeros((seq, LANES), F32)], axis=1)
    return jnp.stack([real, ident])


def _to_heads(x, groups, seq, heads):
    d = x.shape[1] // heads
    return x.reshape(groups, seq, heads, d).transpose(0, 2, 1, 3).astype(BF16)


def _with_ones(v):
    return jnp.concatenate([v.astype(BF16), jnp.ones(v.shape, BF16)], axis=-1)


def kernel(x_prompt, x_sample, cache_gqa_k, cache_gqa_v, cache_mla_ckv, cache_mla_krope, cache_diff_k, cache_diff_v, state_rwkv, c, c_ctx, norm1_g, norm2_g, ada_w, ada_b, w_in, w_out, gqa_qn, gqa_kn, mla_qa_norm, mla_q_up, mla_kva_norm, mla_kv_up, mla_qn, mla_kn, diff_qn, diff_kn, diff_lam, diff_subln, rwkv_mix, rwkv_w0, rwkv_w2, rwkv_a0, rwkv_a2, rwkv_rk, rwkv_g2, rwkv_kk, rwkv_ka, rwkv_ln_g, rwkv_ln_b, router_w, router_b, moe_w1, moe_b1, moe_w2, moe_b2):
    P = dict(w_in=w_in, gqa_qn=gqa_qn, gqa_kn=gqa_kn, mla_qa_norm=mla_qa_norm, mla_q_up=mla_q_up,
             mla_kva_norm=mla_kva_norm, mla_kv_up=mla_kv_up, mla_qn=mla_qn, mla_kn=mla_kn,
             diff_qn=diff_qn, diff_kn=diff_kn, rwkv_mix=rwkv_mix, rwkv_w0=rwkv_w0, rwkv_w2=rwkv_w2,
             rwkv_a0=rwkv_a0, rwkv_a2=rwkv_a2, rwkv_rk=rwkv_rk, rwkv_g2=rwkv_g2, rwkv_kk=rwkv_kk,
             rwkv_ka=rwkv_ka)
    depth = norm1_g.shape[0]
    bc, tc, d = x_prompt.shape
    bl, tl, _ = x_sample.shape
    assert bc * tc == tl, "context tokens must fill exactly one latent-sequence group"
    assert tl % ROW_TILE == 0 and tc % ROW_TILE == 0 and tc % SCAN_CHUNK == 0
    n_lat = bl * tl
    n = n_lat + tl
    groups = bl + 1
    past = cache_gqa_k.shape[3]

    cond = jnp.concatenate([c, c_ctx[None], jnp.zeros((SUBLANES - groups, d), F32)], axis=0)
    mods_all = _adaln(cond, ada_w, ada_b)
    tables = dict(a=_rope_table(tl, HEAD_DIM, HEAD_DIM),
                  m=_rope_table(tl, MLA_ROPE, LANES, lane_offset=MLA_NOPE),
                  c=_rope_table(tl, DIFF_QK, DIFF_QK))
    x = jnp.concatenate([x_sample.reshape(n_lat, d), x_prompt.reshape(tl, d)], axis=0)
    ctx_rows = slice(n_lat, n)
    new = [[] for _ in range(7)]

    for l in range(depth):
        lam_init = 0.8 - 0.6 * math.exp(-0.3 * l)
        mods = mods_all[l]
        wpack = _pack_in_weights(P, l)
        rpack = _pack_rwkv(P, l)
        (qa_h, ka_h, va_h, qb_h, kb_h, vb_h, qc_h, kc_h, vc_h, ka, va, ckvn, kr, kc, vc, pd) = _inproj(
            x, mods, norm1_g[l][None], wpack, tables, rows_per_group=tl, n_lat=n_lat)

        new[0].append(ka[ctx_rows].reshape(bc, tc, GQA_KV_HEADS, HEAD_DIM).transpose(0, 2, 1, 3))
        new[1].append(va[ctx_rows].reshape(bc, tc, GQA_KV_HEADS, HEAD_DIM).transpose(0, 2, 1, 3))
        new[2].append(ckvn[ctx_rows].reshape(bc, tc, MLA_KV_LORA))
        new[3].append(kr[ctx_rows, MLA_NOPE:MLA_NOPE + MLA_ROPE].reshape(bc, tc, MLA_ROPE))
        new[4].append(kc[ctx_rows].reshape(bc, tc, DIFF_HEADS, 2, DIFF_QK).transpose(0, 2, 3, 1, 4))
        new[5].append(vc[ctx_rows].reshape(bc, tc, DIFF_HEADS, DIFF_V).transpose(0, 2, 1, 3))

        o_a = _attention(qa_h, ka_h, va_h, cache_gqa_k[:, l].astype(BF16), _with_ones(cache_gqa_v[:, l]),
                         n_req=bl, seq=tc, ctx_group=bl)
        krp_c = _place_heads(jnp.tile(cache_mla_krope[:, l].reshape(bl * past, MLA_ROPE), (1, MLA_HEADS)),
                             MLA_ROPE, MLA_NOPE)
        kb_c, vb_c = _mla_cache(cache_mla_ckv[:, l].reshape(bl * past, MLA_KV_LORA), krp_c, wpack)
        o_b = _attention(qb_h, kb_h, vb_h,
                         _to_heads(kb_c, bl, past, MLA_HEADS), _with_ones(_to_heads(vb_c, bl, past, MLA_HEADS)),
                         n_req=bl, seq=tc, ctx_group=bl)
        kc_c = cache_diff_k[:, l].transpose(0, 1, 3, 2, 4).reshape(bl, DIFF_HEADS, past, 2 * DIFF_QK)
        o_c = _attention(qc_h, kc_h, vc_h, kc_c.astype(BF16), _with_ones(cache_diff_v[:, l]),
                         n_req=bl, seq=tc, ctx_group=bl,
                         diff_params=(diff_lam[l], diff_subln[l][None]), lam_init=lam_init)

        names = ("r", "v", "kk", "e0", "e1", "k0", "k1", "b0", "b1", "g", "bonus")
        rw = dict(zip(names, _rwkv_prep(pd, rpack, n_lat=n_lat, lat_seq=tl, ctx_seq=tc)))
        streams = {k_: rw[k_] for k_ in names[:9]}
        h0_lat = jnp.swapaxes(state_rwkv[:, l], -1, -2)
        y0l, y1l, _ = _rwkv_scan(streams, h0_lat, first_seq=0, seq_len=tl)
        h0_ctx = jnp.zeros((bc, 2, RWKV_HEADS, RWKV_N, RWKV_N), F32)
        y0c, y1c, h_ctx = _rwkv_scan(streams, h0_ctx, first_seq=n_lat // tc, seq_len=tc)
        new[6].append(jnp.swapaxes(h_ctx, -1, -2))
        y0 = jnp.concatenate([y0l.reshape(n_lat, RWKV_W), y0c.reshape(tl, RWKV_W)], axis=0)
        y1 = jnp.concatenate([y1l.reshape(n_lat, RWKV_W), y1c.reshape(tl, RWKV_W)], axis=0)

        ln = jnp.stack([rwkv_ln_g[l], rwkv_ln_b[l]])
        rw_f = _pad_cols(router_w[l], LANES)
        rw_hi = rw_f.astype(BF16)
        rw_lo = (rw_f - rw_hi.astype(F32)).astype(BF16)
        rb = _pad_cols(router_b[l][None], LANES)
        x1, h2, logits = _outproj(x, o_a, o_b, o_c, y0, y1,
                                  rw["bonus"], rw["g"], ln, rpack["b64"], w_out[l].astype(BF16), mods,
                                  norm2_g[l][None], rw_hi, rw_lo, rb, rows_per_group=tl)

        gates, buf_tok, block_e, nused, slot_dest = _route(logits[:, :N_EXPERTS])
        yb = _moe_experts(h2, block_e, nused, buf_tok, moe_w1, moe_w2,
                          moe_b1[:, :, None, :], moe_b2[:, :, None, :], l)
        x = _moe_combine(yb, x1, gates, slot_dest, mods, rows_per_group=tl)

    y_sample = x[:n_lat].reshape(bl, tl, d)
    y_prompt = x[n_lat:].reshape(bc, tc, d)
    return (y_prompt, y_sample) + tuple(jnp.stack(t, axis=1) for t in new)
```

```python
import functools
import math

import jax
import jax.numpy as jnp
import numpy as np
from jax import lax
from jax.experimental import pallas as pl
from jax.experimental.pallas import tpu as pltpu

F32 = jnp.float32
BF16 = jnp.bfloat16

GRID_W = 64
ROPE_BASE = 10000.0
EPS = 1e-6
HEAD_DIM = 64
GQA_HEADS, GQA_KV_HEADS = 4, 2
MLA_HEADS, MLA_Q_LORA, MLA_KV_LORA, MLA_NOPE, MLA_ROPE, MLA_V = 4, 192, 128, 64, 32, 64
MLA_QK = MLA_NOPE + MLA_ROPE
MLA_PAD = 128
DIFF_HEADS, DIFF_QK, DIFF_V = 4, 32, 64
RWKV_HEADS, RWKV_N = 4, 64
RWKV_W = RWKV_HEADS * RWKV_N
DECAY_LORA, AAA_LORA, GATE_LORA = 32, 32, 64
RWKV_GN_EPS = 64e-5
RWKV_COLS = 3 * RWKV_W + 2 * DECAY_LORA + 2 * AAA_LORA + GATE_LORA
N_EXPERTS, TOP_K = 32, 4
SWIGLU_ALPHA, SWIGLU_LIMIT = 1.702, 7.0

LANES = 128
SUBLANES = 8
VMEM_LIMIT = 48 * 1024 * 1024
MOE_VMEM_LIMIT = 56 * 1024 * 1024
ROW_TILE = 256
Q_TILE = 256
KV_CHUNK = 512
SCAN_CHUNK = 64
SCAN_SUB = 16
SCAN_SEQS = 4
MOE_ROWS = 512
MOE_SLOTS = 3
COMB_ROWS = 128
ADA_COLS = 1536
DMA_UNROLL = 8


def _cparams(sem, vmem=VMEM_LIMIT):
    return pltpu.CompilerParams(dimension_semantics=sem, vmem_limit_bytes=vmem)


def _mm(a, b):
    return jnp.dot(a.astype(BF16), b.astype(BF16), preferred_element_type=F32)


def _mm_nt(a, b):
    return lax.dot_general(a.astype(BF16), b.astype(BF16), (((1,), (1,)), ((), ())),
                           preferred_element_type=F32)


def _split2(x):
    hi = x.astype(BF16)
    lo = (x - hi.astype(F32)).astype(BF16)
    return hi, lo


def _split3(x):
    hi = x.astype(BF16)
    r = x - hi.astype(F32)
    mid = r.astype(BF16)
    lo = (r - mid.astype(F32)).astype(BF16)
    return hi, mid, lo


def _mm_x2(x, ones_blk):
    hi, lo = _split2(x)
    return (jnp.dot(hi, ones_blk, preferred_element_type=F32)
            + jnp.dot(lo, ones_blk, preferred_element_type=F32))


def _mm3(a, b):
    ah, al = _split2(a)
    bh, bl = _split2(b)
    return (jnp.dot(ah, bh, preferred_element_type=F32)
            + jnp.dot(ah, bl, preferred_element_type=F32)
            + jnp.dot(al, bh, preferred_element_type=F32))


def _rope(x, cos, sin_signed, rot):
    n = x.shape[-1]
    q = rot // 4
    lane = lax.broadcasted_iota(jnp.int32, x.shape, 1)
    first = (lane & (rot // 2 - 1)) < q
    partner = jnp.where(first, pltpu.roll(x, n - q, 1), pltpu.roll(x, q, 1))
    return x * cos + partner * sin_signed


def _tile_lanes(x, reps):
    return x if reps == 1 else jnp.concatenate([x] * reps, axis=1)


def _ada_kernel(c_ref, w_ref, b_ref, o_ref):
    c = c_ref[...]
    o_ref[0] = _mm(c * jax.nn.sigmoid(c), w_ref[0]) + b_ref[0]


def _adaln(cond, ada_w, ada_b):
    depth, d, cols = ada_w.shape
    rows = cond.shape[0]
    return pl.pallas_call(
        _ada_kernel,
        out_shape=jax.ShapeDtypeStruct((depth, rows, cols), F32),
        grid=(depth, cols // ADA_COLS),
        in_specs=[
            pl.BlockSpec((rows, d), lambda l, j: (0, 0)),
            pl.BlockSpec((1, d, ADA_COLS), lambda l, j: (l, 0, j)),
            pl.BlockSpec((1, 1, ADA_COLS), lambda l, j: (l, 0, j)),
        ],
        out_specs=pl.BlockSpec((1, rows, ADA_COLS), lambda l, j: (l, 0, j)),
        compiler_params=_cparams(("arbitrary", "arbitrary")),
        name="adaln",
    )(cond, ada_w, ada_b.reshape(depth, 1, cols))


_C_AQ, _C_AK, _C_AV = 0, 256, 384
_C_CQ, _C_CKV, _C_KRP = 512, 768, 896
_C_DQ, _C_DK, _C_DV = 1408, 1664, 1920
_C_PD = 2176
_C_END = 3200


def _mla_kv(ckvn, krp, w_knope, w_v, kn_gain, blk128, cos, sin):
    kpre = _mm(ckvn, w_knope) + krp
    ss = _mm_x2(kpre * kpre, blk128) * (1.0 / MLA_QK)
    kb = kpre * lax.rsqrt(ss + EPS) * kn_gain
    kb = _rope(kb, cos, sin, MLA_ROPE)
    return kb, _mm(ckvn, w_v)


def _store_heads(o_ref, x, width, ones=False):
    for h in range(o_ref.shape[1]):
        piece = x[:, width * h:width * (h + 1)]
        if ones:
            piece = jnp.concatenate([piece, jnp.ones_like(piece)], axis=1)
        o_ref[0, h] = piece.astype(o_ref.dtype)


def _inproj_kernel(x_ref, sh_ref, sc_ref, g1_ref, w_ref, qup_ref, wkn_ref, wv_ref, gains_ref,
                   b64_ref, b32_ref, b128_ref, ra_ref, rm_ref, rc_ref,
                   qa_h, ka_h, va_h, qb_h, kb_h, vb_h, qc_h, kc_h, vc_h,
                   ka_o, va_o, ckv_o, kr_o, kc_o, vc_o, pd_o,
                   *, rows_per_group):
    i = pl.program_id(0)
    g = (i * ROW_TILE) // rows_per_group
    x = x_ref[...]
    ms = jnp.mean(x * x, axis=-1, keepdims=True)
    xn = x * lax.rsqrt(ms + EPS) * g1_ref[...]
    h = xn * (1.0 + sc_ref[pl.ds(g, 1), :]) + sh_ref[pl.ds(g, 1), :]
    p = _mm(h, w_ref[...])
    gains = gains_ref[...]
    b64, b32, b128 = b64_ref[...], b32_ref[...], b128_ref[...]

    ra = ra_ref[0]
    cos_a, sin_a = ra[:, :LANES], ra[:, LANES:]
    aq = p[:, _C_AQ:_C_AK]
    ssq = _mm_x2(aq * aq, b64) * (1.0 / HEAD_DIM)
    qa = aq * lax.rsqrt(ssq + EPS) * gains[0:1, :256]
    qa = _rope(qa, _tile_lanes(cos_a, 2), _tile_lanes(sin_a, 2), HEAD_DIM)
    _store_heads(qa_h, qa * (HEAD_DIM ** -0.5), HEAD_DIM)
    ak = p[:, _C_AK:_C_AV]
    ssk = _mm_x2(ak * ak, b64[:LANES, :LANES]) * (1.0 / HEAD_DIM)
    ka = ak * lax.rsqrt(ssk + EPS) * gains[1:2, :128]
    ka = _rope(ka, cos_a, sin_a, HEAD_DIM)
    ka_o[...] = ka
    _store_heads(ka_h, ka, HEAD_DIM)
    va = p[:, _C_AV:_C_CQ]
    va_o[...] = va
    _store_heads(va_h, va, HEAD_DIM, ones=True)

    rm = rm_ref[0]
    cos_m, sin_m = _tile_lanes(rm[:, :LANES], MLA_HEADS), _tile_lanes(rm[:, LANES:], MLA_HEADS)
    cq = p[:, _C_CQ:_C_CKV]
    cqn = cq * lax.rsqrt(jnp.sum(cq * cq, axis=-1, keepdims=True) * (1.0 / MLA_Q_LORA) + EPS)
    qb = _mm(cqn * gains[2:3, :256], qup_ref[...])
    ssb = _mm_x2(qb * qb, b128) * (1.0 / MLA_QK)
    qb = qb * lax.rsqrt(ssb + EPS) * gains[4:5, :]
    _store_heads(qb_h, _rope(qb, cos_m, sin_m, MLA_ROPE) * (MLA_QK ** -0.5), MLA_PAD)
    ckv = p[:, _C_CKV:_C_KRP]
    ckvn = ckv * lax.rsqrt(jnp.mean(ckv * ckv, axis=-1, keepdims=True) + EPS) * gains[3:4, :128]
    ckv_o[...] = ckvn
    krp = p[:, _C_KRP:_C_DQ]
    kr_o[...] = krp[:, :LANES]
    kb, vb = _mla_kv(ckvn, krp, wkn_ref[...], wv_ref[...], gains[5:6, :], b128, cos_m, sin_m)
    _store_heads(kb_h, kb, MLA_PAD)
    _store_heads(vb_h, vb, MLA_V, ones=True)

    rc = rc_ref[0]
    cos_c, sin_c = _tile_lanes(rc[:, :LANES], 2), _tile_lanes(rc[:, LANES:], 2)
    dq = p[:, _C_DQ:_C_DK]
    ssd = _mm_x2(dq * dq, b32) * (1.0 / DIFF_QK)
    qc = dq * lax.rsqrt(ssd + EPS) * gains[6:7, :256]
    _store_heads(qc_h, _rope(qc, cos_c, sin_c, DIFF_QK) * (DIFF_QK ** -0.5), 2 * DIFF_QK)
    dk = p[:, _C_DK:_C_DV]
    ssd = _mm_x2(dk * dk, b32) * (1.0 / DIFF_QK)
    kc = dk * lax.rsqrt(ssd + EPS) * gains[7:8, :256]
    kc = _rope(kc, cos_c, sin_c, DIFF_QK)
    kc_o[...] = kc
    _store_heads(kc_h, kc, 2 * DIFF_QK)
    vc = p[:, _C_DV:_C_PD]
    vc_o[...] = vc
    _store_heads(vc_h, vc, DIFF_V, ones=True)

    pd_o[...] = p[:, _C_PD:_C_END]


def _inproj(x, mods, g1, wpack, tables, *, rows_per_group, n_lat):
    n, d = x.shape
    steps = n // ROW_TILE
    lat_steps = n_lat // ROW_TILE
    pos_steps = rows_per_group // ROW_TILE

    def rope_map(i):
        is_ctx = i >= lat_steps
        return (jnp.where(is_ctx, 1, 0), jnp.where(is_ctx, 0, i % pos_steps), 0)

    full = lambda shape: pl.BlockSpec(shape, lambda i: (0,) * len(shape))
    row = lambda w: pl.BlockSpec((ROW_TILE, w), lambda i: (i, 0))
    groups = n // rows_per_group
    head_shapes = ((GQA_HEADS, HEAD_DIM), (GQA_KV_HEADS, HEAD_DIM), (GQA_KV_HEADS, 2 * HEAD_DIM),
                   (MLA_HEADS, MLA_PAD), (MLA_HEADS, MLA_PAD), (MLA_HEADS, 2 * MLA_V),
                   (DIFF_HEADS, 2 * DIFF_QK), (DIFF_HEADS, 2 * DIFF_QK), (DIFF_HEADS, 2 * DIFF_V))
    heads = lambda hh, w: pl.BlockSpec((1, hh, ROW_TILE, w), lambda i: (i // pos_steps, 0, i % pos_steps, 0))
    widths = (128, 128, 128, 128, 256, 256, 1024)
    return pl.pallas_call(
        functools.partial(_inproj_kernel, rows_per_group=rows_per_group),
        out_shape=(tuple(jax.ShapeDtypeStruct((groups, hh, rows_per_group, w), BF16) for hh, w in head_shapes)
                   + tuple(jax.ShapeDtypeStruct((n, w), F32) for w in widths)),
        grid=(steps,),
        in_specs=[
            row(d),
            pl.BlockSpec((SUBLANES, d), lambda i: (0, 0)),
            pl.BlockSpec((SUBLANES, d), lambda i: (0, 1)),
            full((1, d)),
            full(wpack["w_all"].shape), full(wpack["q_up"].shape), full(wpack["w_knope"].shape),
            full(wpack["w_v"].shape), full(wpack["gains"].shape),
            full(wpack["b64"].shape), full(wpack["b32"].shape), full(wpack["b128"].shape),
            pl.BlockSpec((1, ROW_TILE, 2 * LANES), rope_map),
            pl.BlockSpec((1, ROW_TILE, 2 * LANES), rope_map),
            pl.BlockSpec((1, ROW_TILE, 2 * LANES), rope_map),
        ],
        out_specs=tuple(heads(hh, w) for hh, w in head_shapes) + tuple(row(w) for w in widths),
        compiler_params=_cparams(("arbitrary",)),
        name="inproj",
    )(x, mods, mods, g1, wpack["w_all"], wpack["q_up"], wpack["w_knope"], wpack["w_v"],
      wpack["gains"], wpack["b64"], wpack["b32"], wpack["b128"],
      tables["a"], tables["m"], tables["c"])


def _mla_cache_kernel(ckv_ref, krp_ref, wkn_ref, wv_ref, gains_ref, b128_ref, kb_o, vb_o):
    kn = gains_ref[...][5:6, :]
    one = jnp.ones((1, MLA_HEADS * MLA_PAD), F32)
    kb, vb = _mla_kv(ckv_ref[...], krp_ref[...], wkn_ref[...], wv_ref[...], kn, b128_ref[...],
                     one, jnp.zeros_like(one))
    kb_o[...] = kb
    vb_o[...] = vb


def _mla_cache(ckv, krp, wpack):
    rows = ckv.shape[0]
    tile = min(rows, ROW_TILE)
    full = lambda shape: pl.BlockSpec(shape, lambda i: (0,) * len(shape))
    row = lambda w: pl.BlockSpec((tile, w), lambda i: (i, 0))
    kw, vw = MLA_HEADS * MLA_PAD, MLA_HEADS * MLA_V
    return pl.pallas_call(
        _mla_cache_kernel,
        out_shape=(jax.ShapeDtypeStruct((rows, kw), F32), jax.ShapeDtypeStruct((rows, vw), F32)),
        grid=(rows // tile,),
        in_specs=[row(MLA_KV_LORA), row(kw), full(wpack["w_knope"].shape), full(wpack["w_v"].shape),
                  full(wpack["gains"].shape), full(wpack["b128"].shape)],
        out_specs=(row(kw), row(vw)),
        compiler_params=_cparams(("arbitrary",)),
        name="mla_cache",
    )(ckv, krp, wpack["w_knope"], wpack["w_v"], wpack["gains"], wpack["b128"])


def _softmax_streams(qs, kv_heads, segs, s_scr):
    pieces = []
    off = 0
    for k_ref, v_ref in segs:
        n_keys = k_ref.shape[2]
        step = min(KV_CHUNK, n_keys)
        for c0 in range(0, n_keys, step):
            pieces.append((k_ref, v_ref, c0, step, off + c0))
        off += n_keys
    row_max = []
    for j, q in enumerate(qs):
        m = None
        for k_ref, _, c0, step, col in pieces:
            s = _mm_nt(q, k_ref[0, kv_heads[j], c0:c0 + step, :])
            s_scr[j, :, col:col + step] = s
            for t0 in range(0, step, LANES):
                part = s[:, t0:t0 + LANES]
                m = part if m is None else jnp.maximum(m, part)
        row_max.append(jnp.max(m, axis=-1, keepdims=True))
    outs = []
    for j in range(len(qs)):
        acc = None
        for _, v_ref, c0, step, col in pieces:
            p = jnp.exp((s_scr[j, :, col:col + step] - row_max[j]).astype(BF16))
            pv = jnp.dot(p, v_ref[0, kv_heads[j], c0:c0 + step, :], preferred_element_type=F32)
            acc = pv if acc is None else acc + pv
        den = pltpu.roll(acc, HEAD_DIM, 1)
        outs.append((acc / den)[:, :HEAD_DIM])
    return outs


def _attn_kernel(*refs, has_cache, diff, lam_init, kv_heads):
    refs = list(refs)
    if diff:
        lam_ref, sub_ref = refs.pop(0), refs.pop(0)
    q_ref, k_ref, v_ref = refs[:3]
    o_ref, s_scr = refs[-2], refs[-1]
    segs = [(k_ref, v_ref)]
    if has_cache:
        segs.append((refs[3], refs[4]))
    if not diff:
        outs = _softmax_streams([q_ref[0, j] for j in range(q_ref.shape[1])], kv_heads, segs, s_scr)
        o_ref[...] = jnp.concatenate(outs, axis=1).astype(o_ref.dtype)
        return
    qs = []
    for j in range(q_ref.shape[1]):
        q = q_ref[0, j]
        lane = lax.broadcasted_iota(jnp.int32, q.shape, 1)
        zero = jnp.zeros_like(q)
        qs += [jnp.where(lane < DIFF_QK, q, zero), jnp.where(lane >= DIFF_QK, q, zero)]
    outs = _softmax_streams(qs, kv_heads, segs, s_scr)
    lv = lam_ref[...]
    lam = (jnp.exp(jnp.sum(lv[0:1] * lv[1:2], axis=-1, keepdims=True))
           - jnp.exp(jnp.sum(lv[2:3] * lv[3:4], axis=-1, keepdims=True)) + lam_init)
    heads = []
    for j in range(q_ref.shape[1]):
        o = outs[2 * j] - lam * outs[2 * j + 1]
        o = o * lax.rsqrt(jnp.mean(o * o, axis=-1, keepdims=True) + EPS) * sub_ref[...]
        heads.append(o * (1.0 - lam_init))
    o_ref[...] = jnp.concatenate(heads, axis=1).astype(o_ref.dtype)


def _attention(q, k, v, kc, vc, *, n_req, seq, ctx_group, diff_params=None, lam_init=0.0):
    g_all, hq, s_all, dq = q.shape
    hk, dk, dvp = k.shape[1], k.shape[3], v.shape[3]
    diff = diff_params is not None
    qh = 2
    if diff:
        kh, kv_heads = 2, (0, 0, 1, 1)
    elif hq == 2 * hk:
        kh, kv_heads = 1, (0, 0)
    else:
        kh, kv_heads = 2, (0, 1)
    n_streams = len(kv_heads)
    steps_h = hq // qh
    out_w = qh * HEAD_DIM
    q_tiles = s_all // Q_TILE
    extra_in, extra_specs2, extra_specs3 = [], [], []
    if diff:
        extra_in = [diff_params[0], diff_params[1]]
        extra_specs3 = [pl.BlockSpec(diff_params[0].shape, lambda b, h, i: (0, 0)),
                        pl.BlockSpec(diff_params[1].shape, lambda b, h, i: (0, 0))]
        extra_specs2 = [pl.BlockSpec(diff_params[0].shape, lambda b, h: (0, 0)),
                        pl.BlockSpec(diff_params[1].shape, lambda b, h: (0, 0))]
    past = kc.shape[2]
    body = functools.partial(_attn_kernel, diff=diff, lam_init=lam_init, kv_heads=kv_heads)
    lat = pl.pallas_call(
        functools.partial(body, has_cache=True),
        out_shape=jax.ShapeDtypeStruct((n_req * s_all, hq * HEAD_DIM), BF16),
        grid=(n_req, steps_h, q_tiles),
        in_specs=extra_specs3 + [
            pl.BlockSpec((1, qh, Q_TILE, dq), lambda b, h, i: (b, h, i, 0)),
            pl.BlockSpec((1, kh, s_all, dk), lambda b, h, i: (b, h, 0, 0)),
            pl.BlockSpec((1, kh, s_all, dvp), lambda b, h, i: (b, h, 0, 0)),
            pl.BlockSpec((1, kh, past, dk), lambda b, h, i: (b, h, 0, 0)),
            pl.BlockSpec((1, kh, past, dvp), lambda b, h, i: (b, h, 0, 0)),
        ],
        out_specs=pl.BlockSpec((Q_TILE, out_w), lambda b, h, i: (b * q_tiles + i, h)),
        scratch_shapes=[pltpu.VMEM((n_streams, Q_TILE, s_all + past), F32)],
        compiler_params=_cparams(("arbitrary", "arbitrary", "arbitrary")),
        name="attn_latent",
    )(*extra_in, q, k, v, kc, vc)
    n_seq = s_all // seq
    ctx = pl.pallas_call(
        functools.partial(body, has_cache=False),
        out_shape=jax.ShapeDtypeStruct((s_all, hq * HEAD_DIM), BF16),
        grid=(n_seq, steps_h),
        in_specs=extra_specs2 + [
            pl.BlockSpec((1, qh, seq, dq), lambda s, h: (ctx_group, h, s, 0)),
            pl.BlockSpec((1, kh, seq, dk), lambda s, h: (ctx_group, h, s, 0)),
            pl.BlockSpec((1, kh, seq, dvp), lambda s, h: (ctx_group, h, s, 0)),
        ],
        out_specs=pl.BlockSpec((seq, out_w), lambda s, h: (s, h)),
        scratch_shapes=[pltpu.VMEM((n_streams, seq, seq), F32)],
        compiler_params=_cparams(("arbitrary", "arbitrary")),
        name="attn_context",
    )(*extra_in, q, k, v)
    return jnp.concatenate([lat, ctx], axis=0)


def _rprep_kernel(p_ref, pp_ref, pn_ref, mix_ref, wl_ref, vec_ref, b64_ref,
                  r_o, v_o, kk_o, e0_o, e1_o, k0_o, k1_o, b0_o, b1_o, g_o, bon_o,
                  *, lat_steps, lat_seq_steps, ctx_seq_steps):
    i = pl.program_id(0)
    is_lat = i < lat_steps
    pos = jnp.where(is_lat, i % lat_seq_steps, (i - lat_steps) % ctx_seq_steps)
    last = jnp.where(is_lat, lat_seq_steps - 1, ctx_seq_steps - 1)
    at_start = pos == 0
    at_end = pos == last
    p = p_ref[...]
    rows = lax.broadcasted_iota(jnp.int32, p.shape, 0)
    prev_edge = jnp.where(at_start, 0.0, pp_ref[SUBLANES - 1:SUBLANES, :])
    next_edge = jnp.where(at_end, 0.0, pn_ref[0:1, :])
    prev = jnp.where(rows == 0, prev_edge, pltpu.roll(p, 1, 0))
    nxt = jnp.where(rows == ROW_TILE - 1, next_edge, pltpu.roll(p, ROW_TILE - 1, 0))
    mix = mix_ref[...]
    xs = p + (prev - p) * mix[0:1] + (nxt - p) * mix[1:2]
    r, k, v = xs[:, 0:256], xs[:, 256:512], xs[:, 512:768]
    lo = xs[:, 768:1024]
    lane = lax.broadcasted_iota(jnp.int32, lo.shape, 1)
    act = jnp.where(lane < 2 * DECAY_LORA, jnp.tanh(lo),
                    jnp.where(lane < 2 * (DECAY_LORA + AAA_LORA), lo, jax.nn.sigmoid(lo)))
    lora = _mm(act, wl_ref[...])
    vec = vec_ref[...]
    z = -(vec[0:1, :] + lora[:, 0:512])
    softplus = jnp.maximum(z, 0.0) + jnp.log1p(jnp.exp(-jnp.abs(z)))
    e = jnp.exp(-softplus - 0.5)
    a = jax.nn.sigmoid(vec[1:2, :] + lora[:, 512:1024])
    g_o[...] = lora[:, 1024:1280]
    b64 = b64_ref[...]
    kk = k * vec[2:3, 0:256]
    kk = kk * lax.rsqrt(_mm_x2(kk * kk, b64) + 1e-12)
    ka = vec[2:3, 256:512]
    r_o[...] = r
    v_o[...] = v
    kk_o[...] = kk
    bonus = None
    for d, (e_o, k_o, b_o) in enumerate(((e0_o, k0_o, b0_o), (e1_o, k1_o, b1_o))):
        a_d = a[:, 256 * d:256 * (d + 1)]
        k_d = k * (1.0 + (a_d - 1.0) * ka)
        e_o[...] = e[:, 256 * d:256 * (d + 1)]
        k_o[...] = k_d
        b_o[...] = kk * a_d
        term = _mm_x2(r * k_d * vec[3:4, 256 * d:256 * (d + 1)], b64) * v
        bonus = term if bonus is None else bonus + term
    bon_o[...] = bonus


def _rwkv_prep(pd, rpack, *, n_lat, lat_seq, ctx_seq):
    n, w = pd.shape
    steps = n // ROW_TILE
    halo_blocks = n // SUBLANES
    per = ROW_TILE // SUBLANES
    full = lambda shape: pl.BlockSpec(shape, lambda i: (0,) * len(shape))
    row = lambda width: pl.BlockSpec((ROW_TILE, width), lambda i: (i, 0))
    return pl.pallas_call(
        functools.partial(_rprep_kernel, lat_steps=n_lat // ROW_TILE,
                          lat_seq_steps=lat_seq // ROW_TILE, ctx_seq_steps=ctx_seq // ROW_TILE),
        out_shape=tuple(jax.ShapeDtypeStruct((n, RWKV_W), F32) for _ in range(11)),
        grid=(steps,),
        in_specs=[
            row(w),
            pl.BlockSpec((SUBLANES, w), lambda i: (jnp.maximum(i * per - 1, 0), 0)),
            pl.BlockSpec((SUBLANES, w), lambda i: (jnp.minimum((i + 1) * per, halo_blocks - 1), 0)),
            full(rpack["mix"].shape), full(rpack["w_lora"].shape), full(rpack["vec"].shape),
            full(rpack["b64"].shape),
        ],
        out_specs=tuple(row(RWKV_W) for _ in range(11)),
        compiler_params=_cparams(("arbitrary",)),
        name="rwkv_prep",
    )(pd, pd, pd, rpack["mix"], rpack["w_lora"], rpack["vec"], rpack["b64"])


def _bmm(a, b):
    return lax.dot_general(a.astype(BF16), b.astype(BF16), (((2,), (1,)), ((0,), (0,))),
                           preferred_element_type=F32)


def _bmm_nt(a, b):
    return lax.dot_general(a.astype(BF16), b.astype(BF16), (((2,), (2,)), ((0,), (0,))),
                           preferred_element_type=F32)


def _bmm3(a, b):
    ah, al = _split2(a)
    bh, bl = _split2(b)
    return _bmm(ah, bh) + _bmm(ah, bl) + _bmm(al, bh)


def _btranspose(x):
    return jnp.stack([x[i].T for i in range(x.shape[0])])


def _scan_chunks(r, v, kk, e, k, b, h0, rev):
    n, c, _ = r.shape
    ti = lax.broadcasted_iota(jnp.int32, (n, c, c), 1)
    si = lax.broadcasted_iota(jnp.int32, (n, c, c), 2)
    order = (si - ti) * jnp.where(rev, -1, 1)
    incl = order <= 0
    strict = order < 0
    e_hi, e_mid, e_lo = _split3(e)
    incl_b = jnp.where(incl, 1.0, 0.0).astype(BF16)
    cs = _bmm(incl_b, e_hi) + _bmm(incl_b, e_mid) + _bmm(incl_b, e_lo)
    g_prev = jnp.exp(e - cs)
    g_incl = jnp.exp(-cs)
    g_inv = jnp.exp(cs)
    kkg, rg, bq, kq = kk * g_prev, r * g_incl, b * g_inv, k * g_inv
    left = jnp.concatenate([kkg, rg], axis=1)
    sb = _bmm_nt(left, bq)
    sk = _bmm_nt(left, kq)
    zero = jnp.zeros((n, c, c), F32)
    a_m = jnp.where(strict, sb[:, :c], zero)
    b_m = jnp.where(strict, sk[:, :c], zero)
    a_p = jnp.where(incl, sb[:, c:], zero)
    b_p = jnp.where(incl, sk[:, c:], zero)

    eye = (ti == si).astype(F32)
    same = (ti // SCAN_SUB) == (si // SCAN_SUB)
    a_d = jnp.where(same, a_m, zero)
    a_off = a_m - a_d
    t_d = eye - a_d
    pw = _bmm3(a_d, a_d)
    steps = int(math.log2(SCAN_SUB))
    for j in range(1, steps):
        t_d = t_d + _bmm3(t_d, pw)
        if j + 1 < steps:
            pw = _bmm3(pw, pw)
    nil = _bmm3(t_d, a_off)
    inm = eye - nil
    nblk = c // SCAN_SUB
    if nblk > 2:
        nil2 = _bmm3(nil, nil)
        inm = inm + _bmm3(inm, nil2)
    t_full = _bmm3(inm, t_d) if nblk > 1 else t_d

    bmv = _bmm(b_m, v)
    w1 = _bmm(t_full, kkg)
    z = _bmm(t_full, bmv)
    cs_t = _btranspose(cs)
    tot_col = jnp.maximum(cs_t[:, :, 0:1], cs_t[:, :, c - 1:c])
    g_end_t = jnp.exp(cs_t - tot_col)
    kt_t = _btranspose(k) * g_end_t
    bt_t = _btranspose(b) * g_end_t
    u = _bmm(w1, h0) + z
    h_new = jnp.exp(-tot_col) * h0 + _bmm(kt_t, v) - _bmm(bt_t, u)
    y = _bmm(rg, h0) - _bmm(a_p, u) + _bmm(b_p, v)
    return y, h_new


def _scan_kernel(rf, vf, kkf, ef, kf, bf, rb, vb, kkb, eb, kb, bb, h0_ref,
                 y0_ref, y1_ref, hout_ref, h_scr, *, n_seq, n_chunks):
    c = pl.program_id(1)

    @pl.when(c == 0)
    def _():
        h_scr[...] = h0_ref[...]

    groups = ((rf, vf, kkf, ef, kf, bf), (rb, vb, kkb, eb, kb, bb))
    loaded = [[ref[...] for ref in refs] for refs in groups]
    ops = []
    for j in range(6):
        ops.append(jnp.stack([loaded[d][j][s, :, RWKV_N * h:RWKV_N * (h + 1)]
                              for s in range(n_seq) for d in range(2) for h in range(RWKV_HEADS)]))
    n = n_seq * 2 * RWKV_HEADS
    chain = lax.broadcasted_iota(jnp.int32, (n, 1, 1), 0)
    rev = ((chain // RWKV_HEADS) % 2) == 1
    h0 = h_scr[...].reshape(n, RWKV_N, RWKV_N)
    y, h_new = _scan_chunks(*ops, h0, rev)
    h_scr[...] = h_new.reshape(h_scr.shape)
    for s in range(n_seq):
        for d, y_ in enumerate((y0_ref, y1_ref)):
            base = (s * 2 + d) * RWKV_HEADS
            y_[s] = jnp.concatenate([y[base + h] for h in range(RWKV_HEADS)], axis=1)

    @pl.when(c == n_chunks - 1)
    def _():
        hout_ref[...] = h_scr[...]


def _rwkv_scan(streams, h0, *, first_seq, seq_len):
    n_chunks = seq_len // SCAN_CHUNK
    total = h0.shape[0]
    sb = min(SCAN_SEQS, total)
    assert total % sb == 0 and first_seq % sb == 0
    arrs = {name: a.reshape(-1, seq_len, RWKV_W) for name, a in streams.items()}
    blk0 = first_seq // sb
    blk = (sb, SCAN_CHUNK, RWKV_W)
    fwd = pl.BlockSpec(blk, lambda s, c: (blk0 + s, c, 0))
    bwd = pl.BlockSpec(blk, lambda s, c: (blk0 + s, n_chunks - 1 - c, 0))
    st_shape = (sb,) + h0.shape[1:]
    st = pl.BlockSpec(st_shape, lambda s, c: (s, 0, 0, 0, 0))
    y_shape = jax.ShapeDtypeStruct((total, seq_len, RWKV_W), F32)
    y0, y1, h_fin = pl.pallas_call(
        functools.partial(_scan_kernel, n_seq=sb, n_chunks=n_chunks),
        out_shape=(y_shape, y_shape, jax.ShapeDtypeStruct(h0.shape, F32)),
        grid=(total // sb, n_chunks),
        in_specs=[fwd] * 6 + [bwd] * 6 + [st],
        out_specs=(pl.BlockSpec(blk, lambda s, c: (s, c, 0)),
                   pl.BlockSpec(blk, lambda s, c: (s, n_chunks - 1 - c, 0)),
                   st),
        scratch_shapes=[pltpu.VMEM(st_shape, F32)],
        compiler_params=_cparams(("arbitrary", "arbitrary")),
        name="rwkv_scan",
    )(arrs["r"], arrs["v"], arrs["kk"], arrs["e0"], arrs["k0"], arrs["b0"],
      arrs["r"], arrs["v"], arrs["kk"], arrs["e1"], arrs["k1"], arrs["b1"], h0)
    return y0, y1, h_fin


def _store_token_tiles(ref, x):
    rows, width = x.shape
    tiles = width // LANES
    for j in range(tiles):
        ref[pl.ds(j, rows, stride=tiles), :] = x[:, LANES * j:LANES * (j + 1)]


def _load_token_tiles(ref, lead, rows, tiles):
    idx = tuple(lead)
    return jnp.concatenate([ref[idx + (pl.ds(j, rows, stride=tiles), slice(None))] for j in range(tiles)],
                           axis=1)


def _outproj_kernel(x_ref, oa_ref, ob_ref, oc_ref, y0_ref, y1_ref, bon_ref, g_ref, ln_ref, b64_ref,
                    wo_ref, gate_ref, sh_ref, sc_ref, g2_ref, rwh_ref, rwl_ref, rb_ref,
                    x1_o, h2_o, lg_o, *, rows_per_group):
    i = pl.program_id(0)
    g = (i * ROW_TILE) // rows_per_group
    b64 = b64_ref[...]
    y = y0_ref[...] + y1_ref[...]
    mu = _mm_x2(y, b64) * (1.0 / RWKV_N)
    dy = y - mu
    var = _mm_x2(dy * dy, b64) * (1.0 / RWKV_N)
    ln = ln_ref[...]
    yn = dy * lax.rsqrt(var + RWKV_GN_EPS) * ln[0:1] + ln[1:2]
    od = (yn + bon_ref[...]) * g_ref[...]
    wo = wo_ref[...]
    mixed = (_mm(oa_ref[...], wo[0:256]) + _mm(ob_ref[...], wo[256:512])
             + _mm(oc_ref[...], wo[512:768]) + _mm(od, wo[768:1024]))
    x1 = x_ref[...] + gate_ref[pl.ds(g, 1), :] * mixed
    x1_o[...] = x1
    ms = jnp.mean(x1 * x1, axis=-1, keepdims=True)
    h2 = (x1 * lax.rsqrt(ms + EPS) * g2_ref[...]) * (1.0 + sc_ref[pl.ds(g, 1), :]) + sh_ref[pl.ds(g, 1), :]
    _store_token_tiles(h2_o, h2)
    hh, hl = _split2(h2)
    rwh, rwl = rwh_ref[...], rwl_ref[...]
    lg_o[...] = (jnp.dot(hh, rwh, preferred_element_type=F32) + jnp.dot(hh, rwl, preferred_element_type=F32)
                 + jnp.dot(hl, rwh, preferred_element_type=F32)) + rb_ref[...]


def _outproj(x, oa, ob, oc, y0, y1, bonus, gate, ln, b64, wo, mods, g2, rw_hi, rw_lo, rb,
             *, rows_per_group):
    n, d = x.shape
    full = lambda shape: pl.BlockSpec(shape, lambda i: (0,) * len(shape))
    row = lambda w: pl.BlockSpec((ROW_TILE, w), lambda i: (i, 0))
    mod = lambda j: pl.BlockSpec((SUBLANES, d), lambda i: (0, j))
    return pl.pallas_call(
        functools.partial(_outproj_kernel, rows_per_group=rows_per_group),
        out_shape=(jax.ShapeDtypeStruct((n, d), F32), jax.ShapeDtypeStruct((n * (d // LANES), LANES), F32),
                   jax.ShapeDtypeStruct((n, LANES), F32)),
        grid=(n // ROW_TILE,),
        in_specs=[row(d), row(256), row(256), row(256), row(256), row(256), row(256), row(256),
                  full(ln.shape), full(b64.shape), full(wo.shape), mod(2), mod(3), mod(4),
                  full(g2.shape), full(rw_hi.shape), full(rw_lo.shape), full(rb.shape)],
        out_specs=(row(d), pl.BlockSpec((ROW_TILE * (d // LANES), LANES), lambda i: (i, 0)), row(LANES)),
        compiler_params=_cparams(("arbitrary",)),
        name="outproj",
    )(x, oa, ob, oc, y0, y1, bonus, gate, ln, b64, wo, mods, mods, mods, g2, rw_hi, rw_lo, rb)


TOKEN_TILES = 8


def _token_rows(token, count=1):
    start = token * TOKEN_TILES
    if not isinstance(token, int):
        start = pl.multiple_of(start, TOKEN_TILES)
    return pl.ds(start, count * TOKEN_TILES)


def _row_copy(src_hbm, token, dst, dst_row, sem):
    return pltpu.make_async_copy(src_hbm.at[_token_rows(token), :], dst.at[_token_rows(dst_row), :], sem)


def _issue_rows(n_rows, start_row):
    def body(j, carry):
        for u in range(DMA_UNROLL):
            start_row(j * DMA_UNROLL + u)
        return carry
    lax.fori_loop(0, n_rows // DMA_UNROLL, body, 0)


def _moe_kernel(be_ref, nused_ref, cur_ref, nxt_ref, nxt2_ref, h_hbm, w1_ref, w2_ref, b1_ref, b2_ref,
                out_ref, xbuf, w1b, w2xb, sem):
    i = pl.program_id(0)
    nused = nused_ref[0]
    slot = i % MOE_SLOTS
    ahead = (i + MOE_SLOTS - 1) % MOE_SLOTS

    def issue(idx_ref, sl):
        _issue_rows(MOE_ROWS, lambda r: _row_copy(h_hbm, idx_ref[0, 0, r], xbuf.at[sl], r, sem.at[sl]).start())

    def wait_slot(sl):
        pltpu.make_async_copy(h_hbm.at[_token_rows(0, MOE_ROWS), :], xbuf.at[sl], sem.at[sl]).wait()

    @pl.when(i == 0)
    def _():
        issue(cur_ref, 0)
        issue(nxt_ref, 1)

    @pl.when(i < nused)
    def _():
        wait_slot(slot)

        @pl.when(jnp.logical_or(i == 0, be_ref[i] != be_ref[jnp.maximum(i - 1, 0)]))
        def _():
            w1b[...] = w1_ref[0].astype(BF16)
            half = LANES // 2
            for p in range(w2_ref.shape[1] // LANES):
                xb = pltpu.bitcast(w2_ref[0, LANES * p:LANES * p + half, :].astype(BF16).astype(F32), jnp.uint32)
                yb = pltpu.bitcast(w2_ref[0, LANES * p + half:LANES * (p + 1), :].astype(BF16).astype(F32),
                                   jnp.uint32)
                word = lax.shift_right_logical(xb, jnp.uint32(16)) | (yb & jnp.uint32(0xFFFF0000))
                w2xb[LANES * p:LANES * (p + 1), :] = pltpu.bitcast(word, BF16)

        x = _load_token_tiles(xbuf, (slot,), MOE_ROWS, TOKEN_TILES).astype(BF16)
        for r in range(MOE_ROWS):
            _row_copy(h_hbm, nxt2_ref[0, 0, r], xbuf.at[ahead], r, sem.at[ahead]).start()
        hcat = jnp.dot(x, w1b[...], preferred_element_type=F32) + b1_ref[0]
        glu = jnp.minimum(hcat, SWIGLU_LIMIT)
        lin1 = jnp.clip(hcat, -SWIGLU_LIMIT, SWIGLU_LIMIT) + 1.0
        width = hcat.shape[1]
        spread = glu * jax.nn.sigmoid(SWIGLU_ALPHA * glu) * pltpu.roll(lin1, width - 1, 1)
        even = (lax.broadcasted_iota(jnp.int32, (MOE_ROWS, LANES), 1) & 1) == 0
        act = jnp.concatenate(
            [jnp.where(even, spread[:, LANES * a:LANES * (a + 1)],
                       pltpu.roll(spread[:, LANES * (a + 1):LANES * (a + 2)], 1, 1))
             for a in range(0, width // LANES, 2)], axis=1)
        _store_token_tiles(out_ref, _mm(act, w2xb[...]) + b2_ref[0])

    @pl.when(i == nused - 1)
    def _():
        wait_slot((i + 1) % MOE_SLOTS)
        wait_slot(ahead)

    @pl.when(i >= nused)
    def _():
        out_ref[...] = jnp.zeros_like(out_ref)


def _moe_experts(h2, block_e, nused, buf_tok, w1, w2, b1, b2, layer):
    tiles = TOKEN_TILES
    d = tiles * LANES
    n_blocks = block_e.shape[0]
    dff2 = w1.shape[3]
    idx = buf_tok.reshape(n_blocks, 1, MOE_ROWS)
    wspec = lambda s1, s2: pl.BlockSpec((None, 1, s1, s2), lambda i, be, nu: (layer, be[i], 0, 0))
    grid_spec = pltpu.PrefetchScalarGridSpec(
        num_scalar_prefetch=2,
        grid=(n_blocks,),
        in_specs=[
            pl.BlockSpec((1, 1, MOE_ROWS), lambda i, be, nu: (i, 0, 0), memory_space=pltpu.SMEM),
            pl.BlockSpec((1, 1, MOE_ROWS), lambda i, be, nu: (jnp.minimum(i + 1, n_blocks - 1), 0, 0),
                         memory_space=pltpu.SMEM),
            pl.BlockSpec((1, 1, MOE_ROWS), lambda i, be, nu: (jnp.minimum(i + 2, n_blocks - 1), 0, 0),
                         memory_space=pltpu.SMEM),
            pl.BlockSpec(memory_space=pl.ANY),
            wspec(d, dff2), wspec(dff2 // 2, d), wspec(1, dff2), wspec(1, d),
        ],
        out_specs=pl.BlockSpec((MOE_ROWS * tiles, LANES), lambda i, be, nu: (i, 0)),
        scratch_shapes=[pltpu.VMEM((MOE_SLOTS, MOE_ROWS * tiles, LANES), F32), pltpu.VMEM((d, dff2), BF16),
                        pltpu.VMEM((dff2 // 2, d), BF16), pltpu.SemaphoreType.DMA((MOE_SLOTS,))],
    )
    return pl.pallas_call(
        _moe_kernel,
        out_shape=jax.ShapeDtypeStruct((n_blocks * MOE_ROWS * tiles, LANES), F32),
        grid_spec=grid_spec,
        compiler_params=_cparams(("arbitrary",), vmem=MOE_VMEM_LIMIT),
        name="moe_experts",
    )(block_e, nused, idx, idx, idx, h2, w1, w2, b1, b2)


def _comb_kernel(cur_ref, nxt_ref, yb_hbm, x_ref, gates_ref, gate_ref, out_ref, buf, sem,
                 *, rows_per_group, n_steps):
    i = pl.program_id(0)
    slot = i % 2
    g = (i * COMB_ROWS) // rows_per_group

    def issue(idx_ref, sl):
        def start_token(t):
            for kx in range(TOP_K):
                _row_copy(yb_hbm, idx_ref[0, 0, t * TOP_K + kx], buf.at[sl, kx], t, sem.at[sl]).start(
                    priority=kx % 2)
        _issue_rows(COMB_ROWS, start_token)

    @pl.when(i == 0)
    def _():
        issue(cur_ref, 0)

    @pl.when(i + 1 < n_steps)
    def _():
        issue(nxt_ref, 1 - slot)

    for kx in range(TOP_K):
        pltpu.make_async_copy(yb_hbm.at[_token_rows(0, COMB_ROWS), :], buf.at[slot, kx], sem.at[slot]).wait()
    gates = gates_ref[...]
    acc = gates[:, 0:1] * _load_token_tiles(buf, (slot, 0), COMB_ROWS, TOKEN_TILES)
    for kx in range(1, TOP_K):
        acc = acc + gates[:, kx:kx + 1] * _load_token_tiles(buf, (slot, kx), COMB_ROWS, TOKEN_TILES)
    out_ref[...] = x_ref[...] + gate_ref[pl.ds(g, 1), :] * acc


def _moe_combine(yb, x1, gates, slot_dest, mods, *, rows_per_group):
    n, d = x1.shape
    steps = n // COMB_ROWS
    idx = slot_dest.reshape(steps, 1, COMB_ROWS * TOP_K)
    return pl.pallas_call(
        functools.partial(_comb_kernel, rows_per_group=rows_per_group, n_steps=steps),
        out_shape=jax.ShapeDtypeStruct((n, d), F32),
        grid=(steps,),
        in_specs=[
            pl.BlockSpec((1, 1, COMB_ROWS * TOP_K), lambda i: (i, 0, 0), memory_space=pltpu.SMEM),
            pl.BlockSpec((1, 1, COMB_ROWS * TOP_K), lambda i: (jnp.minimum(i + 1, steps - 1), 0, 0),
                         memory_space=pltpu.SMEM),
            pl.BlockSpec(memory_space=pl.ANY),
            pl.BlockSpec((COMB_ROWS, d), lambda i: (i, 0)),
            pl.BlockSpec((COMB_ROWS, TOP_K), lambda i: (i, 0)),
            pl.BlockSpec((SUBLANES, d), lambda i: (0, 5)),
        ],
        out_specs=pl.BlockSpec((COMB_ROWS, d), lambda i: (i, 0)),
        scratch_shapes=[pltpu.VMEM((2, TOP_K, COMB_ROWS * TOKEN_TILES, LANES), F32),
                        pltpu.SemaphoreType.DMA((2,))],
        compiler_params=_cparams(("arbitrary",)),
        name="moe_combine",
    )(idx, idx, yb, x1, gates, mods)


def _route(logits):
    n = logits.shape[0]
    top_val, top_idx = lax.top_k(logits, TOP_K)
    gates = jax.nn.softmax(top_val, axis=-1)
    flat_e = top_idx.reshape(-1).astype(jnp.int32)
    nk = n * TOP_K
    order = jnp.argsort(flat_e).astype(jnp.int32)
    experts = jnp.arange(N_EXPERTS, dtype=jnp.int32)
    picked = jnp.sum((top_idx[:, :, None] == experts[None, None, :]).astype(jnp.int32), axis=1)
    upto = jnp.cumsum(picked, axis=0)
    counts = upto[-1]
    within_group = jnp.take_along_axis(upto - picked, top_idx.astype(jnp.int32), axis=1).reshape(-1)
    padded = (counts + MOE_ROWS - 1) // MOE_ROWS * MOE_ROWS
    start = jnp.cumsum(counts) - counts
    pad_end = jnp.cumsum(padded)
    pad_start = pad_end - padded
    n_blocks = (nk + N_EXPERTS * (MOE_ROWS - 1) + MOE_ROWS - 1) // MOE_ROWS
    cap = n_blocks * MOE_ROWS
    block_start = jnp.arange(n_blocks, dtype=jnp.int32) * MOE_ROWS
    block_e = jnp.minimum(jnp.sum((pad_end[None, :] <= block_start[:, None]).astype(jnp.int32), axis=1),
                          N_EXPERTS - 1)
    nused = (pad_end[-1] // MOE_ROWS).astype(jnp.int32).reshape(1)
    row = jnp.arange(cap, dtype=jnp.int32)
    row_e = jnp.repeat(block_e, MOE_ROWS)
    within = row - pad_start[row_e]
    valid = within < counts[row_e]
    src = jnp.clip(start[row_e] + within, 0, nk - 1)
    buf_tok = jnp.where(valid, order[src] // TOP_K, 0).astype(jnp.int32)
    slot_dest = (pad_start[flat_e] + within_group).astype(jnp.int32)
    return gates, buf_tok, block_e, nused, slot_dest


def _block_ones(n, blk):
    idx = np.arange(n) // blk
    return jnp.asarray(idx[:, None] == idx[None, :], dtype=BF16)


def _pad_cols(w, width):
    return jnp.pad(w, ((0, 0), (0, width - w.shape[1])))


def _place_heads(w, per_head, offset, heads=MLA_HEADS, slot=MLA_PAD):
    rows = w.shape[0]
    w3 = w.reshape(rows, heads, per_head)
    out = jnp.zeros((rows, heads, slot), w.dtype).at[:, :, offset:offset + per_head].set(w3)
    return out.reshape(rows, heads * slot)


def _pack_in_weights(P, l):
    w_in = P["w_in"][l]
    sizes = (256, 128, 128, MLA_Q_LORA, MLA_KV_LORA, MLA_ROPE, 256, 256, 256, RWKV_COLS)
    offs = np.concatenate([[0], np.cumsum(sizes)])
    seg = [w_in[:, offs[j]:offs[j + 1]] for j in range(len(sizes))]
    krp = _place_heads(jnp.tile(seg[5], (1, MLA_HEADS)), MLA_ROPE, MLA_NOPE)
    w_all = jnp.concatenate([seg[0], seg[1], seg[2], _pad_cols(seg[3], 256), seg[4], krp,
                             seg[6], seg[7], seg[8], _pad_cols(seg[9], 1024)], axis=1).astype(BF16)
    q_up = _place_heads(P["mla_q_up"][l], MLA_QK, 0)
    q_up = jnp.pad(q_up, ((0, 256 - MLA_Q_LORA), (0, 0))).astype(BF16)
    kv_up = P["mla_kv_up"][l].reshape(MLA_KV_LORA, MLA_HEADS, MLA_NOPE + MLA_V)
    w_knope = _place_heads(kv_up[:, :, :MLA_NOPE].reshape(MLA_KV_LORA, -1), MLA_NOPE, 0).astype(BF16)
    w_v = kv_up[:, :, MLA_NOPE:].reshape(MLA_KV_LORA, MLA_HEADS * MLA_V).astype(BF16)
    width = MLA_HEADS * MLA_PAD

    def rowpad(v):
        return jnp.pad(v, (0, width - v.shape[0]))

    gains = jnp.stack([
        rowpad(jnp.tile(P["gqa_qn"][l], GQA_HEADS)),
        rowpad(jnp.tile(P["gqa_kn"][l], GQA_KV_HEADS)),
        rowpad(P["mla_qa_norm"][l]),
        rowpad(P["mla_kva_norm"][l]),
        _place_heads(jnp.tile(P["mla_qn"][l], MLA_HEADS)[None], MLA_QK, 0)[0],
        _place_heads(jnp.tile(P["mla_kn"][l], MLA_HEADS)[None], MLA_QK, 0)[0],
        rowpad(jnp.tile(P["diff_qn"][l], 2 * DIFF_HEADS)),
        rowpad(jnp.tile(P["diff_kn"][l], 2 * DIFF_HEADS)),
    ]).astype(F32)
    return dict(w_all=w_all, q_up=q_up, w_knope=w_knope, w_v=w_v, gains=gains,
                b64=_block_ones(256, 64), b32=_block_ones(256, 32), b128=_block_ones(width, MLA_PAD))


def _pack_rwkv(P, l):
    mix = _pad_cols(P["rwkv_mix"][l], 1024)
    w_lora = jnp.zeros((256, 1280), F32)
    w_lora = w_lora.at[0:32, 0:256].set(P["rwkv_w2"][l, 0]).at[32:64, 256:512].set(P["rwkv_w2"][l, 1])
    w_lora = w_lora.at[64:96, 512:768].set(P["rwkv_a2"][l, 0]).at[96:128, 768:1024].set(P["rwkv_a2"][l, 1])
    w_lora = w_lora.at[128:192, 1024:1280].set(P["rwkv_g2"][l]).astype(BF16)
    vec = jnp.zeros((SUBLANES, 512), F32)
    vec = vec.at[0].set(P["rwkv_w0"][l].reshape(-1)).at[1].set(P["rwkv_a0"][l].reshape(-1))
    vec = vec.at[2].set(jnp.concatenate([P["rwkv_kk"][l], P["rwkv_ka"][l]]))
    vec = vec.at[3].set(P["rwkv_rk"][l].reshape(-1))
    return dict(mix=mix, w_lora=w_lora, vec=vec, b64=_block_ones(256, 64))


def _rope_table(seq, rot, lanes_per_tile_group, lane_offset=0):
    rows = seq // GRID_W
    row = jnp.repeat(jnp.arange(rows, dtype=F32), GRID_W)
    col = (jnp.arange(rows * GRID_W) % GRID_W).astype(F32)
    n_freq = rot // 4
    inv_freq = ROPE_BASE ** (-jnp.arange(n_freq, dtype=F32) / n_freq)
    ang = jnp.stack([row[:, None] * inv_freq, col[:, None] * inv_freq], axis=1)
    cos = jnp.cos(ang)[:, :, None, :]
    sin = jnp.sin(ang)[:, :, None, :]
    cos_g = jnp.broadcast_to(cos, (seq, 2, 2, n_freq)).reshape(seq, rot)
    sin_g = (jnp.broadcast_to(sin, (seq, 2, 2, n_freq))
             * jnp.asarray([-1.0, 1.0], F32)[None, None, :, None]).reshape(seq, rot)
    cos_t = jnp.ones((seq, LANES), F32)
    sin_t = jnp.zeros((seq, LANES), F32)
    for start in range(lane_offset, LANES, lanes_per_tile_group):
        cos_t = cos_t.at[:, start:start + rot].set(cos_g)
        sin_t = sin_t.at[:, start:start + rot].set(sin_g)
    real = jnp.concatenate([cos_t, sin_t], axis=1)
    ident = jnp.concatenate([jnp.ones((seq, LANES), F32), jnp.zeros((seq, LANES), F32)], axis=1)
    return jnp.stack([real, ident])


def _to_heads(x, groups, seq, heads):
    d = x.shape[1] // heads
    return x.reshape(groups, seq, heads, d).transpose(0, 2, 1, 3).astype(BF16)


def _with_ones(v):
    return jnp.concatenate([v.astype(BF16), jnp.ones(v.shape, BF16)], axis=-1)


def kernel(x_prompt, x_sample, cache_gqa_k, cache_gqa_v, cache_mla_ckv, cache_mla_krope, cache_diff_k, cache_diff_v, state_rwkv, c, c_ctx, norm1_g, norm2_g, ada_w, ada_b, w_in, w_out, gqa_qn, gqa_kn, mla_qa_norm, mla_q_up, mla_kva_norm, mla_kv_up, mla_qn, mla_kn, diff_qn, diff_kn, diff_lam, diff_subln, rwkv_mix, rwkv_w0, rwkv_w2, rwkv_a0, rwkv_a2, rwkv_rk, rwkv_g2, rwkv_kk, rwkv_ka, rwkv_ln_g, rwkv_ln_b, router_w, router_b, moe_w1, moe_b1, moe_w2, moe_b2):
    P = dict(w_in=w_in, gqa_qn=gqa_qn, gqa_kn=gqa_kn, mla_qa_norm=mla_qa_norm, mla_q_up=mla_q_up,
             mla_kva_norm=mla_kva_norm, mla_kv_up=mla_kv_up, mla_qn=mla_qn, mla_kn=mla_kn,
             diff_qn=diff_qn, diff_kn=diff_kn, rwkv_mix=rwkv_mix, rwkv_w0=rwkv_w0, rwkv_w2=rwkv_w2,
             rwkv_a0=rwkv_a0, rwkv_a2=rwkv_a2, rwkv_rk=rwkv_rk, rwkv_g2=rwkv_g2, rwkv_kk=rwkv_kk,
             rwkv_ka=rwkv_ka)
    depth = norm1_g.shape[0]
    bc, tc, d = x_prompt.shape
    bl, tl, _ = x_sample.shape
    assert bc * tc == tl, "context tokens must fill exactly one latent-sequence group"
    assert tl % ROW_TILE == 0 and tc % ROW_TILE == 0 and tc % SCAN_CHUNK == 0
    n_lat = bl * tl
    n = n_lat + tl
    groups = bl + 1
    past = cache_gqa_k.shape[3]

    cond = jnp.concatenate([c, c_ctx[None], jnp.zeros((SUBLANES - groups, d), F32)], axis=0)
    mods_all = _adaln(cond, ada_w, ada_b)
    tables = dict(a=_rope_table(tl, HEAD_DIM, HEAD_DIM),
                  m=_rope_table(tl, MLA_ROPE, LANES, lane_offset=MLA_NOPE),
                  c=_rope_table(tl, DIFF_QK, DIFF_QK))
    x = jnp.concatenate([x_sample.reshape(n_lat, d), x_prompt.reshape(tl, d)], axis=0)
    ctx_rows = slice(n_lat, n)
    new = [[] for _ in range(7)]

    for l in range(depth):
        lam_init = 0.8 - 0.6 * math.exp(-0.3 * l)
        mods = mods_all[l]
        wpack = _pack_in_weights(P, l)
        rpack = _pack_rwkv(P, l)
        (qa_h, ka_h, va_h, qb_h, kb_h, vb_h, qc_h, kc_h, vc_h, ka, va, ckvn, kr, kc, vc, pd) = _inproj(
            x, mods, norm1_g[l][None], wpack, tables, rows_per_group=tl, n_lat=n_lat)

        new[0].append(ka[ctx_rows].reshape(bc, tc, GQA_KV_HEADS, HEAD_DIM).transpose(0, 2, 1, 3))
        new[1].append(va[ctx_rows].reshape(bc, tc, GQA_KV_HEADS, HEAD_DIM).transpose(0, 2, 1, 3))
        new[2].append(ckvn[ctx_rows].reshape(bc, tc, MLA_KV_LORA))
        new[3].append(kr[ctx_rows, MLA_NOPE:MLA_NOPE + MLA_ROPE].reshape(bc, tc, MLA_ROPE))
        new[4].append(kc[ctx_rows].reshape(bc, tc, DIFF_HEADS, 2, DIFF_QK).transpose(0, 2, 3, 1, 4))
        new[5].append(vc[ctx_rows].reshape(bc, tc, DIFF_HEADS, DIFF_V).transpose(0, 2, 1, 3))

        o_a = _attention(qa_h, ka_h, va_h, cache_gqa_k[:, l].astype(BF16), _with_ones(cache_gqa_v[:, l]),
                         n_req=bl, seq=tc, ctx_group=bl)
        krp_c = _place_heads(jnp.tile(cache_mla_krope[:, l].reshape(bl * past, MLA_ROPE), (1, MLA_HEADS)),
                             MLA_ROPE, MLA_NOPE)
        kb_c, vb_c = _mla_cache(cache_mla_ckv[:, l].reshape(bl * past, MLA_KV_LORA), krp_c, wpack)
        o_b = _attention(qb_h, kb_h, vb_h,
                         _to_heads(kb_c, bl, past, MLA_HEADS), _with_ones(_to_heads(vb_c, bl, past, MLA_HEADS)),
                         n_req=bl, seq=tc, ctx_group=bl)
        kc_c = cache_diff_k[:, l].transpose(0, 1, 3, 2, 4).reshape(bl, DIFF_HEADS, past, 2 * DIFF_QK)
        o_c = _attention(qc_h, kc_h, vc_h, kc_c.astype(BF16), _with_ones(cache_diff_v[:, l]),
                         n_req=bl, seq=tc, ctx_group=bl,
                         diff_params=(diff_lam[l], diff_subln[l][None]), lam_init=lam_init)

        names = ("r", "v", "kk", "e0", "e1", "k0", "k1", "b0", "b1", "g", "bonus")
        rw = dict(zip(names, _rwkv_prep(pd, rpack, n_lat=n_lat, lat_seq=tl, ctx_seq=tc)))
        streams = {k_: rw[k_] for k_ in names[:9]}
        h0_lat = jnp.swapaxes(state_rwkv[:, l], -1, -2)
        y0l, y1l, _ = _rwkv_scan(streams, h0_lat, first_seq=0, seq_len=tl)
        h0_ctx = jnp.zeros((bc, 2, RWKV_HEADS, RWKV_N, RWKV_N), F32)
        y0c, y1c, h_ctx = _rwkv_scan(streams, h0_ctx, first_seq=n_lat // tc, seq_len=tc)
        new[6].append(jnp.swapaxes(h_ctx, -1, -2))
        y0 = jnp.concatenate([y0l.reshape(n_lat, RWKV_W), y0c.reshape(tl, RWKV_W)], axis=0)
        y1 = jnp.concatenate([y1l.reshape(n_lat, RWKV_W), y1c.reshape(tl, RWKV_W)], axis=0)

        ln = jnp.stack([rwkv_ln_g[l], rwkv_ln_b[l]])
        rw_f = _pad_cols(router_w[l], LANES)
        rw_hi = rw_f.astype(BF16)
        rw_lo = (rw_f - rw_hi.astype(F32)).astype(BF16)
        rb = _pad_cols(router_b[l][None], LANES)
        x1, h2, logits = _outproj(x, o_a, o_b, o_c, y0, y1,
                                  rw["bonus"], rw["g"], ln, rpack["b64"], w_out[l].astype(BF16), mods,
                                  norm2_g[l][None], rw_hi, rw_lo, rb, rows_per_group=tl)

        gates, buf_tok, block_e, nused, slot_dest = _route(logits[:, :N_EXPERTS])
        yb = _moe_experts(h2, block_e, nused, buf_tok, moe_w1, moe_w2,
                          moe_b1[:, :, None, :], moe_b2[:, :, None, :], l)
        x = _moe_combine(yb, x1, gates, slot_dest, mods, rows_per_group=tl)

    y_sample = x[:n_lat].reshape(bl, tl, d)
    y_prompt = x[n_lat:].reshape(bc, tc, d)
    return (y_prompt, y_sample) + tuple(jnp.stack(t, axis=1) for t in new)
```

```python
import functools
import math

import jax
import jax.numpy as jnp
import numpy as np
from jax import lax
from jax.experimental import pallas as pl
from jax.experimental.pallas import tpu as pltpu

F32 = jnp.float32
BF16 = jnp.bfloat16

GRID_W = 64
ROPE_BASE = 10000.0
EPS = 1e-6
HEAD_DIM = 64
GQA_HEADS, GQA_KV_HEADS = 4, 2
MLA_HEADS, MLA_Q_LORA, MLA_KV_LORA, MLA_NOPE, MLA_ROPE, MLA_V = 4, 192, 128, 64, 32, 64
MLA_QK = MLA_NOPE + MLA_ROPE
MLA_PAD = 128
DIFF_HEADS, DIFF_QK, DIFF_V = 4, 32, 64
RWKV_HEADS, RWKV_N = 4, 64
RWKV_W = RWKV_HEADS * RWKV_N
DECAY_LORA, AAA_LORA, GATE_LORA = 32, 32, 64
RWKV_GN_EPS = 64e-5
RWKV_COLS = 3 * RWKV_W + 2 * DECAY_LORA + 2 * AAA_LORA + GATE_LORA
N_EXPERTS, TOP_K = 32, 4
SWIGLU_ALPHA, SWIGLU_LIMIT = 1.702, 7.0

LANES = 128
SUBLANES = 8
VMEM_LIMIT = 48 * 1024 * 1024
MOE_VMEM_LIMIT = 56 * 1024 * 1024
ROW_TILE = 256
Q_TILE = 256
KV_CHUNK = 512
SCAN_CHUNK = 64
SCAN_SUB = 16
SCAN_SEQS = 4
MOE_ROWS = 256
MOE_SLOTS = 3
GATHER_PRIORITY = 1
COMB_ROWS = 128
ADA_COLS = 1536
DMA_UNROLL = 8


def _cparams(sem, vmem=VMEM_LIMIT):
    return pltpu.CompilerParams(dimension_semantics=sem, vmem_limit_bytes=vmem)


def _mm(a, b):
    return jnp.dot(a.astype(BF16), b.astype(BF16), preferred_element_type=F32)


def _mm_nt(a, b):
    return lax.dot_general(a.astype(BF16), b.astype(BF16), (((1,), (1,)), ((), ())),
                           preferred_element_type=F32)


def _split2(x):
    hi = x.astype(BF16)
    lo = (x - hi.astype(F32)).astype(BF16)
    return hi, lo


def _split3(x):
    hi = x.astype(BF16)
    r = x - hi.astype(F32)
    mid = r.astype(BF16)
    lo = (r - mid.astype(F32)).astype(BF16)
    return hi, mid, lo


def _mm_x2(x, ones_blk):
    hi, lo = _split2(x)
    return (jnp.dot(hi, ones_blk, preferred_element_type=F32)
            + jnp.dot(lo, ones_blk, preferred_element_type=F32))


def _mm3(a, b):
    ah, al = _split2(a)
    bh, bl = _split2(b)
    return (jnp.dot(ah, bh, preferred_element_type=F32)
            + jnp.dot(ah, bl, preferred_element_type=F32)
            + jnp.dot(al, bh, preferred_element_type=F32))


def _rope(x, cos, sin_signed, rot):
    n = x.shape[-1]
    q = rot // 4
    lane = lax.broadcasted_iota(jnp.int32, x.shape, 1)
    first = (lane & (rot // 2 - 1)) < q
    partner = jnp.where(first, pltpu.roll(x, n - q, 1), pltpu.roll(x, q, 1))
    return x * cos + partner * sin_signed


def _tile_lanes(x, reps):
    return x if reps == 1 else jnp.concatenate([x] * reps, axis=1)


def _ada_kernel(c_ref, w_ref, b_ref, o_ref):
    c = c_ref[...]
    o_ref[0] = _mm(c * jax.nn.sigmoid(c), w_ref[0]) + b_ref[0]


def _adaln(cond, ada_w, ada_b):
    depth, d, cols = ada_w.shape
    rows = cond.shape[0]
    return pl.pallas_call(
        _ada_kernel,
        out_shape=jax.ShapeDtypeStruct((depth, rows, cols), F32),
        grid=(depth, cols // ADA_COLS),
        in_specs=[
            pl.BlockSpec((rows, d), lambda l, j: (0, 0)),
            pl.BlockSpec((1, d, ADA_COLS), lambda l, j: (l, 0, j)),
            pl.BlockSpec((1, 1, ADA_COLS), lambda l, j: (l, 0, j)),
        ],
        out_specs=pl.BlockSpec((1, rows, ADA_COLS), lambda l, j: (l, 0, j)),
        compiler_params=_cparams(("arbitrary", "arbitrary")),
        name="adaln",
    )(cond, ada_w, ada_b.reshape(depth, 1, cols))


_C_AQ, _C_AK, _C_AV = 0, 256, 384
_C_CQ, _C_CKV, _C_KRP = 512, 768, 896
_C_DQ, _C_DK, _C_DV = 1408, 1664, 1920
_C_PD = 2176
_C_END = 3200


def _mla_kv(ckvn, krp, w_knope, w_v, kn_gain, blk128, cos, sin):
    kpre = _mm(ckvn, w_knope) + krp
    ss = _mm_x2(kpre * kpre, blk128) * (1.0 / MLA_QK)
    kb = kpre * lax.rsqrt(ss + EPS) * kn_gain
    kb = _rope(kb, cos, sin, MLA_ROPE)
    return kb, _mm(ckvn, w_v)


def _store_heads(o_ref, x, width, ones=False):
    for h in range(o_ref.shape[1]):
        piece = x[:, width * h:width * (h + 1)]
        if ones:
            piece = jnp.concatenate([piece, jnp.ones_like(piece)], axis=1)
        o_ref[0, h] = piece.astype(o_ref.dtype)


def _inproj_kernel(x_ref, sh_ref, sc_ref, g1_ref, w_ref, qup_ref, wkn_ref, wv_ref, gains_ref,
                   b64_ref, b32_ref, b128_ref, ra_ref, rm_ref, rc_ref,
                   qa_h, ka_h, va_h, qb_h, kb_h, vb_h, qc_h, kc_h, vc_h,
                   ka_o, va_o, ckv_o, kr_o, kc_o, vc_o, pd_o,
                   *, rows_per_group):
    i = pl.program_id(0)
    g = (i * ROW_TILE) // rows_per_group
    x = x_ref[...]
    ms = jnp.mean(x * x, axis=-1, keepdims=True)
    xn = x * lax.rsqrt(ms + EPS) * g1_ref[...]
    h = xn * (1.0 + sc_ref[pl.ds(g, 1), :]) + sh_ref[pl.ds(g, 1), :]
    p = _mm(h, w_ref[...])
    gains = gains_ref[...]
    b64, b32, b128 = b64_ref[...], b32_ref[...], b128_ref[...]

    ra = ra_ref[0]
    cos_a, sin_a = ra[:, :LANES], ra[:, LANES:]
    aq = p[:, _C_AQ:_C_AK]
    ssq = _mm_x2(aq * aq, b64) * (1.0 / HEAD_DIM)
    qa = aq * lax.rsqrt(ssq + EPS) * gains[0:1, :256]
    qa = _rope(qa, _tile_lanes(cos_a, 2), _tile_lanes(sin_a, 2), HEAD_DIM)
    _store_heads(qa_h, qa * (HEAD_DIM ** -0.5), HEAD_DIM)
    ak = p[:, _C_AK:_C_AV]
    ssk = _mm_x2(ak * ak, b64[:LANES, :LANES]) * (1.0 / HEAD_DIM)
    ka = ak * lax.rsqrt(ssk + EPS) * gains[1:2, :128]
    ka = _rope(ka, cos_a, sin_a, HEAD_DIM)
    ka_o[...] = ka
    _store_heads(ka_h, ka, HEAD_DIM)
    va = p[:, _C_AV:_C_CQ]
    va_o[...] = va
    _store_heads(va_h, va, HEAD_DIM, ones=True)

    rm = rm_ref[0]
    cos_m, sin_m = _tile_lanes(rm[:, :LANES], MLA_HEADS), _tile_lanes(rm[:, LANES:], MLA_HEADS)
    cq = p[:, _C_CQ:_C_CKV]
    cqn = cq * lax.rsqrt(jnp.sum(cq * cq, axis=-1, keepdims=True) * (1.0 / MLA_Q_LORA) + EPS)
    qb = _mm(cqn * gains[2:3, :256], qup_ref[...])
    ssb = _mm_x2(qb * qb, b128) * (1.0 / MLA_QK)
    qb = qb * lax.rsqrt(ssb + EPS) * gains[4:5, :]
    _store_heads(qb_h, _rope(qb, cos_m, sin_m, MLA_ROPE) * (MLA_QK ** -0.5), MLA_PAD)
    ckv = p[:, _C_CKV:_C_KRP]
    ckvn = ckv * lax.rsqrt(jnp.mean(ckv * ckv, axis=-1, keepdims=True) + EPS) * gains[3:4, :128]
    ckv_o[...] = ckvn
    krp = p[:, _C_KRP:_C_DQ]
    kr_o[...] = krp[:, :LANES]
    kb, vb = _mla_kv(ckvn, krp, wkn_ref[...], wv_ref[...], gains[5:6, :], b128, cos_m, sin_m)
    _store_heads(kb_h, kb, MLA_PAD)
    _store_heads(vb_h, vb, MLA_V, ones=True)

    rc = rc_ref[0]
    cos_c, sin_c = _tile_lanes(rc[:, :LANES], 2), _tile_lanes(rc[:, LANES:], 2)
    dq = p[:, _C_DQ:_C_DK]
    ssd = _mm_x2(dq * dq, b32) * (1.0 / DIFF_QK)
    qc = dq * lax.rsqrt(ssd + EPS) * gains[6:7, :256]
    _store_heads(qc_h, _rope(qc, cos_c, sin_c, DIFF_QK) * (DIFF_QK ** -0.5), 2 * DIFF_QK)
    dk = p[:, _C_DK:_C_DV]
    ssd = _mm_x2(dk * dk, b32) * (1.0 / DIFF_QK)
    kc = dk * lax.rsqrt(ssd + EPS) * gains[7:8, :256]
    kc = _rope(kc, cos_c, sin_c, DIFF_QK)
    kc_o[...] = kc
    _store_heads(kc_h, kc, 2 * DIFF_QK)
    vc = p[:, _C_DV:_C_PD]
    vc_o[...] = vc
    _store_heads(vc_h, vc, DIFF_V, ones=True)

    pd_o[...] = p[:, _C_PD:_C_END]


def _inproj(x, mods, g1, wpack, tables, *, rows_per_group, n_lat):
    n, d = x.shape
    steps = n // ROW_TILE
    lat_steps = n_lat // ROW_TILE
    pos_steps = rows_per_group // ROW_TILE

    def rope_map(i):
        is_ctx = i >= lat_steps
        return (jnp.where(is_ctx, 1, 0), jnp.where(is_ctx, 0, i % pos_steps), 0)

    full = lambda shape: pl.BlockSpec(shape, lambda i: (0,) * len(shape))
    row = lambda w: pl.BlockSpec((ROW_TILE, w), lambda i: (i, 0))
    groups = n // rows_per_group
    head_shapes = ((GQA_HEADS, HEAD_DIM), (GQA_KV_HEADS, HEAD_DIM), (GQA_KV_HEADS, 2 * HEAD_DIM),
                   (MLA_HEADS, MLA_PAD), (MLA_HEADS, MLA_PAD), (MLA_HEADS, 2 * MLA_V),
                   (DIFF_HEADS, 2 * DIFF_QK), (DIFF_HEADS, 2 * DIFF_QK), (DIFF_HEADS, 2 * DIFF_V))
    heads = lambda hh, w: pl.BlockSpec((1, hh, ROW_TILE, w), lambda i: (i // pos_steps, 0, i % pos_steps, 0))
    widths = (128, 128, 128, 128, 256, 256, 1024)
    return pl.pallas_call(
        functools.partial(_inproj_kernel, rows_per_group=rows_per_group),
        out_shape=(tuple(jax.ShapeDtypeStruct((groups, hh, rows_per_group, w), BF16) for hh, w in head_shapes)
                   + tuple(jax.ShapeDtypeStruct((n, w), F32) for w in widths)),
        grid=(steps,),
        in_specs=[
            row(d),
            pl.BlockSpec((SUBLANES, d), lambda i: (0, 0)),
            pl.BlockSpec((SUBLANES, d), lambda i: (0, 1)),
            full((1, d)),
            full(wpack["w_all"].shape), full(wpack["q_up"].shape), full(wpack["w_knope"].shape),
            full(wpack["w_v"].shape), full(wpack["gains"].shape),
            full(wpack["b64"].shape), full(wpack["b32"].shape), full(wpack["b128"].shape),
            pl.BlockSpec((1, ROW_TILE, 2 * LANES), rope_map),
            pl.BlockSpec((1, ROW_TILE, 2 * LANES), rope_map),
            pl.BlockSpec((1, ROW_TILE, 2 * LANES), rope_map),
        ],
        out_specs=tuple(heads(hh, w) for hh, w in head_shapes) + tuple(row(w) for w in widths),
        compiler_params=_cparams(("arbitrary",)),
        name="inproj",
    )(x, mods, mods, g1, wpack["w_all"], wpack["q_up"], wpack["w_knope"], wpack["w_v"],
      wpack["gains"], wpack["b64"], wpack["b32"], wpack["b128"],
      tables["a"], tables["m"], tables["c"])


def _mla_cache_kernel(ckv_ref, krp_ref, wkn_ref, wv_ref, gains_ref, b128_ref, kb_o, vb_o):
    kn = gains_ref[...][5:6, :]
    one = jnp.ones((1, MLA_HEADS * MLA_PAD), F32)
    kb, vb = _mla_kv(ckv_ref[...], krp_ref[...], wkn_ref[...], wv_ref[...], kn, b128_ref[...],
                     one, jnp.zeros_like(one))
    kb_o[...] = kb
    vb_o[...] = vb


def _mla_cache(ckv, krp, wpack):
    rows = ckv.shape[0]
    tile = min(rows, ROW_TILE)
    full = lambda shape: pl.BlockSpec(shape, lambda i: (0,) * len(shape))
    row = lambda w: pl.BlockSpec((tile, w), lambda i: (i, 0))
    kw, vw = MLA_HEADS * MLA_PAD, MLA_HEADS * MLA_V
    return pl.pallas_call(
        _mla_cache_kernel,
        out_shape=(jax.ShapeDtypeStruct((rows, kw), F32), jax.ShapeDtypeStruct((rows, vw), F32)),
        grid=(rows // tile,),
        in_specs=[row(MLA_KV_LORA), row(kw), full(wpack["w_knope"].shape), full(wpack["w_v"].shape),
                  full(wpack["gains"].shape), full(wpack["b128"].shape)],
        out_specs=(row(kw), row(vw)),
        compiler_params=_cparams(("arbitrary",)),
        name="mla_cache",
    )(ckv, krp, wpack["w_knope"], wpack["w_v"], wpack["gains"], wpack["b128"])


def _softmax_streams(qs, kv_heads, segs, s_scr):
    pieces = []
    off = 0
    for k_ref, v_ref in segs:
        n_keys = k_ref.shape[2]
        step = min(KV_CHUNK, n_keys)
        for c0 in range(0, n_keys, step):
            pieces.append((k_ref, v_ref, c0, step, off + c0))
        off += n_keys
    row_max = []
    for j, q in enumerate(qs):
        m = None
        for k_ref, _, c0, step, col in pieces:
            s = _mm_nt(q, k_ref[0, kv_heads[j], c0:c0 + step, :])
            s_scr[j, :, col:col + step] = s
            for t0 in range(0, step, LANES):
                part = s[:, t0:t0 + LANES]
                m = part if m is None else jnp.maximum(m, part)
        row_max.append(jnp.max(m, axis=-1, keepdims=True))
    outs = []
    for j in range(len(qs)):
        acc = None
        for _, v_ref, c0, step, col in pieces:
            p = jnp.exp((s_scr[j, :, col:col + step] - row_max[j]).astype(BF16))
            pv = jnp.dot(p, v_ref[0, kv_heads[j], c0:c0 + step, :], preferred_element_type=F32)
            acc = pv if acc is None else acc + pv
        den = pltpu.roll(acc, HEAD_DIM, 1)
        outs.append((acc / den)[:, :HEAD_DIM])
    return outs


def _attn_kernel(*refs, has_cache, diff, lam_init, kv_heads):
    refs = list(refs)
    if diff:
        lam_ref, sub_ref = refs.pop(0), refs.pop(0)
    q_ref, k_ref, v_ref = refs[:3]
    o_ref, s_scr = refs[-2], refs[-1]
    segs = [(k_ref, v_ref)]
    if has_cache:
        segs.append((refs[3], refs[4]))
    if not diff:
        outs = _softmax_streams([q_ref[0, j] for j in range(q_ref.shape[1])], kv_heads, segs, s_scr)
        o_ref[...] = jnp.concatenate(outs, axis=1).astype(o_ref.dtype)
        return
    qs = []
    for j in range(q_ref.shape[1]):
        q = q_ref[0, j]
        lane = lax.broadcasted_iota(jnp.int32, q.shape, 1)
        zero = jnp.zeros_like(q)
        qs += [jnp.where(lane < DIFF_QK, q, zero), jnp.where(lane >= DIFF_QK, q, zero)]
    outs = _softmax_streams(qs, kv_heads, segs, s_scr)
    lv = lam_ref[...]
    lam = (jnp.exp(jnp.sum(lv[0:1] * lv[1:2], axis=-1, keepdims=True))
           - jnp.exp(jnp.sum(lv[2:3] * lv[3:4], axis=-1, keepdims=True)) + lam_init)
    heads = []
    for j in range(q_ref.shape[1]):
        o = outs[2 * j] - lam * outs[2 * j + 1]
        o = o * lax.rsqrt(jnp.mean(o * o, axis=-1, keepdims=True) + EPS) * sub_ref[...]
        heads.append(o * (1.0 - lam_init))
    o_ref[...] = jnp.concatenate(heads, axis=1).astype(o_ref.dtype)


def _attention(q, k, v, kc, vc, *, n_req, seq, ctx_group, diff_params=None, lam_init=0.0):
    g_all, hq, s_all, dq = q.shape
    hk, dk, dvp = k.shape[1], k.shape[3], v.shape[3]
    diff = diff_params is not None
    qh = 2
    if diff:
        kh, kv_heads = 2, (0, 0, 1, 1)
    elif hq == 2 * hk:
        kh, kv_heads = 1, (0, 0)
    else:
        kh, kv_heads = 2, (0, 1)
    n_streams = len(kv_heads)
    steps_h = hq // qh
    out_w = qh * HEAD_DIM
    q_tiles = s_all // Q_TILE
    extra_in, extra_specs2, extra_specs3 = [], [], []
    if diff:
        extra_in = [diff_params[0], diff_params[1]]
        extra_specs3 = [pl.BlockSpec(diff_params[0].shape, lambda b, h, i: (0, 0)),
                        pl.BlockSpec(diff_params[1].shape, lambda b, h, i: (0, 0))]
        extra_specs2 = [pl.BlockSpec(diff_params[0].shape, lambda b, h: (0, 0)),
                        pl.BlockSpec(diff_params[1].shape, lambda b, h: (0, 0))]
    past = kc.shape[2]
    body = functools.partial(_attn_kernel, diff=diff, lam_init=lam_init, kv_heads=kv_heads)
    lat = pl.pallas_call(
        functools.partial(body, has_cache=True),
        out_shape=jax.ShapeDtypeStruct((n_req * s_all, hq * HEAD_DIM), BF16),
        grid=(n_req, steps_h, q_tiles),
        in_specs=extra_specs3 + [
            pl.BlockSpec((1, qh, Q_TILE, dq), lambda b, h, i: (b, h, i, 0)),
            pl.BlockSpec((1, kh, s_all, dk), lambda b, h, i: (b, h, 0, 0)),
            pl.BlockSpec((1, kh, s_all, dvp), lambda b, h, i: (b, h, 0, 0)),
            pl.BlockSpec((1, kh, past, dk), lambda b, h, i: (b, h, 0, 0)),
            pl.BlockSpec((1, kh, past, dvp), lambda b, h, i: (b, h, 0, 0)),
        ],
        out_specs=pl.BlockSpec((Q_TILE, out_w), lambda b, h, i: (b * q_tiles + i, h)),
        scratch_shapes=[pltpu.VMEM((n_streams, Q_TILE, s_all + past), F32)],
        compiler_params=_cparams(("arbitrary", "arbitrary", "arbitrary")),
        name="attn_latent",
    )(*extra_in, q, k, v, kc, vc)
    n_seq = s_all // seq
    ctx = pl.pallas_call(
        functools.partial(body, has_cache=False),
        out_shape=jax.ShapeDtypeStruct((s_all, hq * HEAD_DIM), BF16),
        grid=(n_seq, steps_h),
        in_specs=extra_specs2 + [
            pl.BlockSpec((1, qh, seq, dq), lambda s, h: (ctx_group, h, s, 0)),
            pl.BlockSpec((1, kh, seq, dk), lambda s, h: (ctx_group, h, s, 0)),
            pl.BlockSpec((1, kh, seq, dvp), lambda s, h: (ctx_group, h, s, 0)),
        ],
        out_specs=pl.BlockSpec((seq, out_w), lambda s, h: (s, h)),
        scratch_shapes=[pltpu.VMEM((n_streams, seq, seq), F32)],
        compiler_params=_cparams(("arbitrary", "arbitrary")),
        name="attn_context",
    )(*extra_in, q, k, v)
    return jnp.concatenate([lat, ctx], axis=0)


def _rprep_kernel(p_ref, pp_ref, pn_ref, mix_ref, wl_ref, vec_ref, b64_ref,
                  r_o, v_o, kk_o, e0_o, e1_o, k0_o, k1_o, b0_o, b1_o, g_o, bon_o,
                  *, lat_steps, lat_seq_steps, ctx_seq_steps):
    i = pl.program_id(0)
    is_lat = i < lat_steps
    pos = jnp.where(is_lat, i % lat_seq_steps, (i - lat_steps) % ctx_seq_steps)
    last = jnp.where(is_lat, lat_seq_steps - 1, ctx_seq_steps - 1)
    at_start = pos == 0
    at_end = pos == last
    p = p_ref[...]
    rows = lax.broadcasted_iota(jnp.int32, p.shape, 0)
    prev_edge = jnp.where(at_start, 0.0, pp_ref[SUBLANES - 1:SUBLANES, :])
    next_edge = jnp.where(at_end, 0.0, pn_ref[0:1, :])
    prev = jnp.where(rows == 0, prev_edge, pltpu.roll(p, 1, 0))
    nxt = jnp.where(rows == ROW_TILE - 1, next_edge, pltpu.roll(p, ROW_TILE - 1, 0))
    mix = mix_ref[...]
    xs = p + (prev - p) * mix[0:1] + (nxt - p) * mix[1:2]
    r, k, v = xs[:, 0:256], xs[:, 256:512], xs[:, 512:768]
    lo = xs[:, 768:1024]
    lane = lax.broadcasted_iota(jnp.int32, lo.shape, 1)
    act = jnp.where(lane < 2 * DECAY_LORA, jnp.tanh(lo),
                    jnp.where(lane < 2 * (DECAY_LORA + AAA_LORA), lo, jax.nn.sigmoid(lo)))
    lora = _mm(act, wl_ref[...])
    vec = vec_ref[...]
    z = -(vec[0:1, :] + lora[:, 0:512])
    softplus = jnp.maximum(z, 0.0) + jnp.log1p(jnp.exp(-jnp.abs(z)))
    e = jnp.exp(-softplus - 0.5)
    a = jax.nn.sigmoid(vec[1:2, :] + lora[:, 512:1024])
    g_o[...] = lora[:, 1024:1280]
    b64 = b64_ref[...]
    kk = k * vec[2:3, 0:256]
    kk = kk * lax.rsqrt(_mm_x2(kk * kk, b64) + 1e-12)
    ka = vec[2:3, 256:512]
    r_o[...] = r
    v_o[...] = v
    kk_o[...] = kk
    bonus = None
    for d, (e_o, k_o, b_o) in enumerate(((e0_o, k0_o, b0_o), (e1_o, k1_o, b1_o))):
        a_d = a[:, 256 * d:256 * (d + 1)]
        k_d = k * (1.0 + (a_d - 1.0) * ka)
        e_o[...] = e[:, 256 * d:256 * (d + 1)]
        k_o[...] = k_d
        b_o[...] = kk * a_d
        term = _mm_x2(r * k_d * vec[3:4, 256 * d:256 * (d + 1)], b64) * v
        bonus = term if bonus is None else bonus + term
    bon_o[...] = bonus


def _rwkv_prep(pd, rpack, *, n_lat, lat_seq, ctx_seq):
    n, w = pd.shape
    steps = n // ROW_TILE
    halo_blocks = n // SUBLANES
    per = ROW_TILE // SUBLANES
    full = lambda shape: pl.BlockSpec(shape, lambda i: (0,) * len(shape))
    row = lambda width: pl.BlockSpec((ROW_TILE, width), lambda i: (i, 0))
    return pl.pallas_call(
        functools.partial(_rprep_kernel, lat_steps=n_lat // ROW_TILE,
                          lat_seq_steps=lat_seq // ROW_TILE, ctx_seq_steps=ctx_seq // ROW_TILE),
        out_shape=tuple(jax.ShapeDtypeStruct((n, RWKV_W), F32) for _ in range(11)),
        grid=(steps,),
        in_specs=[
            row(w),
            pl.BlockSpec((SUBLANES, w), lambda i: (jnp.maximum(i * per - 1, 0), 0)),
            pl.BlockSpec((SUBLANES, w), lambda i: (jnp.minimum((i + 1) * per, halo_blocks - 1), 0)),
            full(rpack["mix"].shape), full(rpack["w_lora"].shape), full(rpack["vec"].shape),
            full(rpack["b64"].shape),
        ],
        out_specs=tuple(row(RWKV_W) for _ in range(11)),
        compiler_params=_cparams(("arbitrary",)),
        name="rwkv_prep",
    )(pd, pd, pd, rpack["mix"], rpack["w_lora"], rpack["vec"], rpack["b64"])


def _bmm(a, b):
    return lax.dot_general(a.astype(BF16), b.astype(BF16), (((2,), (1,)), ((0,), (0,))),
                           preferred_element_type=F32)


def _bmm_nt(a, b):
    return lax.dot_general(a.astype(BF16), b.astype(BF16), (((2,), (2,)), ((0,), (0,))),
                           preferred_element_type=F32)


def _bmm3(a, b):
    ah, al = _split2(a)
    bh, bl = _split2(b)
    return _bmm(ah, bh) + _bmm(ah, bl) + _bmm(al, bh)


def _btranspose(x):
    return jnp.stack([x[i].T for i in range(x.shape[0])])


def _scan_chunks(r, v, kk, e, k, b, h0, rev):
    n, c, _ = r.shape
    ti = lax.broadcasted_iota(jnp.int32, (n, c, c), 1)
    si = lax.broadcasted_iota(jnp.int32, (n, c, c), 2)
    order = (si - ti) * jnp.where(rev, -1, 1)
    incl = order <= 0
    strict = order < 0
    e_hi, e_mid, e_lo = _split3(e)
    incl_b = jnp.where(incl, 1.0, 0.0).astype(BF16)
    cs = _bmm(incl_b, e_hi) + _bmm(incl_b, e_mid) + _bmm(incl_b, e_lo)
    g_prev = jnp.exp(e - cs)
    g_incl = jnp.exp(-cs)
    g_inv = jnp.exp(cs)
    kkg, rg, bq, kq = kk * g_prev, r * g_incl, b * g_inv, k * g_inv
    left = jnp.concatenate([kkg, rg], axis=1)
    sb = _bmm_nt(left, bq)
    sk = _bmm_nt(left, kq)
    zero = jnp.zeros((n, c, c), F32)
    a_m = jnp.where(strict, sb[:, :c], zero)
    b_m = jnp.where(strict, sk[:, :c], zero)
    a_p = jnp.where(incl, sb[:, c:], zero)
    b_p = jnp.where(incl, sk[:, c:], zero)

    eye = (ti == si).astype(F32)
    same = (ti // SCAN_SUB) == (si // SCAN_SUB)
    a_d = jnp.where(same, a_m, zero)
    a_off = a_m - a_d
    t_d = eye - a_d
    pw = _bmm3(a_d, a_d)
    steps = int(math.log2(SCAN_SUB))
    for j in range(1, steps):
        t_d = t_d + _bmm3(t_d, pw)
        if j + 1 < steps:
            pw = _bmm3(pw, pw)
    nil = _bmm3(t_d, a_off)
    inm = eye - nil
    nblk = c // SCAN_SUB
    if nblk > 2:
        nil2 = _bmm3(nil, nil)
        inm = inm + _bmm3(inm, nil2)
    t_full = _bmm3(inm, t_d) if nblk > 1 else t_d

    bmv = _bmm(b_m, v)
    w1 = _bmm(t_full, kkg)
    z = _bmm(t_full, bmv)
    cs_t = _btranspose(cs)
    tot_col = jnp.maximum(cs_t[:, :, 0:1], cs_t[:, :, c - 1:c])
    g_end_t = jnp.exp(cs_t - tot_col)
    kt_t = _btranspose(k) * g_end_t
    bt_t = _btranspose(b) * g_end_t
    u = _bmm(w1, h0) + z
    h_new = jnp.exp(-tot_col) * h0 + _bmm(kt_t, v) - _bmm(bt_t, u)
    y = _bmm(rg, h0) - _bmm(a_p, u) + _bmm(b_p, v)
    return y, h_new


def _scan_kernel(rf, vf, kkf, ef, kf, bf, rb, vb, kkb, eb, kb, bb, h0_ref,
                 y0_ref, y1_ref, hout_ref, h_scr, *, n_seq, n_chunks):
    c = pl.program_id(1)

    @pl.when(c == 0)
    def _():
        h_scr[...] = h0_ref[...]

    groups = ((rf, vf, kkf, ef, kf, bf), (rb, vb, kkb, eb, kb, bb))
    loaded = [[ref[...] for ref in refs] for refs in groups]
    ops = []
    for j in range(6):
        ops.append(jnp.stack([loaded[d][j][s, :, RWKV_N * h:RWKV_N * (h + 1)]
                              for s in range(n_seq) for d in range(2) for h in range(RWKV_HEADS)]))
    n = n_seq * 2 * RWKV_HEADS
    chain = lax.broadcasted_iota(jnp.int32, (n, 1, 1), 0)
    rev = ((chain // RWKV_HEADS) % 2) == 1
    h0 = h_scr[...].reshape(n, RWKV_N, RWKV_N)
    y, h_new = _scan_chunks(*ops, h0, rev)
    h_scr[...] = h_new.reshape(h_scr.shape)
    for s in range(n_seq):
        for d, y_ in enumerate((y0_ref, y1_ref)):
            base = (s * 2 + d) * RWKV_HEADS
            y_[s] = jnp.concatenate([y[base + h] for h in range(RWKV_HEADS)], axis=1)

    @pl.when(c == n_chunks - 1)
    def _():
        hout_ref[...] = h_scr[...]


def _rwkv_scan(streams, h0, *, first_seq, seq_len):
    n_chunks = seq_len // SCAN_CHUNK
    total = h0.shape[0]
    sb = min(SCAN_SEQS, total)
    assert total % sb == 0 and first_seq % sb == 0
    arrs = {name: a.reshape(-1, seq_len, RWKV_W) for name, a in streams.items()}
    blk0 = first_seq // sb
    blk = (sb, SCAN_CHUNK, RWKV_W)
    fwd = pl.BlockSpec(blk, lambda s, c: (blk0 + s, c, 0))
    bwd = pl.BlockSpec(blk, lambda s, c: (blk0 + s, n_chunks - 1 - c, 0))
    st_shape = (sb,) + h0.shape[1:]
    st = pl.BlockSpec(st_shape, lambda s, c: (s, 0, 0, 0, 0))
    y_shape = jax.ShapeDtypeStruct((total, seq_len, RWKV_W), F32)
    y0, y1, h_fin = pl.pallas_call(
        functools.partial(_scan_kernel, n_seq=sb, n_chunks=n_chunks),
        out_shape=(y_shape, y_shape, jax.ShapeDtypeStruct(h0.shape, F32)),
        grid=(total // sb, n_chunks),
        in_specs=[fwd] * 6 + [bwd] * 6 + [st],
        out_specs=(pl.BlockSpec(blk, lambda s, c: (s, c, 0)),
                   pl.BlockSpec(blk, lambda s, c: (s, n_chunks - 1 - c, 0)),
                   st),
        scratch_shapes=[pltpu.VMEM(st_shape, F32)],
        compiler_params=_cparams(("arbitrary", "arbitrary")),
        name="rwkv_scan",
    )(arrs["r"], arrs["v"], arrs["kk"], arrs["e0"], arrs["k0"], arrs["b0"],
      arrs["r"], arrs["v"], arrs["kk"], arrs["e1"], arrs["k1"], arrs["b1"], h0)
    return y0, y1, h_fin


def _store_token_tiles(ref, x):
    rows, width = x.shape
    tiles = width // LANES
    for j in range(tiles):
        ref[pl.ds(j, rows, stride=tiles), :] = x[:, LANES * j:LANES * (j + 1)]


def _load_token_tiles(ref, lead, rows, tiles):
    idx = tuple(lead)
    return jnp.concatenate([ref[idx + (pl.ds(j, rows, stride=tiles), slice(None))] for j in range(tiles)],
                           axis=1)


def _outproj_kernel(x_ref, oa_ref, ob_ref, oc_ref, y0_ref, y1_ref, bon_ref, g_ref, ln_ref, b64_ref,
                    wo_ref, gate_ref, sh_ref, sc_ref, g2_ref, rwh_ref, rwl_ref, rb_ref,
                    x1_o, h2_o, lg_o, *, rows_per_group):
    i = pl.program_id(0)
    g = (i * ROW_TILE) // rows_per_group
    b64 = b64_ref[...]
    y = y0_ref[...] + y1_ref[...]
    mu = _mm_x2(y, b64) * (1.0 / RWKV_N)
    dy = y - mu
    var = _mm_x2(dy * dy, b64) * (1.0 / RWKV_N)
    ln = ln_ref[...]
    yn = dy * lax.rsqrt(var + RWKV_GN_EPS) * ln[0:1] + ln[1:2]
    od = (yn + bon_ref[...]) * g_ref[...]
    wo = wo_ref[...]
    mixed = (_mm(oa_ref[...], wo[0:256]) + _mm(ob_ref[...], wo[256:512])
             + _mm(oc_ref[...], wo[512:768]) + _mm(od, wo[768:1024]))
    x1 = x_ref[...] + gate_ref[pl.ds(g, 1), :] * mixed
    x1_o[...] = x1
    ms = jnp.mean(x1 * x1, axis=-1, keepdims=True)
    h2 = (x1 * lax.rsqrt(ms + EPS) * g2_ref[...]) * (1.0 + sc_ref[pl.ds(g, 1), :]) + sh_ref[pl.ds(g, 1), :]
    _store_token_tiles(h2_o, h2)
    hh, hl = _split2(h2)
    rwh, rwl = rwh_ref[...], rwl_ref[...]
    lg_o[...] = (jnp.dot(hh, rwh, preferred_element_type=F32) + jnp.dot(hh, rwl, preferred_element_type=F32)
                 + jnp.dot(hl, rwh, preferred_element_type=F32)) + rb_ref[...]


def _outproj(x, oa, ob, oc, y0, y1, bonus, gate, ln, b64, wo, mods, g2, rw_hi, rw_lo, rb,
             *, rows_per_group):
    n, d = x.shape
    full = lambda shape: pl.BlockSpec(shape, lambda i: (0,) * len(shape))
    row = lambda w: pl.BlockSpec((ROW_TILE, w), lambda i: (i, 0))
    mod = lambda j: pl.BlockSpec((SUBLANES, d), lambda i: (0, j))
    return pl.pallas_call(
        functools.partial(_outproj_kernel, rows_per_group=rows_per_group),
        out_shape=(jax.ShapeDtypeStruct((n, d), F32), jax.ShapeDtypeStruct((n * (d // LANES), LANES), F32),
                   jax.ShapeDtypeStruct((n, LANES), F32)),
        grid=(n // ROW_TILE,),
        in_specs=[row(d), row(256), row(256), row(256), row(256), row(256), row(256), row(256),
                  full(ln.shape), full(b64.shape), full(wo.shape), mod(2), mod(3), mod(4),
                  full(g2.shape), full(rw_hi.shape), full(rw_lo.shape), full(rb.shape)],
        out_specs=(row(d), pl.BlockSpec((ROW_TILE * (d // LANES), LANES), lambda i: (i, 0)), row(LANES)),
        compiler_params=_cparams(("arbitrary",)),
        name="outproj",
    )(x, oa, ob, oc, y0, y1, bonus, gate, ln, b64, wo, mods, mods, mods, g2, rw_hi, rw_lo, rb)


TOKEN_TILES = 8


def _token_rows(token, count=1):
    start = token * TOKEN_TILES
    if not isinstance(token, int):
        start = pl.multiple_of(start, TOKEN_TILES)
    return pl.ds(start, count * TOKEN_TILES)


def _row_copy(src_hbm, token, dst, dst_row, sem):
    return pltpu.make_async_copy(src_hbm.at[_token_rows(token), :], dst.at[_token_rows(dst_row), :], sem)


def _issue_rows(n_rows, start_row):
    def body(j, carry):
        for u in range(DMA_UNROLL):
            start_row(j * DMA_UNROLL + u)
        return carry
    lax.fori_loop(0, n_rows // DMA_UNROLL, body, 0)


def _moe_kernel(be_ref, nused_ref, cur_ref, nxt_ref, nxt2_ref, h_hbm, w1_ref, w2_ref, b1_ref, b2_ref,
                out_ref, xbuf, w1b, w2xb, sem):
    i = pl.program_id(0)
    nused = nused_ref[0]
    slot = i % MOE_SLOTS
    ahead = (i + MOE_SLOTS - 1) % MOE_SLOTS

    def issue(idx_ref, sl):
        _issue_rows(MOE_ROWS, lambda r: _row_copy(h_hbm, idx_ref[0, 0, r], xbuf.at[sl], r, sem.at[sl]).start(
            priority=GATHER_PRIORITY))

    def wait_slot(sl):
        pltpu.make_async_copy(h_hbm.at[_token_rows(0, MOE_ROWS), :], xbuf.at[sl], sem.at[sl]).wait()

    @pl.when(i == 0)
    def _():
        issue(cur_ref, 0)
        issue(nxt_ref, 1)

    @pl.when(i < nused)
    def _():
        wait_slot(slot)

        @pl.when(jnp.logical_or(i == 0, be_ref[i] != be_ref[jnp.maximum(i - 1, 0)]))
        def _():
            w1b[...] = w1_ref[0].astype(BF16)
            half = LANES // 2
            for p in range(w2_ref.shape[1] // LANES):
                xb = pltpu.bitcast(w2_ref[0, LANES * p:LANES * p + half, :].astype(BF16).astype(F32), jnp.uint32)
                yb = pltpu.bitcast(w2_ref[0, LANES * p + half:LANES * (p + 1), :].astype(BF16).astype(F32),
                                   jnp.uint32)
                word = lax.shift_right_logical(xb, jnp.uint32(16)) | (yb & jnp.uint32(0xFFFF0000))
                w2xb[LANES * p:LANES * (p + 1), :] = pltpu.bitcast(word, BF16)

        x = _load_token_tiles(xbuf, (slot,), MOE_ROWS, TOKEN_TILES).astype(BF16)
        for r in range(MOE_ROWS):
            _row_copy(h_hbm, nxt2_ref[0, 0, r], xbuf.at[ahead], r, sem.at[ahead]).start(priority=GATHER_PRIORITY)
        hcat = jnp.dot(x, w1b[...], preferred_element_type=F32) + b1_ref[0]
        glu = jnp.minimum(hcat, SWIGLU_LIMIT)
        lin1 = jnp.clip(hcat, -SWIGLU_LIMIT, SWIGLU_LIMIT) + 1.0
        width = hcat.shape[1]
        spread = glu * jax.nn.sigmoid(SWIGLU_ALPHA * glu) * pltpu.roll(lin1, width - 1, 1)
        even = (lax.broadcasted_iota(jnp.int32, (MOE_ROWS, LANES), 1) & 1) == 0
        act = jnp.concatenate(
            [jnp.where(even, spread[:, LANES * a:LANES * (a + 1)],
                       pltpu.roll(spread[:, LANES * (a + 1):LANES * (a + 2)], 1, 1))
             for a in range(0, width // LANES, 2)], axis=1)
        _store_token_tiles(out_ref, _mm(act, w2xb[...]) + b2_ref[0])

    @pl.when(i == nused - 1)
    def _():
        wait_slot((i + 1) % MOE_SLOTS)
        wait_slot(ahead)

    @pl.when(i >= nused)
    def _():
        out_ref[...] = jnp.zeros_like(out_ref)


def _moe_experts(h2, block_e, nused, buf_tok, w1, w2, b1, b2, layer):
    tiles = TOKEN_TILES
    d = tiles * LANES
    n_blocks = block_e.shape[0]
    dff2 = w1.shape[3]
    idx = buf_tok.reshape(n_blocks, 1, MOE_ROWS)
    wspec = lambda s1, s2: pl.BlockSpec((None, 1, s1, s2), lambda i, be, nu: (layer, be[i], 0, 0))
    grid_spec = pltpu.PrefetchScalarGridSpec(
        num_scalar_prefetch=2,
        grid=(n_blocks,),
        in_specs=[
            pl.BlockSpec((1, 1, MOE_ROWS), lambda i, be, nu: (i, 0, 0), memory_space=pltpu.SMEM),
            pl.BlockSpec((1, 1, MOE_ROWS), lambda i, be, nu: (jnp.minimum(i + 1, n_blocks - 1), 0, 0),
                         memory_space=pltpu.SMEM),
            pl.BlockSpec((1, 1, MOE_ROWS), lambda i, be, nu: (jnp.minimum(i + 2, n_blocks - 1), 0, 0),
                         memory_space=pltpu.SMEM),
            pl.BlockSpec(memory_space=pl.ANY),
            wspec(d, dff2), wspec(dff2 // 2, d), wspec(1, dff2), wspec(1, d),
        ],
        out_specs=pl.BlockSpec((MOE_ROWS * tiles, LANES), lambda i, be, nu: (i, 0)),
        scratch_shapes=[pltpu.VMEM((MOE_SLOTS, MOE_ROWS * tiles, LANES), F32), pltpu.VMEM((d, dff2), BF16),
                        pltpu.VMEM((dff2 // 2, d), BF16), pltpu.SemaphoreType.DMA((MOE_SLOTS,))],
    )
    return pl.pallas_call(
        _moe_kernel,
        out_shape=jax.ShapeDtypeStruct((n_blocks * MOE_ROWS * tiles, LANES), F32),
        grid_spec=grid_spec,
        compiler_params=_cparams(("arbitrary",), vmem=MOE_VMEM_LIMIT),
        name="moe_experts",
    )(block_e, nused, idx, idx, idx, h2, w1, w2, b1, b2)


def _comb_kernel(cur_ref, nxt_ref, yb_hbm, x_ref, gates_ref, gate_ref, out_ref, buf, sem,
                 *, rows_per_group, n_steps):
    i = pl.program_id(0)
    slot = i % 2
    g = (i * COMB_ROWS) // rows_per_group

    def issue(idx_ref, sl):
        def start_token(t):
            for kx in range(TOP_K):
                _row_copy(yb_hbm, idx_ref[0, 0, t * TOP_K + kx], buf.at[sl, kx], t, sem.at[sl]).start(
                    priority=kx % 2)
        _issue_rows(COMB_ROWS, start_token)

    @pl.when(i == 0)
    def _():
        issue(cur_ref, 0)

    @pl.when(i + 1 < n_steps)
    def _():
        issue(nxt_ref, 1 - slot)

    for kx in range(TOP_K):
        pltpu.make_async_copy(yb_hbm.at[_token_rows(0, COMB_ROWS), :], buf.at[slot, kx], sem.at[slot]).wait()
    gates = gates_ref[...]
    acc = gates[:, 0:1] * _load_token_tiles(buf, (slot, 0), COMB_ROWS, TOKEN_TILES)
    for kx in range(1, TOP_K):
        acc = acc + gates[:, kx:kx + 1] * _load_token_tiles(buf, (slot, kx), COMB_ROWS, TOKEN_TILES)
    out_ref[...] = x_ref[...] + gate_ref[pl.ds(g, 1), :] * acc


def _moe_combine(yb, x1, gates, slot_dest, mods, *, rows_per_group):
    n, d = x1.shape
    steps = n // COMB_ROWS
    idx = slot_dest.reshape(steps, 1, COMB_ROWS * TOP_K)
    return pl.pallas_call(
        functools.partial(_comb_kernel, rows_per_group=rows_per_group, n_steps=steps),
        out_shape=jax.ShapeDtypeStruct((n, d), F32),
        grid=(steps,),
        in_specs=[
            pl.BlockSpec((1, 1, COMB_ROWS * TOP_K), lambda i: (i, 0, 0), memory_space=pltpu.SMEM),
            pl.BlockSpec((1, 1, COMB_ROWS * TOP_K), lambda i: (jnp.minimum(i + 1, steps - 1), 0, 0),
                         memory_space=pltpu.SMEM),
            pl.BlockSpec(memory_space=pl.ANY),
            pl.BlockSpec((COMB_ROWS, d), lambda i: (i, 0)),
            pl.BlockSpec((COMB_ROWS, TOP_K), lambda i: (i, 0)),
            pl.BlockSpec((SUBLANES, d), lambda i: (0, 5)),
        ],
        out_specs=pl.BlockSpec((COMB_ROWS, d), lambda i: (i, 0)),
        scratch_shapes=[pltpu.VMEM((2, TOP_K, COMB_ROWS * TOKEN_TILES, LANES), F32),
                        pltpu.SemaphoreType.DMA((2,))],
        compiler_params=_cparams(("arbitrary",)),
        name="moe_combine",
    )(idx, idx, yb, x1, gates, mods)


def _route(logits):
    n = logits.shape[0]
    top_val, top_idx = lax.top_k(logits, TOP_K)
    gates = jax.nn.softmax(top_val, axis=-1)
    flat_e = top_idx.reshape(-1).astype(jnp.int32)
    nk = n * TOP_K
    order = jnp.argsort(flat_e).astype(jnp.int32)
    rank = jnp.argsort(order).astype(jnp.int32)
    experts = jnp.arange(N_EXPERTS, dtype=jnp.int32)
    counts = jnp.sum((flat_e[:, None] == experts[None, :]).astype(jnp.int32), axis=0)
    padded = (counts + MOE_ROWS - 1) // MOE_ROWS * MOE_ROWS
    start = jnp.cumsum(counts) - counts
    pad_end = jnp.cumsum(padded)
    pad_start = pad_end - padded
    n_blocks = (nk + N_EXPERTS * (MOE_ROWS - 1) + MOE_ROWS - 1) // MOE_ROWS
    cap = n_blocks * MOE_ROWS
    block_start = jnp.arange(n_blocks, dtype=jnp.int32) * MOE_ROWS
    block_e = jnp.minimum(jnp.sum((pad_end[None, :] <= block_start[:, None]).astype(jnp.int32), axis=1),
                          N_EXPERTS - 1)
    nused = (pad_end[-1] // MOE_ROWS).astype(jnp.int32).reshape(1)
    row = jnp.arange(cap, dtype=jnp.int32)
    row_e = jnp.repeat(block_e, MOE_ROWS)
    within = row - pad_start[row_e]
    valid = within < counts[row_e]
    src = jnp.clip(start[row_e] + within, 0, nk - 1)
    buf_tok = jnp.where(valid, order[src] // TOP_K, 0).astype(jnp.int32)
    slot_dest = (pad_start[flat_e] + rank - start[flat_e]).astype(jnp.int32)
    return gates, buf_tok, block_e, nused, slot_dest


def _block_ones(n, blk):
    idx = np.arange(n) // blk
    return jnp.asarray(idx[:, None] == idx[None, :], dtype=BF16)


def _pad_cols(w, width):
    return jnp.pad(w, ((0, 0), (0, width - w.shape[1])))


def _place_heads(w, per_head, offset, heads=MLA_HEADS, slot=MLA_PAD):
    rows = w.shape[0]
    w3 = w.reshape(rows, heads, per_head)
    out = jnp.zeros((rows, heads, slot), w.dtype).at[:, :, offset:offset + per_head].set(w3)
    return out.reshape(rows, heads * slot)


def _pack_in_weights(P, l):
    w_in = P["w_in"][l]
    sizes = (256, 128, 128, MLA_Q_LORA, MLA_KV_LORA, MLA_ROPE, 256, 256, 256, RWKV_COLS)
    offs = np.concatenate([[0], np.cumsum(sizes)])
    seg = [w_in[:, offs[j]:offs[j + 1]] for j in range(len(sizes))]
    krp = _place_heads(jnp.tile(seg[5], (1, MLA_HEADS)), MLA_ROPE, MLA_NOPE)
    w_all = jnp.concatenate([seg[0], seg[1], seg[2], _pad_cols(seg[3], 256), seg[4], krp,
                             seg[6], seg[7], seg[8], _pad_cols(seg[9], 1024)], axis=1).astype(BF16)
    q_up = _place_heads(P["mla_q_up"][l], MLA_QK, 0)
    q_up = jnp.pad(q_up, ((0, 256 - MLA_Q_LORA), (0, 0))).astype(BF16)
    kv_up = P["mla_kv_up"][l].reshape(MLA_KV_LORA, MLA_HEADS, MLA_NOPE + MLA_V)
    w_knope = _place_heads(kv_up[:, :, :MLA_NOPE].reshape(MLA_KV_LORA, -1), MLA_NOPE, 0).astype(BF16)
    w_v = kv_up[:, :, MLA_NOPE:].reshape(MLA_KV_LORA, MLA_HEADS * MLA_V).astype(BF16)
    width = MLA_HEADS * MLA_PAD

    def rowpad(v):
        return jnp.pad(v, (0, width - v.shape[0]))

    gains = jnp.stack([
        rowpad(jnp.tile(P["gqa_qn"][l], GQA_HEADS)),
        rowpad(jnp.tile(P["gqa_kn"][l], GQA_KV_HEADS)),
        rowpad(P["mla_qa_norm"][l]),
        rowpad(P["mla_kva_norm"][l]),
        _place_heads(jnp.tile(P["mla_qn"][l], MLA_HEADS)[None], MLA_QK, 0)[0],
        _place_heads(jnp.tile(P["mla_kn"][l], MLA_HEADS)[None], MLA_QK, 0)[0],
        rowpad(jnp.tile(P["diff_qn"][l], 2 * DIFF_HEADS)),
        rowpad(jnp.tile(P["diff_kn"][l], 2 * DIFF_HEADS)),
    ]).astype(F32)
    return dict(w_all=w_all, q_up=q_up, w_knope=w_knope, w_v=w_v, gains=gains,
                b64=_block_ones(256, 64), b32=_block_ones(256, 32), b128=_block_ones(width, MLA_PAD))


def _pack_rwkv(P, l):
    mix = _pad_cols(P["rwkv_mix"][l], 1024)
    w_lora = jnp.zeros((256, 1280), F32)
    w_lora = w_lora.at[0:32, 0:256].set(P["rwkv_w2"][l, 0]).at[32:64, 256:512].set(P["rwkv_w2"][l, 1])
    w_lora = w_lora.at[64:96, 512:768].set(P["rwkv_a2"][l, 0]).at[96:128, 768:1024].set(P["rwkv_a2"][l, 1])
    w_lora = w_lora.at[128:192, 1024:1280].set(P["rwkv_g2"][l]).astype(BF16)
    vec = jnp.zeros((SUBLANES, 512), F32)
    vec = vec.at[0].set(P["rwkv_w0"][l].reshape(-1)).at[1].set(P["rwkv_a0"][l].reshape(-1))
    vec = vec.at[2].set(jnp.concatenate([P["rwkv_kk"][l], P["rwkv_ka"][l]]))
    vec = vec.at[3].set(P["rwkv_rk"][l].reshape(-1))
    return dict(mix=mix, w_lora=w_lora, vec=vec, b64=_block_ones(256, 64))


def _rope_table(seq, rot, lanes_per_tile_group, lane_offset=0):
    rows = seq // GRID_W
    row = jnp.repeat(jnp.arange(rows, dtype=F32), GRID_W)
    col = (jnp.arange(rows * GRID_W) % GRID_W).astype(F32)
    n_freq = rot // 4
    inv_freq = ROPE_BASE ** (-jnp.arange(n_freq, dtype=F32) / n_freq)
    ang = jnp.stack([row[:, None] * inv_freq, col[:, None] * inv_freq], axis=1)
    cos = jnp.cos(ang)[:, :, None, :]
    sin = jnp.sin(ang)[:, :, None, :]
    cos_g = jnp.broadcast_to(cos, (seq, 2, 2, n_freq)).reshape(seq, rot)
    sin_g = (jnp.broadcast_to(sin, (seq, 2, 2, n_freq))
             * jnp.asarray([-1.0, 1.0], F32)[None, None, :, None]).reshape(seq, rot)
    cos_t = jnp.ones((seq, LANES), F32)
    sin_t = jnp.zeros((seq, LANES), F32)
    for start in range(lane_offset, LANES, lanes_per_tile_group):
        cos_t = cos_t.at[:, start:start + rot].set(cos_g)
        sin_t = sin_t.at[:, start:start + rot].set(sin_g)
    real = jnp.concatenate([cos_t, sin_t], axis=1)
    ident = jnp.concatenate([jnp.ones((seq, LANES), F32), jnp.zeros((seq, LANES), F32)], axis=1)
    return jnp.stack([real, ident])


def _to_heads(x, groups, seq, heads):
    d = x.shape[1] // heads
    return x.reshape(groups, seq, heads, d).transpose(0, 2, 1, 3).astype(BF16)


def _with_ones(v):
    return jnp.concatenate([v.astype(BF16), jnp.ones(v.shape, BF16)], axis=-1)


def kernel(x_prompt, x_sample, cache_gqa_k, cache_gqa_v, cache_mla_ckv, cache_mla_krope, cache_diff_k, cache_diff_v, state_rwkv, c, c_ctx, norm1_g, norm2_g, ada_w, ada_b, w_in, w_out, gqa_qn, gqa_kn, mla_qa_norm, mla_q_up, mla_kva_norm, mla_kv_up, mla_qn, mla_kn, diff_qn, diff_kn, diff_lam, diff_subln, rwkv_mix, rwkv_w0, rwkv_w2, rwkv_a0, rwkv_a2, rwkv_rk, rwkv_g2, rwkv_kk, rwkv_ka, rwkv_ln_g, rwkv_ln_b, router_w, router_b, moe_w1, moe_b1, moe_w2, moe_b2):
    P = dict(w_in=w_in, gqa_qn=gqa_qn, gqa_kn=gqa_kn, mla_qa_norm=mla_qa_norm, mla_q_up=mla_q_up,
             mla_kva_norm=mla_kva_norm, mla_kv_up=mla_kv_up, mla_qn=mla_qn, mla_kn=mla_kn,
             diff_qn=diff_qn, diff_kn=diff_kn, rwkv_mix=rwkv_mix, rwkv_w0=rwkv_w0, rwkv_w2=rwkv_w2,
             rwkv_a0=rwkv_a0, rwkv_a2=rwkv_a2, rwkv_rk=rwkv_rk, rwkv_g2=rwkv_g2, rwkv_kk=rwkv_kk,
             rwkv_ka=rwkv_ka)
    depth = norm1_g.shape[0]
    bc, tc, d = x_prompt.shape
    bl, tl, _ = x_sample.shape
    assert bc * tc == tl, "context tokens must fill exactly one latent-sequence group"
    assert tl % ROW_TILE == 0 and tc % ROW_TILE == 0 and tc % SCAN_CHUNK == 0
    n_lat = bl * tl
    n = n_lat + tl
    groups = bl + 1
    past = cache_gqa_k.shape[3]

    cond = jnp.concatenate([c, c_ctx[None], jnp.zeros((SUBLANES - groups, d), F32)], axis=0)
    mods_all = _adaln(cond, ada_w, ada_b)
    tables = dict(a=_rope_table(tl, HEAD_DIM, HEAD_DIM),
                  m=_rope_table(tl, MLA_ROPE, LANES, lane_offset=MLA_NOPE),
                  c=_rope_table(tl, DIFF_QK, DIFF_QK))
    x = jnp.concatenate([x_sample.reshape(n_lat, d), x_prompt.reshape(tl, d)], axis=0)
    ctx_rows = slice(n_lat, n)
    new = [[] for _ in range(7)]

    for l in range(depth):
        lam_init = 0.8 - 0.6 * math.exp(-0.3 * l)
        mods = mods_all[l]
        wpack = _pack_in_weights(P, l)
        rpack = _pack_rwkv(P, l)
        (qa_h, ka_h, va_h, qb_h, kb_h, vb_h, qc_h, kc_h, vc_h, ka, va, ckvn, kr, kc, vc, pd) = _inproj(
            x, mods, norm1_g[l][None], wpack, tables, rows_per_group=tl, n_lat=n_lat)

        new[0].append(ka[ctx_rows].reshape(bc, tc, GQA_KV_HEADS, HEAD_DIM).transpose(0, 2, 1, 3))
        new[1].append(va[ctx_rows].reshape(bc, tc, GQA_KV_HEADS, HEAD_DIM).transpose(0, 2, 1, 3))
        new[2].append(ckvn[ctx_rows].reshape(bc, tc, MLA_KV_LORA))
        new[3].append(kr[ctx_rows, MLA_NOPE:MLA_NOPE + MLA_ROPE].reshape(bc, tc, MLA_ROPE))
        new[4].append(kc[ctx_rows].reshape(bc, tc, DIFF_HEADS, 2, DIFF_QK).transpose(0, 2, 3, 1, 4))
        new[5].append(vc[ctx_rows].reshape(bc, tc, DIFF_HEADS, DIFF_V).transpose(0, 2, 1, 3))

        o_a = _attention(qa_h, ka_h, va_h, cache_gqa_k[:, l].astype(BF16), _with_ones(cache_gqa_v[:, l]),
                         n_req=bl, seq=tc, ctx_group=bl)
        krp_c = _place_heads(jnp.tile(cache_mla_krope[:, l].reshape(bl * past, MLA_ROPE), (1, MLA_HEADS)),
                             MLA_ROPE, MLA_NOPE)
        kb_c, vb_c = _mla_cache(cache_mla_ckv[:, l].reshape(bl * past, MLA_KV_LORA), krp_c, wpack)
        o_b = _attention(qb_h, kb_h, vb_h,
                         _to_heads(kb_c, bl, past, MLA_HEADS), _with_ones(_to_heads(vb_c, bl, past, MLA_HEADS)),
                         n_req=bl, seq=tc, ctx_group=bl)
        kc_c = cache_diff_k[:, l].transpose(0, 1, 3, 2, 4).reshape(bl, DIFF_HEADS, past, 2 * DIFF_QK)
        o_c = _attention(qc_h, kc_h, vc_h, kc_c.astype(BF16), _with_ones(cache_diff_v[:, l]),
                         n_req=bl, seq=tc, ctx_group=bl,
                         diff_params=(diff_lam[l], diff_subln[l][None]), lam_init=lam_init)

        names = ("r", "v", "kk", "e0", "e1", "k0", "k1", "b0", "b1", "g", "bonus")
        rw = dict(zip(names, _rwkv_prep(pd, rpack, n_lat=n_lat, lat_seq=tl, ctx_seq=tc)))
        streams = {k_: rw[k_] for k_ in names[:9]}
        h0_lat = jnp.swapaxes(state_rwkv[:, l], -1, -2)
        y0l, y1l, _ = _rwkv_scan(streams, h0_lat, first_seq=0, seq_len=tl)
        h0_ctx = jnp.zeros((bc, 2, RWKV_HEADS, RWKV_N, RWKV_N), F32)
        y0c, y1c, h_ctx = _rwkv_scan(streams, h0_ctx, first_seq=n_lat // tc, seq_len=tc)
        new[6].append(jnp.swapaxes(h_ctx, -1, -2))
        y0 = jnp.concatenate([y0l.reshape(n_lat, RWKV_W), y0c.reshape(tl, RWKV_W)], axis=0)
        y1 = jnp.concatenate([y1l.reshape(n_lat, RWKV_W), y1c.reshape(tl, RWKV_W)], axis=0)

        ln = jnp.stack([rwkv_ln_g[l], rwkv_ln_b[l]])
        rw_f = _pad_cols(router_w[l], LANES)
        rw_hi = rw_f.astype(BF16)
        rw_lo = (rw_f - rw_hi.astype(F32)).astype(BF16)
        rb = _pad_cols(router_b[l][None], LANES)
        x1, h2, logits = _outproj(x, o_a, o_b, o_c, y0, y1,
                                  rw["bonus"], rw["g"], ln, rpack["b64"], w_out[l].astype(BF16), mods,
                                  norm2_g[l][None], rw_hi, rw_lo, rb, rows_per_group=tl)

        gates, buf_tok, block_e, nused, slot_dest = _route(logits[:, :N_EXPERTS])
        yb = _moe_experts(h2, block_e, nused, buf_tok, moe_w1, moe_w2,
                          moe_b1[:, :, None, :], moe_b2[:, :, None, :], l)
        x = _moe_combine(yb, x1, gates, slot_dest, mods, rows_per_group=tl)

    y_sample = x[:n_lat].reshape(bl, tl, d)
    y_prompt = x[n_lat:].reshape(bc, tc, d)
    return (y_prompt, y_sample) + tuple(jnp.stack(t, axis=1) for t in new)
```
